```python
import math
import jax
import jax.numpy as jnp
from jax import lax
import numpy as np

D_MODEL = 4096
BATCH = 2
SEQ = 8192
DEPTH = 2

HEAD_DIM = 128
D_MIX = D_MODEL
GROUP_W = D_MIX // 4

A_HEADS = GROUP_W // HEAD_DIM
A_KV_HEADS = 2
IDX_HEADS = A_HEADS // 2
IDX_DIM = 128
DSA_TOPK = 256
DSA_Q_BLOCK = 128

GLA_HEADS = 4
GLA_DV = GROUP_W // GLA_HEADS
GLA_DK = GLA_DV // 2
GLA_GATE_RANK = 16
GLA_TAU = 16.0
GLA_CHUNK = 64

GDN_HEADS = GROUP_W // HEAD_DIM
GDN_DK = HEAD_DIM
GDN_DV = HEAD_DIM
CONV_K = 4
GDN_CHUNK = 64

MOBA_HEADS = GROUP_W // HEAD_DIM
MOBA_BLOCK = 256
MOBA_TOPK = 3
MOBA_Q_BLOCK = 32

D_FF = 4 * D_MODEL
ROPE_THETA = 10000.0
LN_EPS = 1e-5
NORM_EPS = 1e-6
DEEPNORM_ALPHA = (2 * DEPTH) ** 0.25
DEEPNORM_BETA = (8 * DEPTH) ** -0.25

IN_SPLITS = (
    ('a_q', A_HEADS * HEAD_DIM), ('a_k', A_KV_HEADS * HEAD_DIM), ('a_v', A_KV_HEADS * HEAD_DIM),
    ('a_iq', IDX_HEADS * IDX_DIM), ('a_ik', IDX_DIM), ('a_iw', IDX_HEADS),
    ('b_q', GLA_HEADS * GLA_DK), ('b_k', GLA_HEADS * GLA_DK), ('b_v', GLA_HEADS * GLA_DV),
    ('b_glr', GLA_GATE_RANK), ('b_r', GROUP_W),
    ('c_q', GDN_HEADS * GDN_DK), ('c_k', GDN_HEADS * GDN_DK), ('c_v', GDN_HEADS * GDN_DV),
    ('c_a', GDN_HEADS), ('c_b', GDN_HEADS), ('c_g', GROUP_W),
    ('d_q', MOBA_HEADS * HEAD_DIM), ('d_k', MOBA_HEADS * HEAD_DIM), ('d_v', MOBA_HEADS * HEAD_DIM),
)
D_IN = sum(w for _, w in IN_SPLITS)
VALUE_COLS = ('a_v', 'b_v', 'c_v', 'd_v')

kernel_name = 'hybrid_dsa_gla_gdn_moba_deepnorm'

f32 = jnp.float32


def _value_col_scale():
    return np.concatenate([np.full((w,), DEEPNORM_BETA if n in VALUE_COLS else 1.0, np.float32)
                           for n, w in IN_SPLITS])


def _split_projection(z):
    offs = np.cumsum([w for _, w in IN_SPLITS])[:-1].tolist()
    return dict(zip([n for n, _ in IN_SPLITS], jnp.split(z, offs, axis=-1)))


def _layer_norm(x, g, b):
    xf = x.astype(f32)
    mu = jnp.mean(xf, axis=-1, keepdims=True)
    var = jnp.mean(jnp.square(xf - mu), axis=-1, keepdims=True)
    return ((xf - mu) * lax.rsqrt(var + LN_EPS)).astype(x.dtype) * g + b


def _rms_norm(x, g):
    xf = x.astype(f32)
    y = xf * lax.rsqrt(jnp.mean(jnp.square(xf), axis=-1, keepdims=True) + NORM_EPS)
    return y.astype(g.dtype) * g


def _l2_norm(x):
    xf = x.astype(f32)
    return xf * lax.rsqrt(jnp.sum(jnp.square(xf), axis=-1, keepdims=True) + NORM_EPS)


def _rope_tables(T, dtype):
    inv = ROPE_THETA ** (-jnp.arange(0, HEAD_DIM, 2, dtype=f32) / HEAD_DIM)
    ang = jnp.arange(T, dtype=f32)[:, None] * inv[None, :]
    return jnp.cos(ang).astype(dtype), jnp.sin(ang).astype(dtype)


def _rope(x, cos, sin):
    half = x.shape[-1] // 2
    x1, x2 = x[..., :half], x[..., half:]
    c, s = cos[None, :, None, :], sin[None, :, None, :]
    return jnp.concatenate([x1 * c - x2 * s, x2 * c + x1 * s], axis=-1)


def _causal_conv(x, w):
    K, T = w.shape[0], x.shape[1]
    xp = jnp.pad(x, ((0, 0), (K - 1, 0), (0, 0)))
    return sum(xp[:, i:i + T] * w[i] for i in range(K))


def _dsa_attention(q, k, v, iq, ik, iw, cos, sin):
    B, T, HQ, dh = q.shape
    HKV = k.shape[2]
    G = HQ // HKV
    topk = min(DSA_TOPK, T // 4)
    q, k, iq = _rope(q, cos, sin), _rope(k, cos, sin), _rope(iq, cos, sin)
    ik = _rope(ik[:, :, None, :], cos, sin)[:, :, 0, :]
    iw = iw * (IDX_HEADS ** -0.5 * IDX_DIM ** -0.5)
    scale = dh ** -0.5
    key_pos = jnp.arange(T)
    b_idx = jnp.arange(B)[:, None, None]

    def block(i):
        s0 = i * DSA_Q_BLOCK
        qb = lax.dynamic_slice_in_dim(q, s0, DSA_Q_BLOCK, axis=1)
        iqb = lax.dynamic_slice_in_dim(iq, s0, DSA_Q_BLOCK, axis=1)
        iwb = lax.dynamic_slice_in_dim(iw, s0, DSA_Q_BLOCK, axis=1)
        q_pos = s0 + jnp.arange(DSA_Q_BLOCK)
        rel = jax.nn.relu(jnp.einsum('bqhd,bsd->bhqs', iqb, ik))
        score = jnp.einsum('bqh,bhqs->bqs', iwb, rel).astype(f32)
        causal = key_pos[None, :] <= q_pos[:, None]
        score = jnp.where(causal[None], score, -jnp.inf)
        _, idx = lax.top_k(score, topk)
        valid = idx <= q_pos[None, :, None]
        k_sel = k[b_idx, idx]
        v_sel = v[b_idx, idx]
        qg = qb.reshape(B, DSA_Q_BLOCK, HKV, G, dh)
        s = jnp.einsum('bqhgd,bqnhd->bqhgn', qg, k_sel).astype(f32) * scale
        s = jnp.where(valid[:, :, None, None, :], s, -jnp.inf)
        p = jax.nn.softmax(s, axis=-1).astype(v.dtype)
        o = jnp.einsum('bqhgn,bqnhd->bqhgd', p, v_sel)
        return o.reshape(B, DSA_Q_BLOCK, HQ * dh)

    out = lax.map(block, jnp.arange(T // DSA_Q_BLOCK))
    return out.transpose(1, 0, 2, 3).reshape(B, T, HQ * dh)


def _gla(q, k, v, log_a):
    B, T, H, dk = q.shape
    dv = v.shape[-1]
    C = GLA_CHUNK
    NC = T // C

    def chunks(a):
        return a.astype(f32).reshape(B, NC, C, H, a.shape[-1]).transpose(1, 0, 3, 2, 4)

    incl = jnp.tril(jnp.ones((C, C), bool))

    def step(S, inp):
        qc, kc, vc, lc = inp
        b = jnp.cumsum(lc, axis=2)
        diff = jnp.where(incl[:, :, None], b[:, :, :, None, :] - b[:, :, None, :, :], -jnp.inf)
        att = jnp.einsum('bhid,bhijd,bhjd->bhij', qc, jnp.exp(diff), kc)
        o = jnp.einsum('bhij,bhjv->bhiv', att, vc) + jnp.einsum('bhik,bhkv->bhiv', qc * jnp.exp(b), S)
        b_last = b[:, :, -1:, :]
        S = jnp.exp(b_last)[:, :, 0, :, None] * S + jnp.einsum('bhjk,bhjv->bhkv', kc * jnp.exp(b_last - b), vc)
        return S, o

    S0 = jnp.zeros((B, H, dk, dv), f32)
    _, o = lax.scan(step, S0, (chunks(q), chunks(k), chunks(v), chunks(log_a)))
    return o.transpose(1, 0, 3, 2, 4).reshape(B, T, H, dv)


def _gated_delta_rule(q, k, v, beta, g):
    B, T, H, dk = q.shape
    dv = v.shape[-1]
    C = GDN_CHUNK
    NC = T // C

    def c4(a):
        return a.astype(f32).reshape(B, NC, C, H, a.shape[-1]).transpose(1, 0, 3, 2, 4)

    def c3(a):
        return a.astype(f32).reshape(B, NC, C, H).transpose(1, 0, 3, 2)

    incl = jnp.tril(jnp.ones((C, C), bool))
    strict = jnp.tril(jnp.ones((C, C), bool), -1)
    eye = jnp.eye(C, dtype=f32)

    def step(S, inp):
        qc, kc, vc, bc, gc = inp
        gam = jnp.cumsum(gc, axis=-1)
        decay = jnp.exp(jnp.where(incl, gam[..., :, None] - gam[..., None, :], -jnp.inf))
        kk = jnp.einsum('bhid,bhjd->bhij', kc, kc)
        a = jnp.where(strict, bc[..., :, None] * decay * kk, 0.0)
        rhs = jnp.concatenate([bc[..., None] * vc, (bc * jnp.exp(gam))[..., None] * kc], axis=-1)
        sol = lax.linalg.triangular_solve(eye + a, rhs, left_side=True, lower=True, unit_diagonal=True)
        u, w = sol[..., :dv], sol[..., dv:]
        delta = u - jnp.einsum('bhik,bhkv->bhiv', w, S)
        qk = jnp.einsum('bhid,bhjd->bhij', qc, kc) * decay
        o = (jnp.einsum('bhik,bhkv->bhiv', qc * jnp.exp(gam)[..., None], S)
             + jnp.einsum('bhij,bhjv->bhiv', qk, delta))
        g_last = gam[..., -1:]
        S = (jnp.exp(g_last)[..., None] * S
             + jnp.einsum('bhjk,bhjv->bhkv', kc * jnp.exp(g_last - gam)[..., None], delta))
        return S, o

    S0 = jnp.zeros((B, H, dk, dv), f32)
    _, o = lax.scan(step, S0, (c4(q), c4(k), c4(v), c3(beta), c3(g)))
    return o.transpose(1, 0, 3, 2, 4).reshape(B, T, H, dv)


def _moba_attention(q, k, v, cos, sin):
    B, T, H, dh = q.shape
    q, k = _rope(q, cos, sin), _rope(k, cos, sin)
    n_blocks = -(-T // MOBA_BLOCK)
    Tp = n_blocks * MOBA_BLOCK
    pad = Tp - T

    def to_bhtd(a):
        return jnp.pad(a, ((0, 0), (0, pad), (0, 0), (0, 0))).transpose(0, 2, 1, 3)

    q, k, v = to_bhtd(q), to_bhtd(k), to_bhtd(v)
    kb = k.reshape(B, H, n_blocks, MOBA_BLOCK, dh)
    vb = v.reshape(B, H, n_blocks, MOBA_BLOCK, dh)
    k_mean = jnp.mean(kb, axis=3)
    n_sel = min(MOBA_TOPK, n_blocks - 1)
    scale = dh ** -0.5
    blk_ids = jnp.arange(n_blocks)
    in_blk = jnp.arange(MOBA_BLOCK)
    b_idx = jnp.arange(B)[:, None, None, None]
    h_idx = jnp.arange(H)[None, :, None, None]

    def qblock(i):
        s0 = i * MOBA_Q_BLOCK
        qb = lax.dynamic_slice_in_dim(q, s0, MOBA_Q_BLOCK, axis=2)
        q_pos = s0 + jnp.arange(MOBA_Q_BLOCK)
        own = s0 // MOBA_BLOCK
        k_own = lax.dynamic_index_in_dim(kb, own, axis=2, keepdims=False)
        v_own = lax.dynamic_index_in_dim(vb, own, axis=2, keepdims=False)
        s_own = jnp.einsum('bhqd,bhkd->bhqk', qb, k_own).astype(f32) * scale
        own_pos = own * MOBA_BLOCK + in_blk
        s_own = jnp.where(own_pos[None, :] <= q_pos[:, None], s_own, -jnp.inf)
        if n_sel == 0:
            p = jax.nn.softmax(s_own, axis=-1).astype(v.dtype)
            return jnp.einsum('bhqk,bhkd->bhqd', p, v_own)
        gate = jnp.einsum('bhqd,bhnd->bhqn', qb, k_mean).astype(f32)
        gate = jnp.where(blk_ids < own, gate, -jnp.inf)
        _, sel = lax.top_k(gate, n_sel)
        valid = sel < own
        k_sel = kb[b_idx, h_idx, sel]
        v_sel = vb[b_idx, h_idx, sel]
        s_past = jnp.einsum('bhqd,bhqnkd->bhqnk', qb, k_sel).astype(f32) * scale
        s_past = jnp.where(valid[..., None], s_past, -jnp.inf)
        s_past = s_past.reshape(B, H, MOBA_Q_BLOCK, n_sel * MOBA_BLOCK)
        p = jax.nn.softmax(jnp.concatenate([s_past, s_own], axis=-1), axis=-1).astype(v.dtype)
        p_past = p[..., :n_sel * MOBA_BLOCK].reshape(B, H, MOBA_Q_BLOCK, n_sel, MOBA_BLOCK)
        p_own = p[..., n_sel * MOBA_BLOCK:]
        return (jnp.einsum('bhqnk,bhqnkd->bhqd', p_past, v_sel)
                + jnp.einsum('bhqk,bhkd->bhqd', p_own, v_own))

    out = lax.map(qblock, jnp.arange(Tp // MOBA_Q_BLOCK))
    out = out.transpose(1, 0, 3, 2, 4).reshape(B, Tp, H * dh)
    return out[:, :T]


def _token_mixers(h, w_in, gla_gate_w2, gla_gate_b, gla_norm_g, gdn_conv_w, gdn_a_log,
                  gdn_dt_bias, gdn_norm_g, cos, sin):
    B, T, _ = h.shape
    p = _split_projection(h @ w_in)

    def heads(a, n):
        return a.reshape(B, T, n, -1)

    y_a = _dsa_attention(heads(p['a_q'], A_HEADS), heads(p['a_k'], A_KV_HEADS), heads(p['a_v'], A_KV_HEADS),
                         heads(p['a_iq'], IDX_HEADS), p['a_ik'], p['a_iw'], cos, sin)

    gate_logit = (p['b_glr'] @ gla_gate_w2 + gla_gate_b).astype(f32)
    log_a = jax.nn.log_sigmoid(gate_logit) / GLA_TAU
    o_b = _gla(heads(p['b_q'], GLA_HEADS) * GLA_DK ** -0.5, heads(p['b_k'], GLA_HEADS),
               heads(p['b_v'], GLA_HEADS), heads(log_a, GLA_HEADS))
    y_b = (_rms_norm(o_b, gla_norm_g) * jax.nn.silu(heads(p['b_r'], GLA_HEADS))).reshape(B, T, GROUP_W)

    qkv = jax.nn.silu(_causal_conv(jnp.concatenate([p['c_q'], p['c_k'], p['c_v']], axis=-1), gdn_conv_w))
    cq, ck, cv = jnp.split(qkv, 3, axis=-1)
    cq = _l2_norm(heads(cq, GDN_HEADS)) * GDN_DK ** -0.5
    ck = _l2_norm(heads(ck, GDN_HEADS))
    beta = jax.nn.sigmoid(p['c_b'].astype(f32))
    g = -jnp.exp(gdn_a_log.astype(f32)) * jax.nn.softplus(p['c_a'].astype(f32) + gdn_dt_bias.astype(f32))
    o_c = _gated_delta_rule(cq, ck, heads(cv, GDN_HEADS), beta, g)
    y_c = (_rms_norm(o_c, gdn_norm_g) * jax.nn.silu(heads(p['c_g'], GDN_HEADS))).reshape(B, T, GROUP_W)

    y_d = _moba_attention(heads(p['d_q'], MOBA_HEADS), heads(p['d_k'], MOBA_HEADS),
                          heads(p['d_v'], MOBA_HEADS), cos, sin)

    return jnp.concatenate([y_a.astype(h.dtype), y_b.astype(h.dtype), y_c.astype(h.dtype), y_d.astype(h.dtype)], axis=-1)


def setup_inputs(seed: int = 0) -> dict:
    key = jax.random.key(seed)
    ks = jax.random.split(key, 19)
    L = DEPTH

    def nrm(k, shape, scale):
        return jax.random.normal(k, shape, jnp.float32) * scale

    x = nrm(ks[0], (BATCH, SEQ, D_MODEL), 1.0)
    ln_in_g = 1.0 + nrm(ks[1], (D_MODEL,), 0.02)
    ln_in_b = nrm(ks[2], (D_MODEL,), 0.02)
    w_in = nrm(ks[3], (L, D_MODEL, D_IN), D_MODEL ** -0.5) * jnp.asarray(_value_col_scale())
    w_out = nrm(ks[4], (L, D_MIX, D_MODEL), D_MIX ** -0.5) * DEEPNORM_BETA
    ln1_g = 1.0 + nrm(ks[5], (L, D_MODEL), 0.02)
    ln1_b = nrm(ks[6], (L, D_MODEL), 0.02)
    gla_gate_w2 = nrm(ks[7], (L, GLA_GATE_RANK, GLA_HEADS * GLA_DK), GLA_GATE_RANK ** -0.5)
    gla_gate_b = nrm(ks[8], (L, GLA_HEADS * GLA_DK), 0.02)
    gla_norm_g = 1.0 + nrm(ks[9], (L, GLA_DV), 0.02)
    gdn_conv_w = nrm(ks[10], (L, CONV_K, 3 * GROUP_W), CONV_K ** -0.5)
    gdn_a_log = jnp.log(jax.random.uniform(ks[11], (L, GDN_HEADS), jnp.float32, 1.0, 16.0))
    dt = jnp.exp(jax.random.uniform(ks[12], (L, GDN_HEADS), jnp.float32, math.log(1e-3), math.log(1e-1)))
    gdn_dt_bias = dt + jnp.log(-jnp.expm1(-dt))
    gdn_norm_g = 1.0 + nrm(ks[13], (L, GDN_DV), 0.02)
    w_up = nrm(ks[14], (L, D_MODEL, D_FF), D_MODEL ** -0.5) * DEEPNORM_BETA
    w_down = nrm(ks[15], (L, D_FF, D_MODEL), D_FF ** -0.5) * DEEPNORM_BETA
    ln2_g = 1.0 + nrm(ks[16], (L, D_MODEL), 0.02)
    ln2_b = nrm(ks[17], (L, D_MODEL), 0.02)
    return {'x': x, 'ln_in_g': ln_in_g, 'ln_in_b': ln_in_b, 'w_in': w_in, 'w_out': w_out,
            'ln1_g': ln1_g, 'ln1_b': ln1_b, 'gla_gate_w2': gla_gate_w2, 'gla_gate_b': gla_gate_b,
            'gla_norm_g': gla_norm_g, 'gdn_conv_w': gdn_conv_w, 'gdn_a_log': gdn_a_log,
            'gdn_dt_bias': gdn_dt_bias, 'gdn_norm_g': gdn_norm_g, 'w_up': w_up, 'w_down': w_down,
            'ln2_g': ln2_g, 'ln2_b': ln2_b}


def reference(x, ln_in_g, ln_in_b, w_in, w_out, ln1_g, ln1_b, gla_gate_w2, gla_gate_b, gla_norm_g,
              gdn_conv_w, gdn_a_log, gdn_dt_bias, gdn_norm_g, w_up, w_down, ln2_g, ln2_b):
    T = x.shape[1]
    cos, sin = _rope_tables(T, x.dtype)
    h = _layer_norm(x, ln_in_g, ln_in_b)
    for l in range(DEPTH):
        mix = _token_mixers(h, w_in[l], gla_gate_w2[l], gla_gate_b[l], gla_norm_g[l], gdn_conv_w[l],
                            gdn_a_log[l], gdn_dt_bias[l], gdn_norm_g[l], cos, sin)
        h = _layer_norm(DEEPNORM_ALPHA * h + mix @ w_out[l], ln1_g[l], ln1_b[l])
        ff = jnp.square(jax.nn.relu(h @ w_up[l])) @ w_down[l]
        h = _layer_norm(DEEPNORM_ALPHA * h + ff, ln2_g[l], ln2_b[l])
    return h
```

```python
import functools
import math

import numpy as np
import jax
import jax.numpy as jnp
from jax import lax
from jax.experimental import pallas as pl
from jax.experimental.pallas import tpu as pltpu

f32 = jnp.float32
bf16 = jnp.bfloat16
i32 = jnp.int32

HEAD_DIM = 128
A_HEADS = 8
A_KV_HEADS = 2
IDX_HEADS = 4
IDX_DIM = 128
DSA_TOPK = 256
GLA_HEADS = 4
GLA_DK = 128
GLA_DV = 256
GLA_GATE_RANK = 16
GLA_TAU = 16.0
GDN_HEADS = 8
CONV_K = 4
MOBA_HEADS = 8
MOBA_BLOCK = 256
MOBA_TOPK = 3
ROPE_THETA = 10000.0
LN_EPS = 1e-5
NORM_EPS = 1e-6

V7X_VMEM_BYTES = 64 * 2**20
VMEM_LIMIT = V7X_VMEM_BYTES * 3 // 4
LANES = 128

INT_MIN = -2**31
NEG_BIG = -1e30

P_LAYOUT = (
    ('a_q', 1024), ('b_v', 1024), ('b_r', 1024), ('c_q', 1024), ('c_k', 1024), ('c_v', 1024), ('c_g', 1024),
    ('d_q', 1024), ('d_k', 1024), ('d_v', 1024),
    ('a_iq', 512), ('b_q', 512), ('b_k', 512),
    ('a_k', 256), ('a_v', 256),
    ('a_ik', 128), ('a_iw', 128), ('b_glr', 128), ('c_ab', 128),
)
P_OFF = {}
_o = 0
for _n, _w in P_LAYOUT:
    P_OFF[_n] = (_o, _w)
    _o += _w
P_WIDTH = _o

REF_SPLITS = (
    ('a_q', 1024), ('a_k', 256), ('a_v', 256), ('a_iq', 512), ('a_ik', 128), ('a_iw', 4),
    ('b_q', 512), ('b_k', 512), ('b_v', 1024), ('b_glr', 16), ('b_r', 1024),
    ('c_q', 1024), ('c_k', 1024), ('c_v', 1024), ('c_a', 8), ('c_b', 8), ('c_g', 1024),
    ('d_q', 1024), ('d_k', 1024), ('d_v', 1024),
)
REF_OFF = {}
_o = 0
for _n, _w in REF_SPLITS:
    REF_OFF[_n] = (_o, _w)
    _o += _w


def _dot(a, b):
    return jnp.dot(a, b, preferred_element_type=f32)


def _dot_nt(a, b):
    return lax.dot_general(a, b, (((1,), (1,)), ((), ())), preferred_element_type=f32)


def _dot_tn(a, b):
    return lax.dot_general(a, b, (((0,), (0,)), ((), ())), preferred_element_type=f32)


def _split2(x):
    hi = x.astype(bf16)
    lo = (x - hi.astype(f32)).astype(bf16)
    return hi, lo


def _split3(x):
    hi = x.astype(bf16)
    r = x - hi.astype(f32)
    mid = r.astype(bf16)
    lo = (r - mid.astype(f32)).astype(bf16)
    return hi, mid, lo


def _mm3(x, y):
    xh, xl = _split2(x)
    yh, yl = _split2(y)
    return _dot(xh, yh) + _dot(xh, yl) + _dot(xl, yh)


def _sigmoid(x):
    return 1.0 / (1.0 + jnp.exp(-x))


def _softplus(x):
    return jnp.maximum(x, 0.0) + jnp.log1p(jnp.exp(-jnp.abs(x)))


def _params(sem):
    return pltpu.CompilerParams(dimension_semantics=sem, vmem_limit_bytes=VMEM_LIMIT)


def _mm_kernel(a_ref, b_ref, o_ref, *scratch, nk, act):
    def finish(r):
        if act == 'relu2':
            r = jnp.square(jnp.maximum(r, 0.0))
        o_ref[...] = r.astype(o_ref.dtype)

    if nk == 1:
        finish(_dot(a_ref[...], b_ref[...]))
        return
    acc_ref, = scratch
    k = pl.program_id(2)

    @pl.when(k == 0)
    def _():
        acc_ref[...] = jnp.zeros_like(acc_ref)

    acc_ref[...] += _dot(a_ref[...], b_ref[...])

    @pl.when(k == nk - 1)
    def _():
        finish(acc_ref[...])


def _matmul(a, b, *, tm, tn, tk, out_dtype, act=None, name='matmul'):
    M, K = a.shape
    N = b.shape[1]
    tm, tn, tk = min(tm, M), min(tn, N), min(tk, K)
    assert M % tm == 0 and N % tn == 0 and K % tk == 0
    nk = K // tk
    return pl.pallas_call(
        functools.partial(_mm_kernel, nk=nk, act=act),
        grid=(M // tm, N // tn, nk),
        in_specs=[pl.BlockSpec((tm, tk), lambda i, j, k: (i, k)),
                  pl.BlockSpec((tk, tn), lambda i, j, k: (k, j))],
        out_specs=pl.BlockSpec((tm, tn), lambda i, j, k: (i, j)),
        out_shape=jax.ShapeDtypeStruct((M, N), out_dtype),
        scratch_shapes=[pltpu.VMEM((tm, tn), f32)] if nk > 1 else [],
        compiler_params=_params(("parallel", "parallel", "arbitrary")),
        name=name,
    )(a, b)


def _ln_kernel(*refs, alpha, has_y):
    if has_y:
        x_ref, y_ref, g_ref, b_ref, o_ref, ob_ref = refs
        z = x_ref[...] * alpha + y_ref[...]
    else:
        x_ref, g_ref, b_ref, o_ref, ob_ref = refs
        z = x_ref[...]
    mu = jnp.mean(z, axis=-1, keepdims=True)
    zc = z - mu
    var = jnp.mean(zc * zc, axis=-1, keepdims=True)
    out = zc * lax.rsqrt(var + LN_EPS) * g_ref[...] + b_ref[...]
    o_ref[...] = out
    ob_ref[...] = out.astype(bf16)


def _layer_norm(x, y, g, b, *, alpha=1.0, tm=256):
    M, D = x.shape
    tm = min(tm, M)
    row = pl.BlockSpec((tm, D), lambda i: (i, 0))
    vec = pl.BlockSpec((1, D), lambda i: (0, 0))
    has_y = y is not None
    args = (x, y) if has_y else (x,)
    return pl.pallas_call(
        functools.partial(_ln_kernel, alpha=alpha, has_y=has_y),
        grid=(M // tm,),
        in_specs=[row] * len(args) + [vec, vec],
        out_specs=[row, row],
        out_shape=[jax.ShapeDtypeStruct((M, D), f32), jax.ShapeDtypeStruct((M, D), bf16)],
        compiler_params=_params(("parallel",)),
        name='layer_norm',
    )(*args, g.reshape(1, D), b.reshape(1, D))


def _flash_step(s, mask, m_prev, l_prev, acc_prev, vt):
    s = jnp.where(mask, s, NEG_BIG)
    m_new = jnp.maximum(m_prev, jnp.max(s, axis=0, keepdims=True))
    p = jnp.where(mask, jnp.exp(s - m_new), 0.0)
    a = jnp.exp(m_prev - m_new)
    l_new = a * l_prev + jnp.sum(p, axis=0, keepdims=True)
    acc_new = a * acc_prev + _dot(vt, p.astype(bf16))
    return m_new, l_new, acc_new


def _dsa_kernel(iqh_ref, iql_ref, iw_ref, q_ref, ikh_ref, ikl_ref, k_ref, vt_ref, o_ref,
                keys_ref, acc_ref, *, TQ, TK, topk, n_idx_heads, n_kv, group):
    i = pl.program_id(1)
    n_kt = ((i + 1) * TQ + TK - 1) // TK
    t_idx = i * TQ + lax.broadcasted_iota(i32, (1, TQ), 1)
    row = lax.broadcasted_iota(i32, (TK, 1), 0)
    iw = iw_ref[0]

    def score_tile(kt, carry):
        ikh = ikh_ref[0, kt]
        ikl = ikl_ref[0, kt]
        acc = jnp.zeros((TK, TQ), f32)
        for h in range(n_idx_heads):
            sl = slice(h * IDX_DIM, (h + 1) * IDX_DIM)
            qh = iqh_ref[0, :, sl]
            ql = iql_ref[0, :, sl]
            x = _dot_nt(ikh, qh) + _dot_nt(ikh, ql) + _dot_nt(ikl, qh)
            acc = acc + iw[h:h + 1, :] * jnp.maximum(x, 0.0)
        bits = lax.bitcast_convert_type(acc, i32)
        key = bits ^ ((bits >> 31) & jnp.int32(0x7FFFFFFF))
        key = jnp.where(key == -1, 0, key)
        key = jnp.where(kt * TK + row <= t_idx, key, jnp.int32(INT_MIN))
        keys_ref[pl.ds(pl.multiple_of(kt * TK, TK), TK), :] = key
        return carry

    lax.fori_loop(0, n_kt, score_tile, 0)

    def count(pred_fn):
        def body(kt, c):
            key = keys_ref[pl.ds(pl.multiple_of(kt * TK, TK), TK), :]
            return c + jnp.sum(pred_fn(key).astype(i32), axis=0, keepdims=True)
        return lax.fori_loop(0, n_kt, body, jnp.zeros((1, TQ), i32))

    def bit_pass(bi, cur):
        cand = cur + lax.shift_left(jnp.int32(1), 31 - bi)
        cnt = count(lambda key: key >= cand)
        return jnp.where(cnt >= topk, cand, cur)

    thr = lax.fori_loop(0, 32, bit_pass, jnp.full((1, TQ), INT_MIN, i32))
    need = (topk - count(lambda key: key > thr)).astype(f32)

    tri = (lax.broadcasted_iota(i32, (TK, TK), 0) > lax.broadcasted_iota(i32, (TK, TK), 1)).astype(bf16)
    q = q_ref[0]
    q_stack = [jnp.concatenate([q[:, (g * group + j) * HEAD_DIM:(g * group + j + 1) * HEAD_DIM]
                                for j in range(group)], axis=0) for g in range(n_kv)]
    NQ = group * TQ
    acc_ref[...] = jnp.zeros_like(acc_ref)

    def attend_tile(kt, carry):
        tie_carry, stats = carry
        key = keys_ref[pl.ds(pl.multiple_of(kt * TK, TK), TK), :]
        eq = (key == thr) & (key != INT_MIN)
        eq_f = jnp.where(eq, 1.0, 0.0)
        tie_rank = _dot(tri, eq_f.astype(bf16)) + tie_carry
        take = jnp.where(key > thr, 1.0, jnp.where(tie_rank < need, eq_f, 0.0))
        mask = jnp.concatenate([take] * group, axis=1) > 0.5
        ktile = k_ref[0, kt]
        new_stats = []
        for g in range(n_kv):
            m_prev, l_prev = stats[g]
            s = _dot_nt(ktile[:, g * HEAD_DIM:(g + 1) * HEAD_DIM], q_stack[g])
            m_new, l_new, acc_new = _flash_step(s, mask, m_prev, l_prev, acc_ref[g], vt_ref[0, g, kt])
            acc_ref[g] = acc_new
            new_stats.append((m_new, l_new))
        return tie_carry + jnp.sum(eq_f, axis=0, keepdims=True), tuple(new_stats)

    init = (jnp.zeros((1, TQ), f32),
            tuple((jnp.full((1, NQ), NEG_BIG, f32), jnp.zeros((1, NQ), f32)) for _ in range(n_kv)))
    _, stats = lax.fori_loop(0, n_kt, attend_tile, init)

    for g in range(n_kv):
        out_t = acc_ref[g] / stats[g][1]
        for j in range(group):
            h = g * group + j
            o_ref[0, :, h * HEAD_DIM:(h + 1) * HEAD_DIM] = out_t[:, j * TQ:(j + 1) * TQ].T


def _rope(x, cos, sin):
    half = x.shape[-1] // 2
    x1, x2 = x[..., :half], x[..., half:]
    c, s = cos[None, :, None, :], sin[None, :, None, :]
    return jnp.concatenate([x1 * c - x2 * s, x2 * c + x1 * s], axis=-1)


def _dsa(q, k, v, iq, ik, iw, cos, sin, *, TQ=128, TK=256):
    B, T, HQ, dh = q.shape
    HKV = k.shape[2]
    HI = iq.shape[2]
    TK = min(TK, T)
    NKT = T // TK
    topk = min(DSA_TOPK, T // 4)
    q = (_rope(q, cos, sin) * dh ** -0.5).astype(bf16).reshape(B, T, HQ * dh)
    k = _rope(k, cos, sin).astype(bf16).reshape(B, NKT, TK, HKV * dh)
    vt = v.astype(bf16).reshape(B, NKT, TK, HKV, dh).transpose(0, 3, 1, 4, 2)
    iqh, iql = _split2(_rope(iq, cos, sin).reshape(B, T, HI * IDX_DIM))
    ikh, ikl = _split2(_rope(ik[:, :, None, :], cos, sin).reshape(B, NKT, TK, IDX_DIM))
    iwt = (iw * (IDX_HEADS ** -0.5 * IDX_DIM ** -0.5)).transpose(0, 2, 1)

    kern = functools.partial(_dsa_kernel, TQ=TQ, TK=TK, topk=topk, n_idx_heads=HI, n_kv=HKV, group=HQ // HKV)
    return pl.pallas_call(
        kern,
        grid=(B, T // TQ),
        in_specs=[
            pl.BlockSpec((1, TQ, HI * IDX_DIM), lambda b, i: (b, i, 0)),
            pl.BlockSpec((1, TQ, HI * IDX_DIM), lambda b, i: (b, i, 0)),
            pl.BlockSpec((1, HI, TQ), lambda b, i: (b, 0, i)),
            pl.BlockSpec((1, TQ, HQ * dh), lambda b, i: (b, i, 0)),
            pl.BlockSpec((1, NKT, TK, IDX_DIM), lambda b, i: (b, 0, 0, 0)),
            pl.BlockSpec((1, NKT, TK, IDX_DIM), lambda b, i: (b, 0, 0, 0)),
            pl.BlockSpec((1, NKT, TK, HKV * dh), lambda b, i: (b, 0, 0, 0)),
            pl.BlockSpec((1, HKV, NKT, dh, TK), lambda b, i: (b, 0, 0, 0, 0)),
        ],
        out_specs=pl.BlockSpec((1, TQ, HQ * dh), lambda b, i: (b, i, 0)),
        out_shape=jax.ShapeDtypeStruct((B, T, HQ * dh), f32),
        scratch_shapes=[pltpu.VMEM((T, TQ), i32), pltpu.VMEM((HKV, dh, (HQ // HKV) * TQ), f32)],
        compiler_params=_params(("parallel", "arbitrary")),
        name='dsa_attention',
    )(iqh, iql, iwt, q, ikh, ikl, k, vt)


def _moba_kernel(q_ref, qh_ref, ql_ref, kmh_ref, kml_ref, k_ref, vt_ref, o_ref, sel_ref, *, NB, BS, n_sel):
    i = pl.program_id(2)
    kmh, kml = kmh_ref[0, 0], kml_ref[0, 0]
    qh, ql = qh_ref[0], ql_ref[0]
    gate = _dot_nt(kmh, qh) + _dot_nt(kmh, ql) + _dot_nt(kml, qh)
    n_idx = lax.broadcasted_iota(i32, (NB, 1), 0)
    past = n_idx < i
    for n in range(NB):
        gn = gate[n:n + 1, :]
        beats = jnp.where(gate > gn, 1.0, jnp.where((gate == gn) & (n_idx < n), 1.0, 0.0))
        rank = jnp.sum(jnp.where(past, beats, 0.0), axis=0, keepdims=True)
        sel_ref[n:n + 1, :] = jnp.where(rank < n_sel, 1.0, 0.0)

    q = q_ref[0]

    def past_block(n, carry):
        m_prev, l_prev, acc_prev = carry
        s = _dot_nt(k_ref[0, 0, n], q)
        mask = jnp.broadcast_to(sel_ref[pl.ds(n, 1), :] > 0.5, s.shape)
        return _flash_step(s, mask, m_prev, l_prev, acc_prev, vt_ref[0, 0, n])

    init = (jnp.full((1, BS), NEG_BIG, f32), jnp.zeros((1, BS), f32), jnp.zeros((HEAD_DIM, BS), f32))
    carry = lax.fori_loop(0, i, past_block, init)
    s = _dot_nt(k_ref[0, 0, i], q)
    causal = lax.broadcasted_iota(i32, (BS, BS), 0) <= lax.broadcasted_iota(i32, (BS, BS), 1)
    _, l_fin, acc = _flash_step(s, causal, *carry, vt_ref[0, 0, i])
    o_ref[0] = (acc / l_fin).T


def _moba(q, k, v, cos, sin):
    B, T, H, dh = q.shape
    BS = MOBA_BLOCK
    assert T % BS == 0
    NB = T // BS
    n_sel = min(MOBA_TOPK, NB - 1)
    q = _rope(q, cos, sin)
    k = _rope(k, cos, sin)
    qs = (q * dh ** -0.5).astype(bf16).reshape(B, T, H * dh)
    qh, ql = _split2(q.reshape(B, T, H * dh))
    kb = k.reshape(B, NB, BS, H, dh).transpose(0, 3, 1, 2, 4)
    kmh, kml = _split2(jnp.mean(kb, axis=3))
    vt = v.astype(bf16).reshape(B, NB, BS, H, dh).transpose(0, 3, 1, 4, 2)
    kern = functools.partial(_moba_kernel, NB=NB, BS=BS, n_sel=n_sel)
    qspec = pl.BlockSpec((1, BS, dh), lambda b, h, i: (b, i, h))
    return pl.pallas_call(
        kern,
        grid=(B, H, NB),
        in_specs=[qspec, qspec, qspec,
                  pl.BlockSpec((1, 1, NB, dh), lambda b, h, i: (b, h, 0, 0)),
                  pl.BlockSpec((1, 1, NB, dh), lambda b, h, i: (b, h, 0, 0)),
                  pl.BlockSpec((1, 1, NB, BS, dh), lambda b, h, i: (b, h, 0, 0, 0)),
                  pl.BlockSpec((1, 1, NB, dh, BS), lambda b, h, i: (b, h, 0, 0, 0))],
        out_specs=pl.BlockSpec((1, BS, dh), lambda b, h, i: (b, i, h)),
        out_shape=jax.ShapeDtypeStruct((B, T, H * dh), f32),
        scratch_shapes=[pltpu.VMEM((NB, BS), f32)],
        compiler_params=_params(("parallel", "parallel", "arbitrary")),
        name='moba_attention',
    )(qs, qh, ql, kmh, kml, kb.astype(bf16), vt)


def _gla_kernel(q_ref, k_ref, v_ref, glr_ref, r_ref, w2h_ref, w2l_ref, gb_ref, ng_ref, o_ref, st_ref, *, C, R):
    @pl.when(pl.program_id(1) == 0)
    def _():
        st_ref[...] = jnp.zeros_like(st_ref)

    gh, gl = _split2(glr_ref[0])
    logit = _dot(gh, w2h_ref[...]) + _dot(gh, w2l_ref[...]) + _dot(gl, w2h_ref[...]) + gb_ref[...]
    log_a = -_softplus(-logit) * (1.0 / GLA_TAU)
    tril = (lax.broadcasted_iota(i32, (C, C), 0) >= lax.broadcasted_iota(i32, (C, C), 1)).astype(bf16)
    a1, a2, a3 = _split3(log_a)
    b_all = _dot(tril, a1) + _dot(tril, a2) + _dot(tril, a3)
    for h in range(GLA_HEADS):
        ks = slice(h * GLA_DK, (h + 1) * GLA_DK)
        vs = slice(h * GLA_DV, (h + 1) * GLA_DV)
        b = b_all[:, ks]
        qh = q_ref[0, :, ks] * GLA_DK ** -0.5
        kh = k_ref[0, :, ks]
        vh = v_ref[0, :, vs]
        vb = vh.astype(bf16)
        st = st_ref[h]
        o = _dot_nt((qh * jnp.exp(b)).astype(bf16), st.astype(bf16))
        rows = []
        for blk in range(C // R):
            r0, r1 = blk * R, (blk + 1) * R
            b0 = b[r0 - 1:r0, :] if blk else jnp.zeros((1, GLA_DK), f32)
            qe = qh[r0:r1] * jnp.exp(b[r0:r1] - b0)
            ke = kh[:r1] * jnp.exp(b0 - b[:r1])
            att = _dot_nt(qe.astype(bf16), ke.astype(bf16))
            keep = lax.broadcasted_iota(i32, (R, r1), 1) <= lax.broadcasted_iota(i32, (R, r1), 0) + r0
            att = jnp.where(keep, att, 0.0)
            rows.append(_dot(att.astype(bf16), vb[:r1]))
        o = o + jnp.concatenate(rows, axis=0)
        b_last = b[C - 1:C, :]
        k_dec = kh * jnp.exp(b_last - b)
        st_ref[h] = st * jnp.exp(b_last) + _dot_tn(vb, k_dec.astype(bf16))
        y = o * lax.rsqrt(jnp.mean(o * o, axis=-1, keepdims=True) + NORM_EPS) * ng_ref[...]
        rh = r_ref[0, :, vs]
        o_ref[0, :, vs] = y * (rh * _sigmoid(rh))


def _gla(pq, pk, pv, pglr, pr, w2, gate_b, norm_g, *, C=128, R=32):
    B, T, _ = pq.shape
    C = min(C, T)
    W = GLA_HEADS * GLA_DV
    w2p = jnp.zeros((LANES, GLA_HEADS * GLA_DK), f32).at[:GLA_GATE_RANK].set(w2)
    w2h, w2l = _split2(w2p)
    tok = lambda w: pl.BlockSpec((1, C, w), lambda b, c: (b, c, 0))
    full = lambda a: pl.BlockSpec(a.shape, lambda b, c: (0,) * a.ndim)
    gb = gate_b.reshape(1, -1)
    ng = norm_g.reshape(1, -1)
    return pl.pallas_call(
        functools.partial(_gla_kernel, C=C, R=min(R, C)),
        grid=(B, T // C),
        in_specs=[tok(GLA_HEADS * GLA_DK), tok(GLA_HEADS * GLA_DK), tok(W), tok(LANES), tok(W),
                  full(w2h), full(w2l), full(gb), full(ng)],
        out_specs=tok(W),
        out_shape=jax.ShapeDtypeStruct((B, T, W), f32),
        scratch_shapes=[pltpu.VMEM((GLA_HEADS, GLA_DV, GLA_DK), f32)],
        compiler_params=_params(("parallel", "arbitrary")),
        name='gla',
    )(pq, pk, pv, pglr, pr, w2h, w2l, gb, ng)


def _unit_lower_inverse(a, row, col):
    C = a.shape[0]
    eye = jnp.where(row == col, 1.0, 0.0)
    n1 = jnp.where((row >> 3) == (col >> 3), a, 0.0)
    n2 = _mm3(n1, n1)
    n4 = _mm3(n2, n2)
    t = _mm3(_mm3(eye - n1, eye + n2), eye + n4)
    s = 8
    while s < C:
        sh = s.bit_length() - 1
        off = jnp.where(((row >> (sh + 1)) == (col >> (sh + 1))) & ((row >> sh) != (col >> sh)), a, 0.0)
        t = t - _mm3(_mm3(t, off), t)
        s *= 2
    return t


def _gdn_kernel(q_ref, k_ref, v_ref, ab_ref, gate_ref, alog_ref, dtb_ref, ng_ref, o_ref, s_ref, *, C):
    @pl.when(pl.program_id(1) == 0)
    def _():
        s_ref[...] = jnp.zeros_like(s_ref)

    H = GDN_HEADS
    row = lax.broadcasted_iota(i32, (C, C), 0)
    col = lax.broadcasted_iota(i32, (C, C), 1)
    ab = ab_ref[0]
    g_all = -jnp.exp(alog_ref[...]) * _softplus(ab + dtb_ref[...])
    beta_all = _sigmoid(ab)
    tril = (row >= col).astype(bf16)
    g1, g2, g3 = _split3(g_all)
    gam_all = _dot(tril, g1) + _dot(tril, g2) + _dot(tril, g3)
    gam_t = gam_all.T
    for h in range(H):
        sl = slice(h * HEAD_DIM, (h + 1) * HEAD_DIM)
        qh, kh, vh = q_ref[0, :, sl], k_ref[0, :, sl], v_ref[0, :, sl]
        gcol = gam_all[:, h:h + 1]
        grow = gam_t[h:h + 1, :]
        bcol = beta_all[:, H + h:H + h + 1]
        kb, qb = kh.astype(bf16), qh.astype(bf16)
        dec = jnp.where(row >= col, jnp.exp(jnp.minimum(gcol - grow, 0.0)), 0.0)
        a = jnp.where(row > col, bcol * dec * _dot_nt(kb, kb), 0.0)
        tinv = _unit_lower_inverse(a, row, col)
        egam = jnp.exp(gcol)
        rhs = jnp.concatenate([bcol * vh, (bcol * egam) * kh], axis=1)
        sol = _mm3(tinv, rhs)
        u, w = sol[:, :HEAD_DIM], sol[:, HEAD_DIM:]
        s = s_ref[h]
        sb = s.astype(bf16)
        delta = u - _dot(w.astype(bf16), sb)
        db = delta.astype(bf16)
        qk = _dot_nt(qb, kb) * dec
        o = _dot((qh * egam).astype(bf16), sb) + _dot(qk.astype(bf16), db)
        g_last = gcol[C - 1:C, :]
        k_dec = kh * jnp.exp(g_last - gcol)
        s_ref[h] = jnp.exp(g_last) * s + _dot_tn(k_dec.astype(bf16), db)
        y = o * lax.rsqrt(jnp.mean(o * o, axis=-1, keepdims=True) + NORM_EPS) * ng_ref[...]
        gt = gate_ref[0, :, sl]
        o_ref[0, :, sl] = y * (gt * _sigmoid(gt))


def _gdn(cq, ck, cv, pab, pgate, a_log, dt_bias, norm_g, *, C=128):
    B, T, W = cq.shape
    C = min(C, T)
    H = GDN_HEADS
    alog = jnp.zeros((1, LANES), f32).at[0, :H].set(a_log)
    dtb = jnp.zeros((1, LANES), f32).at[0, :H].set(dt_bias)
    ng = norm_g.reshape(1, -1)
    tok = lambda w: pl.BlockSpec((1, C, w), lambda b, c: (b, c, 0))
    full = lambda a: pl.BlockSpec(a.shape, lambda b, c: (0,) * a.ndim)
    return pl.pallas_call(
        functools.partial(_gdn_kernel, C=C),
        grid=(B, T // C),
        in_specs=[tok(W), tok(W), tok(W), tok(LANES), tok(W), full(alog), full(dtb), full(ng)],
        out_specs=tok(W),
        out_shape=jax.ShapeDtypeStruct((B, T, W), f32),
        scratch_shapes=[pltpu.VMEM((H, HEAD_DIM, HEAD_DIM), f32)],
        compiler_params=_params(("parallel", "arbitrary")),
        name='gated_delta_rule',
    )(cq, ck, cv, pab, pgate, alog, dtb, ng)


def _pack_w_in(w_in):
    D = w_in.shape[0]
    cols = []
    for name, width in P_LAYOUT:
        if name == 'c_ab':
            o, _ = REF_OFF['c_a']
            seg = w_in[:, o:o + 2 * GDN_HEADS]
        else:
            o, w = REF_OFF[name]
            seg = w_in[:, o:o + w]
        if seg.shape[1] < width:
            seg = jnp.pad(seg, ((0, 0), (0, width - seg.shape[1])))
        cols.append(seg)
    return jnp.concatenate(cols, axis=1).astype(bf16)


def _causal_conv_silu(x, w):
    K, T = w.shape[0], x.shape[1]
    xp = jnp.pad(x, ((0, 0), (K - 1, 0), (0, 0)))
    y = sum(xp[:, i:i + T] * w[i] for i in range(K))
    return y * jax.nn.sigmoid(y)


def _l2_norm(x):
    return x * lax.rsqrt(jnp.sum(jnp.square(x), axis=-1, keepdims=True) + NORM_EPS)


def _token_mixers(hb, B, T, w_in, gla_gate_w2, gla_gate_b, gla_norm_g, gdn_conv_w, gdn_a_log, gdn_dt_bias,
                  gdn_norm_g, cos, sin):
    p = _matmul(hb, _pack_w_in(w_in), tm=1024, tn=512, tk=hb.shape[1], out_dtype=f32, name='in_proj')
    p = p.reshape(B, T, P_WIDTH)

    def seg(name):
        o, w = P_OFF[name]
        return p[:, :, o:o + w]

    def heads(a, n):
        return a.reshape(B, T, n, -1)

    y_a = _dsa(heads(seg('a_q'), A_HEADS), heads(seg('a_k'), A_KV_HEADS), heads(seg('a_v'), A_KV_HEADS),
               heads(seg('a_iq'), IDX_HEADS), seg('a_ik'), seg('a_iw')[:, :, :IDX_HEADS], cos, sin)

    y_b = _gla(seg('b_q'), seg('b_k'), seg('b_v'), seg('b_glr'), seg('b_r'), gla_gate_w2, gla_gate_b, gla_norm_g)

    qkv = _causal_conv_silu(jnp.concatenate([seg('c_q'), seg('c_k'), seg('c_v')], axis=-1), gdn_conv_w)
    cq, ck, cv = jnp.split(qkv, 3, axis=-1)
    cq = (_l2_norm(heads(cq, GDN_HEADS)) * HEAD_DIM ** -0.5).reshape(B, T, -1)
    ck = _l2_norm(heads(ck, GDN_HEADS)).reshape(B, T, -1)
    y_c = _gdn(cq, ck, cv, seg('c_ab'), seg('c_g'), gdn_a_log, gdn_dt_bias, gdn_norm_g)

    y_d = _moba(heads(seg('d_q'), MOBA_HEADS), heads(seg('d_k'), MOBA_HEADS), heads(seg('d_v'), MOBA_HEADS),
                cos, sin)
    return jnp.concatenate([y_a, y_b, y_c, y_d], axis=-1).astype(bf16)


def _rope_tables(T):
    inv = ROPE_THETA ** (-jnp.arange(0, HEAD_DIM, 2, dtype=f32) / HEAD_DIM)
    ang = jnp.arange(T, dtype=f32)[:, None] * inv[None, :]
    return jnp.cos(ang), jnp.sin(ang)


def kernel(x, ln_in_g, ln_in_b, w_in, w_out, ln1_g, ln1_b, gla_gate_w2, gla_gate_b, gla_norm_g, gdn_conv_w,
           gdn_a_log, gdn_dt_bias, gdn_norm_g, w_up, w_down, ln2_g, ln2_b):
    B, T, D = x.shape
    depth = w_in.shape[0]
    alpha = (2 * depth) ** 0.25
    cos, sin = _rope_tables(T)
    h, hb = _layer_norm(x.reshape(B * T, D), None, ln_in_g, ln_in_b)
    for l in range(depth):
        mix = _token_mixers(hb, B, T, w_in[l], gla_gate_w2[l], gla_gate_b[l], gla_norm_g[l], gdn_conv_w[l],
                            gdn_a_log[l], gdn_dt_bias[l], gdn_norm_g[l], cos, sin)
        y = _matmul(mix.reshape(B * T, -1), w_out[l].astype(bf16), tm=1024, tn=512, tk=mix.shape[-1],
                    out_dtype=f32, name='out_proj')
        h, hb = _layer_norm(h, y, ln1_g[l], ln1_b[l], alpha=alpha)
        up = _matmul(hb, w_up[l].astype(bf16), tm=1024, tn=512, tk=D, out_dtype=bf16, act='relu2', name='mlp_up')
        ff = _matmul(up, w_down[l].astype(bf16), tm=1024, tn=1024, tk=1024, out_dtype=f32, name='mlp_down')
        h, hb = _layer_norm(h, ff, ln2_g[l], ln2_b[l], alpha=alpha)
    return h.reshape(B, T, D)
```

```python
import functools
import math

import jax
import jax.numpy as jnp
from jax import lax
from jax.experimental import pallas as pl
from jax.experimental.pallas import tpu as pltpu

f32 = jnp.float32
bf16 = jnp.bfloat16
i32 = jnp.int32

HEAD_DIM = 128
A_HEADS = 8
A_KV_HEADS = 2
IDX_HEADS = 4
IDX_DIM = 128
DSA_TOPK = 256
GLA_HEADS = 4
GLA_DK = 128
GLA_DV = 256
GLA_GATE_RANK = 16
GLA_TAU = 16.0
GDN_HEADS = 8
CONV_K = 4
MOBA_HEADS = 8
MOBA_BLOCK = 256
MOBA_TOPK = 3
ROPE_THETA = 10000.0
LN_EPS = 1e-5
NORM_EPS = 1e-6

V7X_VMEM_BYTES = 64 * 2**20
VMEM_LIMIT = V7X_VMEM_BYTES * 3 // 4
LANES = 128
SUBLANES = 8

INT_MIN = -2**31
NEG_BIG = -1e30
LOG2E = math.log2(math.e)
ATTN_Q_SCALE = HEAD_DIM ** -0.5 * LOG2E

TOK_TILE = MOBA_BLOCK

P_LAYOUT = (
    ('a_q', 1024), ('b_v', 1024), ('b_r', 1024), ('c_q', 1024), ('c_k', 1024), ('c_v', 1024), ('c_g', 1024),
    ('d_q', 1024), ('d_k', 1024), ('d_v', 1024),
    ('a_iq', 512), ('b_q', 512), ('b_k', 512),
    ('a_k', 256), ('a_v', 256),
    ('a_ik', 128), ('a_iw', 128), ('b_glr', 128), ('c_ab', 128),
)
P_OFF = {}
_o = 0
for _n, _w in P_LAYOUT:
    P_OFF[_n] = (_o, _w)
    _o += _w
P_WIDTH = _o

REF_SPLITS = (
    ('a_q', 1024), ('a_k', 256), ('a_v', 256), ('a_iq', 512), ('a_ik', 128), ('a_iw', 4),
    ('b_q', 512), ('b_k', 512), ('b_v', 1024), ('b_glr', 16), ('b_r', 1024),
    ('c_q', 1024), ('c_k', 1024), ('c_v', 1024), ('c_a', 8), ('c_b', 8), ('c_g', 1024),
    ('d_q', 1024), ('d_k', 1024), ('d_v', 1024),
)
REF_OFF = {}
_o = 0
for _n, _w in REF_SPLITS:
    REF_OFF[_n] = (_o, _w)
    _o += _w


def _dot(a, b):
    return jnp.dot(a, b, preferred_element_type=f32)


def _dot_nt(a, b):
    return lax.dot_general(a, b, (((1,), (1,)), ((), ())), preferred_element_type=f32)


def _dot_tn(a, b):
    return lax.dot_general(a, b, (((0,), (0,)), ((), ())), preferred_element_type=f32)


def _split2(x):
    hi = x.astype(bf16)
    lo = (x - hi.astype(f32)).astype(bf16)
    return hi, lo


def _split3(x):
    hi = x.astype(bf16)
    r = x - hi.astype(f32)
    mid = r.astype(bf16)
    lo = (r - mid.astype(f32)).astype(bf16)
    return hi, mid, lo


def _mm1(x, y):
    return _dot(x.astype(bf16), y.astype(bf16))


def _mm3(x, y):
    xh, xl = _split2(x)
    yh, yl = _split2(y)
    return _dot(xh, yh) + _dot(xh, yl) + _dot(xl, yh)


def _sigmoid(x):
    return 1.0 / (1.0 + jnp.exp(-x))


def _softplus(x):
    return jnp.maximum(x, 0.0) + jnp.log1p(jnp.exp(-jnp.abs(x)))


def _params(sem):
    return pltpu.CompilerParams(dimension_semantics=sem, vmem_limit_bytes=VMEM_LIMIT)


def _mm_kernel(*refs, n_a, nk, act):
    a_refs, b_ref, o_ref = refs[:n_a], refs[n_a], refs[n_a + 1]

    def product():
        if n_a == 1:
            return _dot(a_refs[0][...], b_ref[...])
        kw = b_ref.shape[0] // n_a
        out = _dot(a_refs[0][...], b_ref[0:kw, :])
        for g in range(1, n_a):
            out = out + _dot(a_refs[g][...], b_ref[g * kw:(g + 1) * kw, :])
        return out

    def finish(r):
        if act == 'relu2':
            r = jnp.square(jnp.maximum(r, 0.0))
        o_ref[...] = r.astype(o_ref.dtype)

    if nk == 1:
        finish(product())
        return
    acc_ref = refs[n_a + 2]
    k = pl.program_id(2)

    @pl.when(k == 0)
    def _():
        acc_ref[...] = jnp.zeros_like(acc_ref)

    acc_ref[...] += product()

    @pl.when(k == nk - 1)
    def _():
        finish(acc_ref[...])


def _matmul(a, b, *, tm, tn, tk, out_dtype, act=None, name='matmul'):
    a_list = a if isinstance(a, (tuple, list)) else (a,)
    n_a = len(a_list)
    M = a_list[0].shape[0]
    K, N = b.shape
    tm, tn, tk = min(tm, M), min(tn, N), min(tk, K)
    assert M % tm == 0 and N % tn == 0 and K % tk == 0
    nk = K // tk
    assert n_a == 1 or nk == 1
    ka = tk // n_a
    return pl.pallas_call(
        functools.partial(_mm_kernel, n_a=n_a, nk=nk, act=act),
        grid=(M // tm, N // tn, nk),
        in_specs=[pl.BlockSpec((tm, ka), lambda i, j, k: (i, k))] * n_a
                 + [pl.BlockSpec((tk, tn), lambda i, j, k: (k, j))],
        out_specs=pl.BlockSpec((tm, tn), lambda i, j, k: (i, j)),
        out_shape=jax.ShapeDtypeStruct((M, N), out_dtype),
        scratch_shapes=[pltpu.VMEM((tm, tn), f32)] if nk > 1 else [],
        compiler_params=_params(("parallel", "parallel", "arbitrary")),
        name=name,
    )(*a_list, b)


def _ln_kernel(*refs, alpha, has_y):
    if has_y:
        x_ref, y_ref, g_ref, b_ref, o_ref, ob_ref = refs
        z = x_ref[...] * alpha + y_ref[...]
    else:
        x_ref, g_ref, b_ref, o_ref, ob_ref = refs
        z = x_ref[...]
    mu = jnp.mean(z, axis=-1, keepdims=True)
    zc = z - mu
    var = jnp.mean(zc * zc, axis=-1, keepdims=True)
    out = zc * lax.rsqrt(var + LN_EPS) * g_ref[...] + b_ref[...]
    o_ref[...] = out
    ob_ref[...] = out.astype(bf16)


def _layer_norm(x, y, g, b, *, alpha=1.0, tm=256):
    M, D = x.shape
    tm = min(tm, M)
    row = pl.BlockSpec((tm, D), lambda i: (i, 0))
    vec = pl.BlockSpec((1, D), lambda i: (0, 0))
    has_y = y is not None
    args = (x, y) if has_y else (x,)
    return pl.pallas_call(
        functools.partial(_ln_kernel, alpha=alpha, has_y=has_y),
        grid=(M // tm,),
        in_specs=[row] * len(args) + [vec, vec],
        out_specs=[row, row],
        out_shape=[jax.ShapeDtypeStruct((M, D), f32), jax.ShapeDtypeStruct((M, D), bf16)],
        compiler_params=_params(("parallel",)),
        name='layer_norm',
    )(*args, g.reshape(1, D), b.reshape(1, D))


def _prep_kernel(aq_ref, ak_ref, av_ref, aiq_ref, aik_ref, aiw_ref,
                 cq_ref, ck_ref, cv_ref, cqh_ref, ckh_ref, cvh_ref,
                 dq_ref, dk_ref, dv_ref, cos_ref, sin_ref, cw_ref,
                 oaq, oak, oavt, oaiq, oaik, oaiwt, ocq, ock, ocv,
                 odq, odqh, odql, odk, odvt, odkmh, odkml, *, TT):
    cosf, sinf = cos_ref[...], sin_ref[...]

    def rope(x):
        return x * cosf + pltpu.roll(x, HEAD_DIM // 2, 1) * sinf

    def head(h):
        return slice(h * HEAD_DIM, (h + 1) * HEAD_DIM)

    for h in range(A_HEADS):
        oaq[0, :, head(h)] = (rope(aq_ref[0, :, head(h)]) * ATTN_Q_SCALE).astype(bf16)
    for g in range(A_KV_HEADS):
        oak[0, :, head(g)] = rope(ak_ref[0, :, head(g)]).astype(bf16)
        oavt[0, 0, head(g), :] = av_ref[0, :, head(g)].T.astype(bf16)
    for h in range(IDX_HEADS):
        oaiq[0, :, head(h)] = rope(aiq_ref[0, :, head(h)]).astype(bf16)
    oaik[0] = rope(aik_ref[0]).astype(bf16)
    oaiwt[0] = aiw_ref[0].T[:SUBLANES, :] * (IDX_HEADS ** -0.5 * IDX_DIM ** -0.5)

    first = pl.program_id(1) == 0
    for j, (x_ref, h_ref, o_ref) in enumerate(((cq_ref, cqh_ref, ocq), (ck_ref, ckh_ref, ock),
                                                (cv_ref, cvh_ref, ocv))):
        width = x_ref.shape[2]
        halo = jnp.where(first, 0.0, h_ref[0])
        cat = jnp.concatenate([halo, x_ref[0]], axis=0)
        w = cw_ref[:, j * width:(j + 1) * width]
        y = cat[SUBLANES - CONV_K + 1:SUBLANES - CONV_K + 1 + TT] * w[0:1]
        for i in range(1, CONV_K):
            s0 = SUBLANES - CONV_K + 1 + i
            y = y + cat[s0:s0 + TT] * w[i:i + 1]
        y = y * _sigmoid(y)
        if j == 2:
            o_ref[0] = y
        else:
            post = HEAD_DIM ** -0.5 if j == 0 else 1.0
            for h in range(GDN_HEADS):
                yh = y[:, head(h)]
                o_ref[0, :, head(h)] = yh * (lax.rsqrt(jnp.sum(yh * yh, axis=-1, keepdims=True) + NORM_EPS) * post)

    for h in range(MOBA_HEADS):
        qr = rope(dq_ref[0, :, head(h)])
        odq[0, :, head(h)] = (qr * ATTN_Q_SCALE).astype(bf16)
        qh, ql = _split2(qr)
        odqh[0, :, head(h)] = qh
        odql[0, :, head(h)] = ql
        kr = rope(dk_ref[0, :, head(h)])
        odk[0, :, head(h)] = kr.astype(bf16)
        kmh, kml = _split2(jnp.mean(kr, axis=0, keepdims=True))
        odkmh[0, 0, :, head(h)] = kmh
        odkml[0, 0, :, head(h)] = kml
        odvt[0, 0, head(h), :] = dv_ref[0, :, head(h)].T.astype(bf16)


def _prep(p, cosf, sinf, conv_w):
    B, T, _ = p.shape
    TT = min(TOK_TILE, T)
    NT = T // TT

    def seg(name, rows=TT):
        off, w = P_OFF[name]
        return pl.BlockSpec((1, rows, w), lambda b, i: (b, i, off // w))

    def halo(name):
        off, w = P_OFF[name]
        return pl.BlockSpec((1, SUBLANES, w),
                            lambda b, i: (b, jnp.maximum(i * (TT // SUBLANES) - 1, 0), off // w))

    in_names = ('a_q', 'a_k', 'a_v', 'a_iq', 'a_ik', 'a_iw', 'c_q', 'c_k', 'c_v')
    in_specs = ([seg(n) for n in in_names] + [halo(n) for n in ('c_q', 'c_k', 'c_v')]
                + [seg(n) for n in ('d_q', 'd_k', 'd_v')]
                + [pl.BlockSpec((TT, HEAD_DIM), lambda b, i: (i, 0))] * 2
                + [pl.BlockSpec(conv_w.shape, lambda b, i: (0, 0))])

    def tok(w, dt):
        return pl.BlockSpec((1, TT, w), lambda b, i: (b, i, 0)), jax.ShapeDtypeStruct((B, T, w), dt)

    def tposed(rows, dt):
        return (pl.BlockSpec((1, 1, rows, TT), lambda b, i: (b, i, 0, 0)),
                jax.ShapeDtypeStruct((B, NT, rows, TT), dt))

    def per_tile(w, dt):
        return (pl.BlockSpec((1, 1, 1, w), lambda b, i: (b, i, 0, 0)),
                jax.ShapeDtypeStruct((B, NT, 1, w), dt))

    outs = [
        tok(1024, bf16), tok(256, bf16), tposed(256, bf16), tok(512, bf16), tok(128, bf16),
        (pl.BlockSpec((1, SUBLANES, TT), lambda b, i: (b, 0, i)), jax.ShapeDtypeStruct((B, SUBLANES, T), f32)),
        tok(1024, f32), tok(1024, f32), tok(1024, f32),
        tok(1024, bf16), tok(1024, bf16), tok(1024, bf16), tok(1024, bf16), tposed(1024, bf16),
        per_tile(1024, bf16), per_tile(1024, bf16),
    ]
    res = pl.pallas_call(
        functools.partial(_prep_kernel, TT=TT),
        grid=(B, NT),
        in_specs=in_specs,
        out_specs=[o[0] for o in outs],
        out_shape=[o[1] for o in outs],
        compiler_params=_params(("parallel", "arbitrary")),
        name='mixer_prep',
    )(*([p] * 15), cosf, sinf, conv_w)
    names = ('aq', 'ak', 'avt', 'aiq', 'aik', 'aiwt', 'cq', 'ck', 'cv', 'dq', 'dqh', 'dql', 'dk', 'dvt',
             'dkmh', 'dkml')
    return dict(zip(names, res))


def _flash_step(s, mask, m_prev, l_prev, acc_prev, vt):
    s = jnp.where(mask, s, NEG_BIG)
    m_new = jnp.maximum(m_prev, jnp.max(s, axis=0, keepdims=True))
    p = jnp.exp2(s - m_new)
    a = jnp.exp2(m_prev - m_new)
    l_new = a * l_prev + jnp.sum(p, axis=0, keepdims=True)
    acc_new = a * acc_prev + _dot(vt, p.astype(bf16))
    return m_new, l_new, acc_new


def _dsa_kernel(iq_ref, iw_ref, q_ref, ik_ref, k_ref, vt_ref, o_ref, keys_ref, acc_ref, *, TT, topk):
    i = pl.program_id(1)
    n_kt = i + 1
    t_idx = i * TT + lax.broadcasted_iota(i32, (1, TT), 1)
    row = lax.broadcasted_iota(i32, (TT, 1), 0)
    iw = iw_ref[0]
    iq = iq_ref[0]

    def tile(kt):
        return pl.ds(pl.multiple_of(kt * TT, TT), TT)

    def score_tile(kt, carry):
        ik = ik_ref[0, tile(kt), :]
        acc = jnp.zeros((TT, TT), f32)
        for h in range(IDX_HEADS):
            x = _dot_nt(ik, iq[:, h * IDX_DIM:(h + 1) * IDX_DIM])
            acc = acc + iw[h:h + 1, :] * jnp.maximum(x, 0.0)
        bits = lax.bitcast_convert_type(acc, i32)
        key = bits ^ ((bits >> 31) & jnp.int32(0x7FFFFFFF))
        key = jnp.where(key == -1, 0, key)
        key = jnp.where(kt * TT + row <= t_idx, key, jnp.int32(INT_MIN))
        keys_ref[tile(kt), :] = key
        return carry

    lax.fori_loop(0, n_kt, score_tile, 0)

    def count(pred_fn):
        def body(kt, c):
            ind = pred_fn(keys_ref[tile(kt), :]).astype(i32)
            return c + jnp.sum(ind.reshape(TT // SUBLANES, SUBLANES, TT), axis=0)
        c = lax.fori_loop(0, n_kt, body, jnp.zeros((SUBLANES, TT), i32))
        return jnp.sum(c, axis=0, keepdims=True)

    takes_all = t_idx + 1 <= topk

    def bit_pass(state):
        bi, cur, cnt_cur, _ = state
        cand = cur + lax.shift_left(jnp.int32(1), 31 - bi)
        cnt = count(lambda key: key >= cand)
        ok = cnt >= topk
        cur = jnp.where(ok, cand, cur)
        cnt_cur = jnp.where(ok, cnt, cnt_cur)
        pending = jnp.max(jnp.where((cnt_cur == topk) | takes_all, 0, 1))
        return bi + 1, cur, cnt_cur, pending

    state = (jnp.int32(0), jnp.full((1, TT), INT_MIN, i32), jnp.full((1, TT), n_kt * TT, i32), jnp.int32(1))
    _, thr, _, _ = lax.while_loop(lambda st: (st[0] < 32) & (st[3] > 0), bit_pass, state)
    need = (topk - count(lambda key: key > thr)).astype(f32)

    tri = (lax.broadcasted_iota(i32, (TT, TT), 0) > lax.broadcasted_iota(i32, (TT, TT), 1)).astype(bf16)
    group = A_HEADS // A_KV_HEADS
    acc_ref[...] = jnp.zeros_like(acc_ref)

    def attend_tile(kt, carry):
        tie_carry, stats = carry
        key = keys_ref[tile(kt), :]
        eq = (key == thr) & (key != INT_MIN)
        eq_f = jnp.where(eq, 1.0, 0.0)
        tie_rank = _dot(tri, eq_f.astype(bf16)) + tie_carry
        mask = (key > thr) | (eq & (tie_rank < need))
        ktile = k_ref[0, tile(kt), :]
        logits = [_dot_nt(ktile[:, (h // group) * HEAD_DIM:(h // group + 1) * HEAD_DIM],
                          q_ref[0, :, h * HEAD_DIM:(h + 1) * HEAD_DIM]) for h in range(A_HEADS)]
        new_stats = []
        for h in range(A_HEADS):
            g = h // group
            m_new, l_new, acc_new = _flash_step(logits[h], mask, *stats[h], acc_ref[h],
                                                vt_ref[0, kt, g * HEAD_DIM:(g + 1) * HEAD_DIM, :])
            acc_ref[h] = acc_new
            new_stats.append((m_new, l_new))
        return tie_carry + jnp.sum(eq_f, axis=0, keepdims=True), tuple(new_stats)

    init = (jnp.zeros((1, TT), f32),
            tuple((jnp.full((1, TT), NEG_BIG, f32), jnp.zeros((1, TT), f32)) for _ in range(A_HEADS)))
    _, stats = lax.fori_loop(0, n_kt, attend_tile, init)
    for h in range(A_HEADS):
        o_ref[0, :, h * HEAD_DIM:(h + 1) * HEAD_DIM] = (acc_ref[h] / stats[h][1]).T.astype(o_ref.dtype)


def _dsa(ops):
    B, T, W = ops['aq'].shape
    TT = min(TOK_TILE, T)
    NT = T // TT
    topk = min(DSA_TOPK, T // 4)
    return pl.pallas_call(
        functools.partial(_dsa_kernel, TT=TT, topk=topk),
        grid=(B, NT),
        in_specs=[
            pl.BlockSpec((1, TT, IDX_HEADS * IDX_DIM), lambda b, i: (b, i, 0)),
            pl.BlockSpec((1, SUBLANES, TT), lambda b, i: (b, 0, i)),
            pl.BlockSpec((1, TT, W), lambda b, i: (b, i, 0)),
            pl.BlockSpec((1, T, IDX_DIM), lambda b, i: (b, 0, 0)),
            pl.BlockSpec((1, T, A_KV_HEADS * HEAD_DIM), lambda b, i: (b, 0, 0)),
            pl.BlockSpec((1, NT, A_KV_HEADS * HEAD_DIM, TT), lambda b, i: (b, 0, 0, 0)),
        ],
        out_specs=pl.BlockSpec((1, TT, W), lambda b, i: (b, i, 0)),
        out_shape=jax.ShapeDtypeStruct((B, T, W), bf16),
        scratch_shapes=[pltpu.VMEM((T, TT), i32), pltpu.VMEM((A_HEADS, HEAD_DIM, TT), f32)],
        compiler_params=_params(("parallel", "arbitrary")),
        name='dsa_attention',
    )(ops['aiq'], ops['aiwt'], ops['aq'], ops['aik'], ops['ak'], ops['avt'])


def _moba_kernel(q_ref, qh_ref, ql_ref, kmh_ref, kml_ref, k_ref, vt_ref, o_ref, sel_ref, acc_ref,
                 *, NB, BS, n_sel, HP):
    i = pl.program_id(2)
    n_idx = lax.broadcasted_iota(i32, (NB, 1), 0)
    past = n_idx < i

    def head(h):
        return slice(h * HEAD_DIM, (h + 1) * HEAD_DIM)

    for h in range(HP):
        kmh, kml = kmh_ref[0, :, head(h)], kml_ref[0, :, head(h)]
        qh, ql = qh_ref[0, :, head(h)], ql_ref[0, :, head(h)]
        gate = _dot_nt(kmh, qh) + _dot_nt(kmh, ql) + _dot_nt(kml, qh)
        g = jnp.where(past, gate, -jnp.inf)
        sel = jnp.zeros((NB, BS), f32)
        for _ in range(n_sel):
            m = jnp.max(g, axis=0, keepdims=True)
            first = jnp.min(jnp.where(g == m, n_idx, NB), axis=0, keepdims=True)
            pick = (n_idx == first) & (m > -jnp.inf)
            sel = jnp.where(pick, 1.0, sel)
            g = jnp.where(pick, -jnp.inf, g)
        sel_ref[h] = sel

    acc_ref[...] = jnp.zeros_like(acc_ref)

    def block(n, stats, mask_fn):
        ktile = k_ref[0, pl.ds(pl.multiple_of(n * BS, BS), BS), :]
        logits = [_dot_nt(ktile[:, head(h)], q_ref[0, :, head(h)]) for h in range(HP)]
        new_stats = []
        for h in range(HP):
            s = logits[h]
            m_new, l_new, acc_new = _flash_step(s, mask_fn(h, s), *stats[h], acc_ref[h], vt_ref[0, n, head(h), :])
            acc_ref[h] = acc_new
            new_stats.append((m_new, l_new))
        return tuple(new_stats)

    def past_block(n, stats):
        return block(n, stats, lambda h, s: jnp.broadcast_to(sel_ref[h, pl.ds(n, 1), :] > 0.5, s.shape))

    init = tuple((jnp.full((1, BS), NEG_BIG, f32), jnp.zeros((1, BS), f32)) for _ in range(HP))
    stats = lax.fori_loop(0, i, past_block, init)
    causal = lax.broadcasted_iota(i32, (BS, BS), 0) <= lax.broadcasted_iota(i32, (BS, BS), 1)
    stats = block(i, stats, lambda h, s: causal)
    for h in range(HP):
        o_ref[0, :, head(h)] = (acc_ref[h] / stats[h][1]).T.astype(o_ref.dtype)


def _moba(ops, *, HP=4):
    B, T, W = ops['dq'].shape
    BS = MOBA_BLOCK
    assert T % BS == 0 and TOK_TILE == BS
    NB = T // BS
    n_sel = min(MOBA_TOPK, NB - 1)
    WP = HP * HEAD_DIM
    kmh = ops['dkmh'].reshape(B, NB, W)
    kml = ops['dkml'].reshape(B, NB, W)
    qspec = pl.BlockSpec((1, BS, WP), lambda b, hg, i: (b, i, hg))
    kmspec = pl.BlockSpec((1, NB, WP), lambda b, hg, i: (b, 0, hg))
    return pl.pallas_call(
        functools.partial(_moba_kernel, NB=NB, BS=BS, n_sel=n_sel, HP=HP),
        grid=(B, W // WP, NB),
        in_specs=[qspec, qspec, qspec, kmspec, kmspec,
                  pl.BlockSpec((1, T, WP), lambda b, hg, i: (b, 0, hg)),
                  pl.BlockSpec((1, NB, WP, BS), lambda b, hg, i: (b, 0, hg, 0))],
        out_specs=qspec,
        out_shape=jax.ShapeDtypeStruct((B, T, W), bf16),
        scratch_shapes=[pltpu.VMEM((HP, NB, BS), f32), pltpu.VMEM((HP, HEAD_DIM, BS), f32)],
        compiler_params=_params(("parallel", "parallel", "arbitrary")),
        name='moba_attention',
    )(ops['dq'], ops['dqh'], ops['dql'], kmh, kml, ops['dk'], ops['dvt'])


def _gla_kernel(q_ref, k_ref, v_ref, glr_ref, r_ref, w2h_ref, w2l_ref, gb_ref, ng_ref, o_ref, st_ref, *, C, R):
    @pl.when(pl.program_id(1) == 0)
    def _():
        st_ref[...] = jnp.zeros_like(st_ref)

    gh, gl = _split2(glr_ref[0])
    logit = _dot(gh, w2h_ref[...]) + _dot(gh, w2l_ref[...]) + _dot(gl, w2h_ref[...]) + gb_ref[...]
    log_a = -_softplus(-logit) * (1.0 / GLA_TAU)
    tril = (lax.broadcasted_iota(i32, (C, C), 0) >= lax.broadcasted_iota(i32, (C, C), 1)).astype(bf16)
    a1, a2, a3 = _split3(log_a)
    b_all = _dot(tril, a1) + _dot(tril, a2) + _dot(tril, a3)
    for h in range(GLA_HEADS):
        ks = slice(h * GLA_DK, (h + 1) * GLA_DK)
        vs = slice(h * GLA_DV, (h + 1) * GLA_DV)
        b = b_all[:, ks]
        qh = q_ref[0, :, ks] * GLA_DK ** -0.5
        kh = k_ref[0, :, ks]
        vb = v_ref[0, :, vs].astype(bf16)
        st = st_ref[h]
        o = _dot_nt((qh * jnp.exp(b)).astype(bf16), st.astype(bf16))
        rows = []
        for blk in range(C // R):
            r0, r1 = blk * R, (blk + 1) * R
            b0 = b[r0 - 1:r0, :] if blk else jnp.zeros((1, GLA_DK), f32)
            qe = qh[r0:r1] * jnp.exp(b[r0:r1] - b0)
            ke = kh[:r1] * jnp.exp(b0 - b[:r1])
            att = _dot_nt(qe.astype(bf16), ke.astype(bf16))
            keep = lax.broadcasted_iota(i32, (R, r1), 1) <= lax.broadcasted_iota(i32, (R, r1), 0) + r0
            att = jnp.where(keep, att, 0.0)
            rows.append(_dot(att.astype(bf16), vb[:r1]))
        o = o + jnp.concatenate(rows, axis=0)
        b_last = b[C - 1:C, :]
        k_dec = kh * jnp.exp(b_last - b)
        st_ref[h] = st * jnp.exp(b_last) + _dot_tn(vb, k_dec.astype(bf16))
        y = o * lax.rsqrt(jnp.mean(o * o, axis=-1, keepdims=True) + NORM_EPS) * ng_ref[...]
        rh = r_ref[0, :, vs]
        o_ref[0, :, vs] = (y * (rh * _sigmoid(rh))).astype(o_ref.dtype)


def _gla(p, w2, gate_b, norm_g, *, C=128, R=32):
    B, T, _ = p.shape
    C = min(C, T)
    W = GLA_HEADS * GLA_DV
    w2p = jnp.zeros((LANES, GLA_HEADS * GLA_DK), f32).at[:GLA_GATE_RANK].set(w2)
    w2h, w2l = _split2(w2p)

    def seg(name):
        off, w = P_OFF[name]
        return pl.BlockSpec((1, C, w), lambda b, c: (b, c, off // w))

    full = lambda a: pl.BlockSpec(a.shape, lambda b, c: (0,) * a.ndim)
    gb = gate_b.reshape(1, -1)
    ng = norm_g.reshape(1, -1)
    return pl.pallas_call(
        functools.partial(_gla_kernel, C=C, R=min(R, C)),
        grid=(B, T // C),
        in_specs=[seg('b_q'), seg('b_k'), seg('b_v'), seg('b_glr'), seg('b_r'),
                  full(w2h), full(w2l), full(gb), full(ng)],
        out_specs=pl.BlockSpec((1, C, W), lambda b, c: (b, c, 0)),
        out_shape=jax.ShapeDtypeStruct((B, T, W), bf16),
        scratch_shapes=[pltpu.VMEM((GLA_HEADS, GLA_DV, GLA_DK), f32)],
        compiler_params=_params(("parallel", "arbitrary")),
        name='gla',
    )(p, p, p, p, p, w2h, w2l, gb, ng)


def _unit_lower_inverses(mats, row, col):
    C = mats[0].shape[0]
    eye = jnp.where(row == col, 1.0, 0.0)
    blk = (row >> 3) == (col >> 3)
    n1 = [jnp.where(blk, a, 0.0) for a in mats]
    n2 = [_mm3(x, x) for x in n1]
    n4 = [_mm3(x, x) for x in n2]
    t = [_mm3(eye - x1, eye + x2) for x1, x2 in zip(n1, n2)]
    t = [_mm3(x, eye + x4) for x, x4 in zip(t, n4)]
    s = SUBLANES
    while s < C:
        sh = s.bit_length() - 1
        sel = ((row >> (sh + 1)) == (col >> (sh + 1))) & ((row >> sh) != (col >> sh))
        left = [_mm1(x, jnp.where(sel, a, 0.0)) for x, a in zip(t, mats)]
        t = [x - _mm1(y, x) for x, y in zip(t, left)]
        s *= 2
    return t


def _gdn_kernel(q_ref, k_ref, v_ref, ab_ref, gate_ref, alog_ref, dtb_ref, ng_ref, o_ref, s_ref, *, C):
    @pl.when(pl.program_id(1) == 0)
    def _():
        s_ref[...] = jnp.zeros_like(s_ref)

    H = GDN_HEADS
    row = lax.broadcasted_iota(i32, (C, C), 0)
    col = lax.broadcasted_iota(i32, (C, C), 1)
    ab = ab_ref[0]
    g_all = -jnp.exp(alog_ref[...]) * _softplus(ab + dtb_ref[...])
    beta_all = _sigmoid(ab)
    tril = (row >= col).astype(bf16)
    g1, g2, g3 = _split3(g_all)
    gam_all = _dot(tril, g1) + _dot(tril, g2) + _dot(tril, g3)
    gam_t = gam_all.T
    heads = range(H)
    sls = [slice(h * HEAD_DIM, (h + 1) * HEAD_DIM) for h in heads]
    q = [q_ref[0, :, sl] for sl in sls]
    k = [k_ref[0, :, sl] for sl in sls]
    kb = [x.astype(bf16) for x in k]
    gcol = [gam_all[:, h:h + 1] for h in heads]
    bcol = [beta_all[:, H + h:H + h + 1] for h in heads]
    egam = [jnp.exp(x) for x in gcol]
    dec = [jnp.where(row >= col, jnp.exp(jnp.minimum(gcol[h] - gam_t[h:h + 1, :], 0.0)), 0.0) for h in heads]
    kk = [_dot_nt(x, x) for x in kb]
    qk = [_dot_nt(q[h].astype(bf16), kb[h]) for h in heads]
    a = [jnp.where(row > col, bcol[h] * dec[h] * kk[h], 0.0) for h in heads]
    tinv = _unit_lower_inverses(a, row, col)
    sol = [_mm1(tinv[h], jnp.concatenate([bcol[h] * v_ref[0, :, sls[h]], (bcol[h] * egam[h]) * k[h]], axis=1))
           for h in heads]
    sb = [s_ref[h].astype(bf16) for h in heads]
    ws = [_dot(sol[h][:, HEAD_DIM:].astype(bf16), sb[h]) for h in heads]
    qs = [_dot((q[h] * egam[h]).astype(bf16), sb[h]) for h in heads]
    db = [(sol[h][:, :HEAD_DIM] - ws[h]).astype(bf16) for h in heads]
    o = [qs[h] + _dot((qk[h] * dec[h]).astype(bf16), db[h]) for h in heads]
    for h in heads:
        g_last = gcol[h][C - 1:C, :]
        k_dec = k[h] * jnp.exp(g_last - gcol[h])
        s_ref[h] = jnp.exp(g_last) * s_ref[h] + _dot_tn(k_dec.astype(bf16), db[h])
    for h in heads:
        y = o[h] * lax.rsqrt(jnp.mean(o[h] * o[h], axis=-1, keepdims=True) + NORM_EPS) * ng_ref[...]
        gt = gate_ref[0, :, sls[h]]
        o_ref[0, :, sls[h]] = (y * (gt * _sigmoid(gt))).astype(o_ref.dtype)


def _gdn(cq, ck, cv, p, a_log, dt_bias, norm_g, *, C=128):
    B, T, W = cq.shape
    C = min(C, T)
    H = GDN_HEADS
    alog = jnp.zeros((1, LANES), f32).at[0, :H].set(a_log)
    dtb = jnp.zeros((1, LANES), f32).at[0, :H].set(dt_bias)
    ng = norm_g.reshape(1, -1)
    tok = pl.BlockSpec((1, C, W), lambda b, c: (b, c, 0))

    def seg(name):
        off, w = P_OFF[name]
        return pl.BlockSpec((1, C, w), lambda b, c: (b, c, off // w))

    full = lambda a: pl.BlockSpec(a.shape, lambda b, c: (0,) * a.ndim)
    return pl.pallas_call(
        functools.partial(_gdn_kernel, C=C),
        grid=(B, T // C),
        in_specs=[tok, tok, tok, seg('c_ab'), seg('c_g'), full(alog), full(dtb), full(ng)],
        out_specs=tok,
        out_shape=jax.ShapeDtypeStruct((B, T, W), bf16),
        scratch_shapes=[pltpu.VMEM((H, HEAD_DIM, HEAD_DIM), f32)],
        compiler_params=_params(("parallel", "arbitrary")),
        name='gated_delta_rule',
    )(cq, ck, cv, p, p, alog, dtb, ng)


def _pack_w_in(w_in):
    cols = []
    for name, width in P_LAYOUT:
        if name == 'c_ab':
            o, _ = REF_OFF['c_a']
            seg = w_in[:, o:o + 2 * GDN_HEADS]
        else:
            o, w = REF_OFF[name]
            seg = w_in[:, o:o + w]
        if seg.shape[1] < width:
            seg = jnp.pad(seg, ((0, 0), (0, width - seg.shape[1])))
        cols.append(seg)
    return jnp.concatenate(cols, axis=1).astype(bf16)


def _token_mixers(hb, B, T, w_in, gla_gate_w2, gla_gate_b, gla_norm_g, gdn_conv_w, gdn_a_log, gdn_dt_bias,
                  gdn_norm_g, cosf, sinf):
    p = _matmul(hb, _pack_w_in(w_in), tm=1024, tn=512, tk=hb.shape[1], out_dtype=f32, name='in_proj')
    p = p.reshape(B, T, P_WIDTH)
    ops = _prep(p, cosf, sinf, gdn_conv_w)
    y_a = _dsa(ops)
    y_b = _gla(p, gla_gate_w2, gla_gate_b, gla_norm_g)
    y_c = _gdn(ops['cq'], ops['ck'], ops['cv'], p, gdn_a_log, gdn_dt_bias, gdn_norm_g)
    y_d = _moba(ops)
    return tuple(y.reshape(B * T, -1) for y in (y_a, y_b, y_c, y_d))


def _rope_tables(T):
    inv = ROPE_THETA ** (-jnp.arange(0, HEAD_DIM, 2, dtype=f32) / HEAD_DIM)
    ang = jnp.arange(T, dtype=f32)[:, None] * inv[None, :]
    cos, sin = jnp.cos(ang), jnp.sin(ang)
    return jnp.concatenate([cos, cos], axis=1), jnp.concatenate([-sin, sin], axis=1)


def kernel(x, ln_in_g, ln_in_b, w_in, w_out, ln1_g, ln1_b, gla_gate_w2, gla_gate_b, gla_norm_g, gdn_conv_w,
           gdn_a_log, gdn_dt_bias, gdn_norm_g, w_up, w_down, ln2_g, ln2_b):
    B, T, D = x.shape
    depth = w_in.shape[0]
    alpha = (2 * depth) ** 0.25
    cosf, sinf = _rope_tables(T)
    h, hb = _layer_norm(x.reshape(B * T, D), None, ln_in_g, ln_in_b)
    for l in range(depth):
        mix = _token_mixers(hb, B, T, w_in[l], gla_gate_w2[l], gla_gate_b[l], gla_norm_g[l], gdn_conv_w[l],
                            gdn_a_log[l], gdn_dt_bias[l], gdn_norm_g[l], cosf, sinf)
        y = _matmul(mix, w_out[l].astype(bf16), tm=1024, tn=512, tk=w_out.shape[1], out_dtype=f32, name='out_proj')
        h, hb = _layer_norm(h, y, ln1_g[l], ln1_b[l], alpha=alpha)
        up = _matmul(hb, w_up[l].astype(bf16), tm=1024, tn=512, tk=D, out_dtype=bf16, act='relu2', name='mlp_up')
        ff = _matmul(up, w_down[l].astype(bf16), tm=1024, tn=1024, tk=1024, out_dtype=f32, name='mlp_down')
        h, hb = _layer_norm(h, ff, ln2_g[l], ln2_b[l], alpha=alpha)
    return h.reshape(B, T, D)
```

```python
import functools
import math

import jax
import jax.numpy as jnp
from jax import lax
from jax.experimental import pallas as pl
from jax.experimental.pallas import tpu as pltpu

f32 = jnp.float32
bf16 = jnp.bfloat16
i32 = jnp.int32

HEAD_DIM = 128
A_HEADS = 8
A_KV_HEADS = 2
IDX_HEADS = 4
IDX_DIM = 128
DSA_TOPK = 256
GLA_HEADS = 4
GLA_DK = 128
GLA_DV = 256
GLA_GATE_RANK = 16
GLA_TAU = 16.0
GDN_HEADS = 8
CONV_K = 4
MOBA_HEADS = 8
MOBA_BLOCK = 256
MOBA_TOPK = 3
ROPE_THETA = 10000.0
LN_EPS = 1e-5
NORM_EPS = 1e-6

V7X_VMEM_BYTES = 64 * 2**20
VMEM_LIMIT = V7X_VMEM_BYTES * 3 // 4
LANES = 128
SUBLANES = 8

INT_MIN = -2**31
NEG_BIG = -1e30
LOG2E = math.log2(math.e)
ATTN_Q_SCALE = HEAD_DIM ** -0.5 * LOG2E

TOK_TILE = MOBA_BLOCK

P_LAYOUT = (
    ('a_q', 1024), ('b_v', 1024), ('b_r', 1024), ('c_q', 1024), ('c_k', 1024), ('c_v', 1024), ('c_g', 1024),
    ('d_q', 1024), ('d_k', 1024), ('d_v', 1024),
    ('a_iq', 512), ('b_q', 512), ('b_k', 512),
    ('a_k', 256), ('a_v', 256),
    ('a_ik', 128), ('a_iw', 128), ('b_glr', 128), ('c_ab', 128),
)
P_OFF = {}
_o = 0
for _n, _w in P_LAYOUT:
    P_OFF[_n] = (_o, _w)
    _o += _w
P_WIDTH = _o

REF_SPLITS = (
    ('a_q', 1024), ('a_k', 256), ('a_v', 256), ('a_iq', 512), ('a_ik', 128), ('a_iw', 4),
    ('b_q', 512), ('b_k', 512), ('b_v', 1024), ('b_glr', 16), ('b_r', 1024),
    ('c_q', 1024), ('c_k', 1024), ('c_v', 1024), ('c_a', 8), ('c_b', 8), ('c_g', 1024),
    ('d_q', 1024), ('d_k', 1024), ('d_v', 1024),
)
REF_OFF = {}
_o = 0
for _n, _w in REF_SPLITS:
    REF_OFF[_n] = (_o, _w)
    _o += _w


def _dot(a, b):
    return jnp.dot(a, b, preferred_element_type=f32)


def _dot_nt(a, b):
    return lax.dot_general(a, b, (((1,), (1,)), ((), ())), preferred_element_type=f32)


def _dot_tn(a, b):
    return lax.dot_general(a, b, (((0,), (0,)), ((), ())), preferred_element_type=f32)


def _split2(x):
    hi = x.astype(bf16)
    lo = (x - hi.astype(f32)).astype(bf16)
    return hi, lo


def _split3(x):
    hi = x.astype(bf16)
    r = x - hi.astype(f32)
    mid = r.astype(bf16)
    lo = (r - mid.astype(f32)).astype(bf16)
    return hi, mid, lo


def _mm1(x, y):
    return _dot(x.astype(bf16), y.astype(bf16))


def _mm3(x, y):
    xh, xl = _split2(x)
    yh, yl = _split2(y)
    return _dot(xh, yh) + _dot(xh, yl) + _dot(xl, yh)


def _sigmoid(x):
    return 1.0 / (1.0 + jnp.exp(-x))


def _softplus(x):
    return jnp.maximum(x, 0.0) + jnp.log1p(jnp.exp(-jnp.abs(x)))


def _params(sem):
    return pltpu.CompilerParams(dimension_semantics=sem, vmem_limit_bytes=VMEM_LIMIT)


def _mm_kernel(*refs, n_a, nk, act):
    a_refs, b_ref, o_ref = refs[:n_a], refs[n_a], refs[n_a + 1]

    def product():
        if n_a == 1:
            return _dot(a_refs[0][...], b_ref[...])
        kw = b_ref.shape[0] // n_a
        out = _dot(a_refs[0][...], b_ref[0:kw, :])
        for g in range(1, n_a):
            out = out + _dot(a_refs[g][...], b_ref[g * kw:(g + 1) * kw, :])
        return out

    def finish(r):
        if act == 'relu2':
            r = jnp.square(jnp.maximum(r, 0.0))
        o_ref[...] = r.astype(o_ref.dtype)

    if nk == 1:
        finish(product())
        return
    acc_ref = refs[n_a + 2]
    k = pl.program_id(2)

    @pl.when(k == 0)
    def _():
        acc_ref[...] = jnp.zeros_like(acc_ref)

    acc_ref[...] += product()

    @pl.when(k == nk - 1)
    def _():
        finish(acc_ref[...])


def _matmul(a, b, layer, *, tm, tn, tk, out_dtype, act=None, name='matmul'):
    a_list = a if isinstance(a, (tuple, list)) else (a,)
    n_a = len(a_list)
    M = a_list[0].shape[0]
    _, K, N = b.shape
    tm, tn, tk = min(tm, M), min(tn, N), min(tk, K)
    assert M % tm == 0 and N % tn == 0 and K % tk == 0
    nk = K // tk
    assert n_a == 1 or nk == 1
    ka = tk // n_a
    return pl.pallas_call(
        functools.partial(_mm_kernel, n_a=n_a, nk=nk, act=act),
        grid=(M // tm, N // tn, nk),
        in_specs=[pl.BlockSpec((tm, ka), lambda i, j, k: (i, k))] * n_a
                 + [pl.BlockSpec((None, tk, tn), lambda i, j, k: (layer, k, j))],
        out_specs=pl.BlockSpec((tm, tn), lambda i, j, k: (i, j)),
        out_shape=jax.ShapeDtypeStruct((M, N), out_dtype),
        scratch_shapes=[pltpu.VMEM((tm, tn), f32)] if nk > 1 else [],
        compiler_params=_params(("parallel", "parallel", "arbitrary")),
        name=name,
    )(*a_list, b)


def _ln_kernel(*refs, alpha, has_y):
    if has_y:
        x_ref, y_ref, g_ref, b_ref, o_ref, ob_ref = refs
        z = x_ref[...] * alpha + y_ref[...]
    else:
        x_ref, g_ref, b_ref, o_ref, ob_ref = refs
        z = x_ref[...]
    mu = jnp.mean(z, axis=-1, keepdims=True)
    zc = z - mu
    var = jnp.mean(zc * zc, axis=-1, keepdims=True)
    out = zc * lax.rsqrt(var + LN_EPS) * g_ref[...] + b_ref[...]
    o_ref[...] = out
    ob_ref[...] = out.astype(bf16)


def _layer_norm(x, y, g, b, *, alpha=1.0, tm=256):
    M, D = x.shape
    tm = min(tm, M)
    row = pl.BlockSpec((tm, D), lambda i: (i, 0))
    vec = pl.BlockSpec((1, D), lambda i: (0, 0))
    has_y = y is not None
    args = (x, y) if has_y else (x,)
    return pl.pallas_call(
        functools.partial(_ln_kernel, alpha=alpha, has_y=has_y),
        grid=(M // tm,),
        in_specs=[row] * len(args) + [vec, vec],
        out_specs=[row, row],
        out_shape=[jax.ShapeDtypeStruct((M, D), f32), jax.ShapeDtypeStruct((M, D), bf16)],
        compiler_params=_params(("parallel",)),
        name='layer_norm',
    )(*args, g.reshape(1, D), b.reshape(1, D))


def _prep_kernel(aq_ref, ak_ref, av_ref, aiq_ref, aik_ref, aiw_ref,
                 cq_ref, ck_ref, cv_ref, cqh_ref, ckh_ref, cvh_ref,
                 dq_ref, dk_ref, dv_ref, cos_ref, sin_ref, cw_ref,
                 oaq, oak, oavt, oaiq, oaik, oaiwt, ocq, ock, ocv,
                 odq, odqh, odql, odk, odvt, odkmh, odkml, *, TT):
    cosf, sinf = cos_ref[...], sin_ref[...]

    def rope(x):
        return x * cosf + pltpu.roll(x, HEAD_DIM // 2, 1) * sinf

    def head(h):
        return slice(h * HEAD_DIM, (h + 1) * HEAD_DIM)

    for h in range(A_HEADS):
        oaq[0, :, head(h)] = (rope(aq_ref[0, :, head(h)]) * ATTN_Q_SCALE).astype(bf16)
    for g in range(A_KV_HEADS):
        oak[0, :, head(g)] = rope(ak_ref[0, :, head(g)]).astype(bf16)
        oavt[0, 0, head(g), :] = av_ref[0, :, head(g)].T.astype(bf16)
    for h in range(IDX_HEADS):
        oaiq[0, :, head(h)] = rope(aiq_ref[0, :, head(h)]).astype(bf16)
    oaik[0] = rope(aik_ref[0]).astype(bf16)
    oaiwt[0] = aiw_ref[0].T[:SUBLANES, :] * (IDX_HEADS ** -0.5 * IDX_DIM ** -0.5)

    first = pl.program_id(1) == 0
    for j, (x_ref, h_ref, o_ref) in enumerate(((cq_ref, cqh_ref, ocq), (ck_ref, ckh_ref, ock),
                                                (cv_ref, cvh_ref, ocv))):
        width = x_ref.shape[2]
        halo = jnp.where(first, 0.0, h_ref[0])
        cat = jnp.concatenate([halo, x_ref[0]], axis=0)
        w = cw_ref[:, j * width:(j + 1) * width]
        y = cat[SUBLANES - CONV_K + 1:SUBLANES - CONV_K + 1 + TT] * w[0:1]
        for i in range(1, CONV_K):
            s0 = SUBLANES - CONV_K + 1 + i
            y = y + cat[s0:s0 + TT] * w[i:i + 1]
        y = y * _sigmoid(y)
        if j == 2:
            o_ref[0] = y
        else:
            post = HEAD_DIM ** -0.5 if j == 0 else 1.0
            for h in range(GDN_HEADS):
                yh = y[:, head(h)]
                o_ref[0, :, head(h)] = yh * (lax.rsqrt(jnp.sum(yh * yh, axis=-1, keepdims=True) + NORM_EPS) * post)

    for h in range(MOBA_HEADS):
        qr = rope(dq_ref[0, :, head(h)])
        odq[0, :, head(h)] = (qr * ATTN_Q_SCALE).astype(bf16)
        qh, ql = _split2(qr)
        odqh[0, :, head(h)] = qh
        odql[0, :, head(h)] = ql
        kr = rope(dk_ref[0, :, head(h)])
        odk[0, :, head(h)] = kr.astype(bf16)
        kmh, kml = _split2(jnp.mean(kr, axis=0, keepdims=True))
        odkmh[0, 0, :, head(h)] = kmh
        odkml[0, 0, :, head(h)] = kml
        odvt[0, 0, head(h), :] = dv_ref[0, :, head(h)].T.astype(bf16)


def _prep(p, cosf, sinf, conv_w):
    B, T, _ = p.shape
    TT = min(TOK_TILE, T)
    NT = T // TT

    def seg(name, rows=TT):
        off, w = P_OFF[name]
        return pl.BlockSpec((1, rows, w), lambda b, i: (b, i, off // w))

    def halo(name):
        off, w = P_OFF[name]
        return pl.BlockSpec((1, SUBLANES, w),
                            lambda b, i: (b, jnp.maximum(i * (TT // SUBLANES) - 1, 0), off // w))

    in_names = ('a_q', 'a_k', 'a_v', 'a_iq', 'a_ik', 'a_iw', 'c_q', 'c_k', 'c_v')
    in_specs = ([seg(n) for n in in_names] + [halo(n) for n in ('c_q', 'c_k', 'c_v')]
                + [seg(n) for n in ('d_q', 'd_k', 'd_v')]
                + [pl.BlockSpec((TT, HEAD_DIM), lambda b, i: (i, 0))] * 2
                + [pl.BlockSpec(conv_w.shape, lambda b, i: (0, 0))])

    def tok(w, dt):
        return pl.BlockSpec((1, TT, w), lambda b, i: (b, i, 0)), jax.ShapeDtypeStruct((B, T, w), dt)

    def tposed(rows, dt):
        return (pl.BlockSpec((1, 1, rows, TT), lambda b, i: (b, i, 0, 0)),
                jax.ShapeDtypeStruct((B, NT, rows, TT), dt))

    def per_tile(w, dt):
        return (pl.BlockSpec((1, 1, 1, w), lambda b, i: (b, i, 0, 0)),
                jax.ShapeDtypeStruct((B, NT, 1, w), dt))

    outs = [
        tok(1024, bf16), tok(256, bf16), tposed(256, bf16), tok(512, bf16), tok(128, bf16),
        (pl.BlockSpec((1, SUBLANES, TT), lambda b, i: (b, 0, i)), jax.ShapeDtypeStruct((B, SUBLANES, T), f32)),
        tok(1024, f32), tok(1024, f32), tok(1024, f32),
        tok(1024, bf16), tok(1024, bf16), tok(1024, bf16), tok(1024, bf16), tposed(1024, bf16),
        per_tile(1024, bf16), per_tile(1024, bf16),
    ]
    res = pl.pallas_call(
        functools.partial(_prep_kernel, TT=TT),
        grid=(B, NT),
        in_specs=in_specs,
        out_specs=[o[0] for o in outs],
        out_shape=[o[1] for o in outs],
        compiler_params=_params(("parallel", "arbitrary")),
        name='mixer_prep',
    )(*([p] * 15), cosf, sinf, conv_w)
    names = ('aq', 'ak', 'avt', 'aiq', 'aik', 'aiwt', 'cq', 'ck', 'cv', 'dq', 'dqh', 'dql', 'dk', 'dvt',
             'dkmh', 'dkml')
    return dict(zip(names, res))


def _flash_update(p, m_prev, m_new, l_prev, acc_prev, vt, ones):
    pb = p.astype(bf16)
    a = jnp.exp2(m_prev - m_new)
    l_new = a * l_prev + _dot(ones, pb)[0:1]
    acc_new = a * acc_prev + _dot(vt, pb)
    return m_new, l_new, acc_new


def _flash_step(s, mask, m_prev, l_prev, acc_prev, vt, ones):
    s = jnp.where(mask, s, NEG_BIG)
    m_new = jnp.maximum(m_prev, jnp.max(s, axis=0, keepdims=True))
    return _flash_update(jnp.exp2(s - m_new), m_prev, m_new, l_prev, acc_prev, vt, ones)


def _flash_step_cols(s, cols, m_prev, l_prev, acc_prev, vt, ones):
    m_new = jnp.where(cols, jnp.maximum(m_prev, jnp.max(s, axis=0, keepdims=True)), m_prev)
    p = jnp.exp2(s - jnp.where(cols, m_new, -NEG_BIG))
    return _flash_update(p, m_prev, m_new, l_prev, acc_prev, vt, ones)


def _dsa_kernel(iq_ref, iw_ref, q_ref, ik_ref, k_ref, vt_ref, o_ref, keys_ref, acc_ref, *, TT, topk):
    i = pl.program_id(1)
    n_kt = i + 1
    t_idx = i * TT + lax.broadcasted_iota(i32, (1, TT), 1)
    row = lax.broadcasted_iota(i32, (TT, 1), 0)
    iw = iw_ref[0]
    iq = iq_ref[0]

    def tile(kt):
        return pl.ds(pl.multiple_of(kt * TT, TT), TT)

    def score_tile(kt, carry):
        ik = ik_ref[0, tile(kt), :]
        acc = jnp.zeros((TT, TT), f32)
        for h in range(IDX_HEADS):
            x = _dot_nt(ik, iq[:, h * IDX_DIM:(h + 1) * IDX_DIM])
            acc = acc + iw[h:h + 1, :] * jnp.maximum(x, 0.0)
        bits = lax.bitcast_convert_type(acc, i32)
        key = bits ^ ((bits >> 31) & jnp.int32(0x7FFFFFFF))
        key = jnp.where(key == -1, 0, key)
        key = jnp.where(kt * TT + row <= t_idx, key, jnp.int32(INT_MIN))
        keys_ref[tile(kt), :] = key
        return carry

    lax.fori_loop(0, n_kt, score_tile, 0)

    def count(pred_fn):
        def body(kt, c):
            ind = pred_fn(keys_ref[tile(kt), :]).astype(i32)
            return c + jnp.sum(ind.reshape(TT // SUBLANES, SUBLANES, TT), axis=0)
        c = lax.fori_loop(0, n_kt, body, jnp.zeros((SUBLANES, TT), i32))
        return jnp.sum(c, axis=0, keepdims=True)

    takes_all = t_idx + 1 <= topk

    def bit_pass(state):
        bi, cur, cnt_cur, _ = state
        cand = cur + lax.shift_left(jnp.int32(1), 31 - bi)
        cnt = count(lambda key: key >= cand)
        ok = cnt >= topk
        cur = jnp.where(ok, cand, cur)
        cnt_cur = jnp.where(ok, cnt, cnt_cur)
        pending = jnp.max(jnp.where((cnt_cur == topk) | takes_all, 0, 1))
        return bi + 1, cur, cnt_cur, pending

    state = (jnp.int32(0), jnp.full((1, TT), INT_MIN, i32), jnp.full((1, TT), n_kt * TT, i32), jnp.int32(1))
    _, thr, _, _ = lax.while_loop(lambda st: (st[0] < 32) & (st[3] > 0), bit_pass, state)
    need = (topk - count(lambda key: key > thr)).astype(f32)

    tri = (lax.broadcasted_iota(i32, (TT, TT), 0) > lax.broadcasted_iota(i32, (TT, TT), 1)).astype(bf16)
    ones = jnp.ones((2 * SUBLANES, TT), bf16)
    group = A_HEADS // A_KV_HEADS
    acc_ref[...] = jnp.zeros_like(acc_ref)

    def attend_tile(kt, carry):
        tie_carry, stats = carry
        key = keys_ref[tile(kt), :]
        eq = (key == thr) & (key != INT_MIN)
        eq_f = jnp.where(eq, 1.0, 0.0)
        tie_rank = _dot(tri, eq_f.astype(bf16)) + tie_carry
        mask = (key > thr) | (eq & (tie_rank < need))
        ktile = k_ref[0, tile(kt), :]
        logits = [_dot_nt(ktile[:, (h // group) * HEAD_DIM:(h // group + 1) * HEAD_DIM],
                          q_ref[0, :, h * HEAD_DIM:(h + 1) * HEAD_DIM]) for h in range(A_HEADS)]
        new_stats = []
        for h in range(A_HEADS):
            g = h // group
            m_new, l_new, acc_new = _flash_step(logits[h], mask, *stats[h], acc_ref[h],
                                                vt_ref[0, kt, g * HEAD_DIM:(g + 1) * HEAD_DIM, :], ones)
            acc_ref[h] = acc_new
            new_stats.append((m_new, l_new))
        return tie_carry + jnp.sum(eq_f, axis=0, keepdims=True), tuple(new_stats)

    init = (jnp.zeros((1, TT), f32),
            tuple((jnp.full((1, TT), NEG_BIG, f32), jnp.zeros((1, TT), f32)) for _ in range(A_HEADS)))
    _, stats = lax.fori_loop(0, n_kt, attend_tile, init)
    for h in range(A_HEADS):
        o_ref[0, :, h * HEAD_DIM:(h + 1) * HEAD_DIM] = (acc_ref[h] / stats[h][1]).T.astype(o_ref.dtype)


def _dsa(ops):
    B, T, W = ops['aq'].shape
    TT = min(TOK_TILE, T)
    NT = T // TT
    topk = min(DSA_TOPK, T // 4)
    return pl.pallas_call(
        functools.partial(_dsa_kernel, TT=TT, topk=topk),
        grid=(B, NT),
        in_specs=[
            pl.BlockSpec((1, TT, IDX_HEADS * IDX_DIM), lambda b, i: (b, i, 0)),
            pl.BlockSpec((1, SUBLANES, TT), lambda b, i: (b, 0, i)),
            pl.BlockSpec((1, TT, W), lambda b, i: (b, i, 0)),
            pl.BlockSpec((1, T, IDX_DIM), lambda b, i: (b, 0, 0)),
            pl.BlockSpec((1, T, A_KV_HEADS * HEAD_DIM), lambda b, i: (b, 0, 0)),
            pl.BlockSpec((1, NT, A_KV_HEADS * HEAD_DIM, TT), lambda b, i: (b, 0, 0, 0)),
        ],
        out_specs=pl.BlockSpec((1, TT, W), lambda b, i: (b, i, 0)),
        out_shape=jax.ShapeDtypeStruct((B, T, W), bf16),
        scratch_shapes=[pltpu.VMEM((T, TT), i32), pltpu.VMEM((A_HEADS, HEAD_DIM, TT), f32)],
        compiler_params=_params(("parallel", "arbitrary")),
        name='dsa_attention',
    )(ops['aiq'], ops['aiwt'], ops['aq'], ops['aik'], ops['ak'], ops['avt'])


def _moba_kernel(q_ref, qh_ref, ql_ref, kmh_ref, kml_ref, k_ref, vt_ref, o_ref, sel_ref, acc_ref,
                 *, NB, BS, n_sel, HP):
    i = pl.program_id(2)
    n_idx = lax.broadcasted_iota(i32, (NB, 1), 0)
    past = n_idx < i

    def head(h):
        return slice(h * HEAD_DIM, (h + 1) * HEAD_DIM)

    for h in range(HP):
        kmh, kml = kmh_ref[0, :, head(h)], kml_ref[0, :, head(h)]
        qh, ql = qh_ref[0, :, head(h)], ql_ref[0, :, head(h)]
        gate = _dot_nt(kmh, qh) + _dot_nt(kmh, ql) + _dot_nt(kml, qh)
        g = jnp.where(past, gate, -jnp.inf)
        sel = jnp.zeros((NB, BS), f32)
        for _ in range(n_sel):
            m = jnp.max(g, axis=0, keepdims=True)
            first = jnp.min(jnp.where(g == m, n_idx, NB), axis=0, keepdims=True)
            pick = (n_idx == first) & (m > -jnp.inf)
            sel = jnp.where(pick, 1.0, sel)
            g = jnp.where(pick, -jnp.inf, g)
        sel_ref[h] = sel

    acc_ref[...] = jnp.zeros_like(acc_ref)
    ones = jnp.ones((2 * SUBLANES, BS), bf16)

    def logits_of(n):
        ktile = k_ref[0, pl.ds(pl.multiple_of(n * BS, BS), BS), :]
        return tuple(_dot_nt(ktile[:, head(h)], q_ref[0, :, head(h)]) for h in range(HP))

    def block(n, logits, stats, step_fn):
        new_stats = []
        for h in range(HP):
            m_new, l_new, acc_new = step_fn(h, logits[h], *stats[h], acc_ref[h], vt_ref[0, n, head(h), :], ones)
            acc_ref[h] = acc_new
            new_stats.append((m_new, l_new))
        return tuple(new_stats)

    def past_block(n, carry):
        logits, stats = carry
        nxt = logits_of(n + 1)
        stats = block(n, logits, stats,
                      lambda h, s, *rest: _flash_step_cols(s, sel_ref[h, pl.ds(n, 1), :] > 0.5, *rest))
        return nxt, stats

    init = tuple((jnp.full((1, BS), NEG_BIG, f32), jnp.zeros((1, BS), f32)) for _ in range(HP))
    logits, stats = lax.fori_loop(0, i, past_block, (logits_of(0), init))
    causal = lax.broadcasted_iota(i32, (BS, BS), 0) <= lax.broadcasted_iota(i32, (BS, BS), 1)
    stats = block(i, logits, stats, lambda h, s, *rest: _flash_step(s, causal, *rest))
    for h in range(HP):
        o_ref[0, :, head(h)] = (acc_ref[h] / stats[h][1]).T.astype(o_ref.dtype)


def _moba(ops, *, HP=4):
    B, T, W = ops['dq'].shape
    BS = MOBA_BLOCK
    assert T % BS == 0 and TOK_TILE == BS
    NB = T // BS
    n_sel = min(MOBA_TOPK, NB - 1)
    WP = HP * HEAD_DIM
    kmh = ops['dkmh'].reshape(B, NB, W)
    kml = ops['dkml'].reshape(B, NB, W)
    qspec = pl.BlockSpec((1, BS, WP), lambda b, hg, i: (b, i, hg))
    kmspec = pl.BlockSpec((1, NB, WP), lambda b, hg, i: (b, 0, hg))
    return pl.pallas_call(
        functools.partial(_moba_kernel, NB=NB, BS=BS, n_sel=n_sel, HP=HP),
        grid=(B, W // WP, NB),
        in_specs=[qspec, qspec, qspec, kmspec, kmspec,
                  pl.BlockSpec((1, T, WP), lambda b, hg, i: (b, 0, hg)),
                  pl.BlockSpec((1, NB, WP, BS), lambda b, hg, i: (b, 0, hg, 0))],
        out_specs=qspec,
        out_shape=jax.ShapeDtypeStruct((B, T, W), bf16),
        scratch_shapes=[pltpu.VMEM((HP, NB, BS), f32), pltpu.VMEM((HP, HEAD_DIM, BS), f32)],
        compiler_params=_params(("parallel", "parallel", "arbitrary")),
        name='moba_attention',
    )(ops['dq'], ops['dqh'], ops['dql'], kmh, kml, ops['dk'], ops['dvt'])


def _gla_kernel(q_ref, k_ref, v_ref, glr_ref, r_ref, w2h_ref, w2l_ref, gb_ref, ng_ref, o_ref, st_ref, *, C, R):
    @pl.when(pl.program_id(1) == 0)
    def _():
        st_ref[...] = jnp.zeros_like(st_ref)

    gh, gl = _split2(glr_ref[0])
    logit = _dot(gh, w2h_ref[...]) + _dot(gh, w2l_ref[...]) + _dot(gl, w2h_ref[...]) + gb_ref[...]
    log_a = -_softplus(-logit) * (1.0 / GLA_TAU)
    tril = (lax.broadcasted_iota(i32, (C, C), 0) >= lax.broadcasted_iota(i32, (C, C), 1)).astype(bf16)
    a1, a2, a3 = _split3(log_a)
    b_all = _dot(tril, a1) + _dot(tril, a2) + _dot(tril, a3)
    for h in range(GLA_HEADS):
        ks = slice(h * GLA_DK, (h + 1) * GLA_DK)
        vs = slice(h * GLA_DV, (h + 1) * GLA_DV)
        b = b_all[:, ks]
        qh = q_ref[0, :, ks] * GLA_DK ** -0.5
        kh = k_ref[0, :, ks]
        vb = v_ref[0, :, vs].astype(bf16)
        st = st_ref[h]
        o = _dot_nt((qh * jnp.exp(b)).astype(bf16), st.astype(bf16))
        rows = []
        for blk in range(C // R):
            r0, r1 = blk * R, (blk + 1) * R
            b0 = b[r0 - 1:r0, :] if blk else jnp.zeros((1, GLA_DK), f32)
            qe = qh[r0:r1] * jnp.exp(b[r0:r1] - b0)
            ke = kh[:r1] * jnp.exp(b0 - b[:r1])
            att = _dot_nt(qe.astype(bf16), ke.astype(bf16))
            keep = lax.broadcasted_iota(i32, (R, r1), 1) <= lax.broadcasted_iota(i32, (R, r1), 0) + r0
            att = jnp.where(keep, att, 0.0)
            rows.append(_dot(att.astype(bf16), vb[:r1]))
        o = o + jnp.concatenate(rows, axis=0)
        b_last = b[C - 1:C, :]
        k_dec = kh * jnp.exp(b_last - b)
        st_ref[h] = st * jnp.exp(b_last) + _dot_tn(vb, k_dec.astype(bf16))
        y = o * lax.rsqrt(jnp.mean(o * o, axis=-1, keepdims=True) + NORM_EPS) * ng_ref[...]
        rh = r_ref[0, :, vs]
        o_ref[0, :, vs] = (y * (rh * _sigmoid(rh))).astype(o_ref.dtype)


def _gla(p, w2, gate_b, norm_g, *, C=128, R=32):
    B, T, _ = p.shape
    C = min(C, T)
    W = GLA_HEADS * GLA_DV
    w2p = jnp.zeros((LANES, GLA_HEADS * GLA_DK), f32).at[:GLA_GATE_RANK].set(w2)
    w2h, w2l = _split2(w2p)

    def seg(name):
        off, w = P_OFF[name]
        return pl.BlockSpec((1, C, w), lambda b, c: (b, c, off // w))

    full = lambda a: pl.BlockSpec(a.shape, lambda b, c: (0,) * a.ndim)
    gb = gate_b.reshape(1, -1)
    ng = norm_g.reshape(1, -1)
    return pl.pallas_call(
        functools.partial(_gla_kernel, C=C, R=min(R, C)),
        grid=(B, T // C),
        in_specs=[seg('b_q'), seg('b_k'), seg('b_v'), seg('b_glr'), seg('b_r'),
                  full(w2h), full(w2l), full(gb), full(ng)],
        out_specs=pl.BlockSpec((1, C, W), lambda b, c: (b, c, 0)),
        out_shape=jax.ShapeDtypeStruct((B, T, W), bf16),
        scratch_shapes=[pltpu.VMEM((GLA_HEADS, GLA_DV, GLA_DK), f32)],
        compiler_params=_params(("parallel", "arbitrary")),
        name='gla',
    )(p, p, p, p, p, w2h, w2l, gb, ng)


def _unit_lower_inverses(mats, row, col):
    C = mats[0].shape[0]
    eye = jnp.where(row == col, 1.0, 0.0)
    blk = (row >> 3) == (col >> 3)
    n1 = [jnp.where(blk, a, 0.0) for a in mats]
    n2 = [_mm3(x, x) for x in n1]
    n4 = [_mm3(x, x) for x in n2]
    t = [_mm3(eye - x1, eye + x2) for x1, x2 in zip(n1, n2)]
    t = [_mm3(x, eye + x4) for x, x4 in zip(t, n4)]
    s = SUBLANES
    while s < C:
        sh = s.bit_length() - 1
        sel = ((row >> (sh + 1)) == (col >> (sh + 1))) & ((row >> sh) != (col >> sh))
        left = [_mm1(x, jnp.where(sel, a, 0.0)) for x, a in zip(t, mats)]
        t = [x - _mm1(y, x) for x, y in zip(t, left)]
        s *= 2
    return t


def _gdn_kernel(q_ref, k_ref, v_ref, ab_ref, gate_ref, alog_ref, dtb_ref, ng_ref, o_ref, s_ref, *, C):
    @pl.when(pl.program_id(1) == 0)
    def _():
        s_ref[...] = jnp.zeros_like(s_ref)

    H = GDN_HEADS
    row = lax.broadcasted_iota(i32, (C, C), 0)
    col = lax.broadcasted_iota(i32, (C, C), 1)
    ab = ab_ref[0]
    g_all = -jnp.exp(alog_ref[...]) * _softplus(ab + dtb_ref[...])
    beta_all = _sigmoid(ab)
    tril = (row >= col).astype(bf16)
    g1, g2, g3 = _split3(g_all)
    gam_all = _dot(tril, g1) + _dot(tril, g2) + _dot(tril, g3)
    gam_t = gam_all.T
    heads = range(H)
    sls = [slice(h * HEAD_DIM, (h + 1) * HEAD_DIM) for h in heads]
    q = [q_ref[0, :, sl] for sl in sls]
    k = [k_ref[0, :, sl] for sl in sls]
    kb = [x.astype(bf16) for x in k]
    gcol = [gam_all[:, h:h + 1] for h in heads]
    bcol = [beta_all[:, H + h:H + h + 1] for h in heads]
    egam = [jnp.exp(x) for x in gcol]
    dec = [jnp.where(row >= col, jnp.exp(jnp.minimum(gcol[h] - gam_t[h:h + 1, :], 0.0)), 0.0) for h in heads]
    kk = [_dot_nt(x, x) for x in kb]
    qk = [_dot_nt(q[h].astype(bf16), kb[h]) for h in heads]
    a = [jnp.where(row > col, bcol[h] * dec[h] * kk[h], 0.0) for h in heads]
    tinv = _unit_lower_inverses(a, row, col)
    sol = [_mm1(tinv[h], jnp.concatenate([bcol[h] * v_ref[0, :, sls[h]], (bcol[h] * egam[h]) * k[h]], axis=1))
           for h in heads]
    sb = [s_ref[h].astype(bf16) for h in heads]
    ws = [_dot(sol[h][:, HEAD_DIM:].astype(bf16), sb[h]) for h in heads]
    qs = [_dot((q[h] * egam[h]).astype(bf16), sb[h]) for h in heads]
    db = [(sol[h][:, :HEAD_DIM] - ws[h]).astype(bf16) for h in heads]
    o = [qs[h] + _dot((qk[h] * dec[h]).astype(bf16), db[h]) for h in heads]
    for h in heads:
        g_last = gcol[h][C - 1:C, :]
        k_dec = k[h] * jnp.exp(g_last - gcol[h])
        s_ref[h] = jnp.exp(g_last) * s_ref[h] + _dot_tn(k_dec.astype(bf16), db[h])
    for h in heads:
        y = o[h] * lax.rsqrt(jnp.mean(o[h] * o[h], axis=-1, keepdims=True) + NORM_EPS) * ng_ref[...]
        gt = gate_ref[0, :, sls[h]]
        o_ref[0, :, sls[h]] = (y * (gt * _sigmoid(gt))).astype(o_ref.dtype)


def _gdn(cq, ck, cv, p, a_log, dt_bias, norm_g, *, C=128):
    B, T, W = cq.shape
    C = min(C, T)
    H = GDN_HEADS
    alog = jnp.zeros((1, LANES), f32).at[0, :H].set(a_log)
    dtb = jnp.zeros((1, LANES), f32).at[0, :H].set(dt_bias)
    ng = norm_g.reshape(1, -1)
    tok = pl.BlockSpec((1, C, W), lambda b, c: (b, c, 0))

    def seg(name):
        off, w = P_OFF[name]
        return pl.BlockSpec((1, C, w), lambda b, c: (b, c, off // w))

    full = lambda a: pl.BlockSpec(a.shape, lambda b, c: (0,) * a.ndim)
    return pl.pallas_call(
        functools.partial(_gdn_kernel, C=C),
        grid=(B, T // C),
        in_specs=[tok, tok, tok, seg('c_ab'), seg('c_g'), full(alog), full(dtb), full(ng)],
        out_specs=tok,
        out_shape=jax.ShapeDtypeStruct((B, T, W), bf16),
        scratch_shapes=[pltpu.VMEM((H, HEAD_DIM, HEAD_DIM), f32)],
        compiler_params=_params(("parallel", "arbitrary")),
        name='gated_delta_rule',
    )(cq, ck, cv, p, p, alog, dtb, ng)


def _pack_kernel(w_ref, o_ref):
    for name, width in P_LAYOUT:
        dst, _ = P_OFF[name]
        src, w = (REF_OFF['c_a'][0], 2 * GDN_HEADS) if name == 'c_ab' else REF_OFF[name]
        if w == width:
            o_ref[0, :, dst:dst + width] = w_ref[0, :, src:src + w].astype(bf16)
        else:
            base = src // LANES * LANES
            win = w_ref[0, :, base:base + LANES]
            lane = lax.broadcasted_iota(i32, win.shape, 1)
            win = jnp.where((lane >= src - base) & (lane < src - base + w), win, 0.0)
            o_ref[0, :, dst:dst + width] = pltpu.roll(win, (LANES - (src - base)) % LANES, 1).astype(bf16)


def _pack_w_in(w_in, *, tr=256):
    L, D, d_in = w_in.shape
    tr = min(tr, D)
    for name, width in P_LAYOUT:
        src, w = (REF_OFF['c_a'][0], 2 * GDN_HEADS) if name == 'c_ab' else REF_OFF[name]
        assert w == width or (width == LANES and src // LANES == (src + w - 1) // LANES)
    return pl.pallas_call(
        _pack_kernel,
        grid=(L, D // tr),
        in_specs=[pl.BlockSpec((1, tr, d_in), lambda l, i: (l, i, 0))],
        out_specs=pl.BlockSpec((1, tr, P_WIDTH), lambda l, i: (l, i, 0)),
        out_shape=jax.ShapeDtypeStruct((L, D, P_WIDTH), bf16),
        compiler_params=_params(("parallel", "parallel")),
        name='pack_w_in',
    )(w_in)


def _token_mixers(hb, B, T, layer, w_in, gla_gate_w2, gla_gate_b, gla_norm_g, gdn_conv_w, gdn_a_log, gdn_dt_bias,
                  gdn_norm_g, cosf, sinf):
    p = _matmul(hb, w_in, layer, tm=1024, tn=512, tk=hb.shape[1], out_dtype=f32, name='in_proj')
    p = p.reshape(B, T, P_WIDTH)
    ops = _prep(p, cosf, sinf, gdn_conv_w)
    y_a = _dsa(ops)
    y_b = _gla(p, gla_gate_w2, gla_gate_b, gla_norm_g)
    y_c = _gdn(ops['cq'], ops['ck'], ops['cv'], p, gdn_a_log, gdn_dt_bias, gdn_norm_g)
    y_d = _moba(ops)
    return tuple(y.reshape(B * T, -1) for y in (y_a, y_b, y_c, y_d))


def _rope_tables(T):
    inv = ROPE_THETA ** (-jnp.arange(0, HEAD_DIM, 2, dtype=f32) / HEAD_DIM)
    ang = jnp.arange(T, dtype=f32)[:, None] * inv[None, :]
    cos, sin = jnp.cos(ang), jnp.sin(ang)
    return jnp.concatenate([cos, cos], axis=1), jnp.concatenate([-sin, sin], axis=1)


def kernel(x, ln_in_g, ln_in_b, w_in, w_out, ln1_g, ln1_b, gla_gate_w2, gla_gate_b, gla_norm_g, gdn_conv_w,
           gdn_a_log, gdn_dt_bias, gdn_norm_g, w_up, w_down, ln2_g, ln2_b):
    B, T, D = x.shape
    depth = w_in.shape[0]
    alpha = (2 * depth) ** 0.25
    cosf, sinf = _rope_tables(T)
    h, hb = _layer_norm(x.reshape(B * T, D), None, ln_in_g, ln_in_b)
    w_in_p = _pack_w_in(w_in)
    w_out_b, w_up_b, w_down_b = w_out.astype(bf16), w_up.astype(bf16), w_down.astype(bf16)
    for l in range(depth):
        mix = _token_mixers(hb, B, T, l, w_in_p, gla_gate_w2[l], gla_gate_b[l], gla_norm_g[l], gdn_conv_w[l],
                            gdn_a_log[l], gdn_dt_bias[l], gdn_norm_g[l], cosf, sinf)
        y = _matmul(mix, w_out_b, l, tm=1024, tn=512, tk=w_out.shape[1], out_dtype=f32, name='out_proj')
        h, hb = _layer_norm(h, y, ln1_g[l], ln1_b[l], alpha=alpha)
        up = _matmul(hb, w_up_b, l, tm=1024, tn=512, tk=D, out_dtype=bf16, act='relu2', name='mlp_up')
        ff = _matmul(up, w_down_b, l, tm=1024, tn=2048, tk=512, out_dtype=f32, name='mlp_down')
        h, hb = _layer_norm(h, ff, ln2_g[l], ln2_b[l], alpha=alpha)
    return h.reshape(B, T, D)
```

```python
import functools
import math

import jax
import jax.numpy as jnp
from jax import lax
from jax.experimental import pallas as pl
from jax.experimental.pallas import tpu as pltpu

f32 = jnp.float32
bf16 = jnp.bfloat16
i32 = jnp.int32

HEAD_DIM = 128
A_HEADS = 8
A_KV_HEADS = 2
IDX_HEADS = 4
IDX_DIM = 128
DSA_TOPK = 256
GLA_HEADS = 4
GLA_DK = 128
GLA_DV = 256
GLA_GATE_RANK = 16
GLA_TAU = 16.0
GDN_HEADS = 8
CONV_K = 4
MOBA_HEADS = 8
MOBA_BLOCK = 256
MOBA_TOPK = 3
ROPE_THETA = 10000.0
LN_EPS = 1e-5
NORM_EPS = 1e-6

V7X_VMEM_BYTES = 64 * 2**20
VMEM_LIMIT = V7X_VMEM_BYTES * 3 // 4
LANES = 128
SUBLANES = 8

INT_MIN = -2**31
NEG_BIG = -1e30
LOG2E = math.log2(math.e)
ATTN_Q_SCALE = HEAD_DIM ** -0.5 * LOG2E

TOK_TILE = MOBA_BLOCK

P_LAYOUT = (
    ('a_q', 1024), ('b_v', 1024), ('b_r', 1024), ('c_q', 1024), ('c_k', 1024), ('c_v', 1024), ('c_g', 1024),
    ('d_q', 1024), ('d_k', 1024), ('d_v', 1024),
    ('a_iq', 512), ('b_q', 512), ('b_k', 512),
    ('a_k', 256), ('a_v', 256),
    ('a_ik', 128), ('a_iw', 128), ('b_glr', 128), ('c_ab', 128),
)
P_OFF = {}
_o = 0
for _n, _w in P_LAYOUT:
    P_OFF[_n] = (_o, _w)
    _o += _w
P_WIDTH = _o

REF_SPLITS = (
    ('a_q', 1024), ('a_k', 256), ('a_v', 256), ('a_iq', 512), ('a_ik', 128), ('a_iw', 4),
    ('b_q', 512), ('b_k', 512), ('b_v', 1024), ('b_glr', 16), ('b_r', 1024),
    ('c_q', 1024), ('c_k', 1024), ('c_v', 1024), ('c_a', 8), ('c_b', 8), ('c_g', 1024),
    ('d_q', 1024), ('d_k', 1024), ('d_v', 1024),
)
REF_OFF = {}
_o = 0
for _n, _w in REF_SPLITS:
    REF_OFF[_n] = (_o, _w)
    _o += _w


def _dot(a, b):
    return jnp.dot(a, b, preferred_element_type=f32)


def _dot_nt(a, b):
    return lax.dot_general(a, b, (((1,), (1,)), ((), ())), preferred_element_type=f32)


def _dot_tn(a, b):
    return lax.dot_general(a, b, (((0,), (0,)), ((), ())), preferred_element_type=f32)


def _split2(x):
    hi = x.astype(bf16)
    lo = (x - hi.astype(f32)).astype(bf16)
    return hi, lo


def _split3(x):
    hi = x.astype(bf16)
    r = x - hi.astype(f32)
    mid = r.astype(bf16)
    lo = (r - mid.astype(f32)).astype(bf16)
    return hi, mid, lo


def _mm1(x, y):
    return _dot(x.astype(bf16), y.astype(bf16))


def _mm3(x, y):
    xh, xl = _split2(x)
    yh, yl = _split2(y)
    return _dot(xh, yh) + _dot(xh, yl) + _dot(xl, yh)


def _sigmoid(x):
    return 1.0 / (1.0 + jnp.exp(-x))


def _softplus(x):
    return jnp.maximum(x, 0.0) + jnp.log1p(jnp.exp(-jnp.abs(x)))


def _params(sem):
    return pltpu.CompilerParams(dimension_semantics=sem, vmem_limit_bytes=VMEM_LIMIT)


def _mm_kernel(*refs, n_a, nk, act):
    a_refs, b_ref, o_ref = refs[:n_a], refs[n_a], refs[n_a + 1]

    def product():
        if n_a == 1:
            return _dot(a_refs[0][...], b_ref[...])
        kw = b_ref.shape[0] // n_a
        out = _dot(a_refs[0][...], b_ref[0:kw, :])
        for g in range(1, n_a):
            out = out + _dot(a_refs[g][...], b_ref[g * kw:(g + 1) * kw, :])
        return out

    def finish(r):
        if act == 'relu2':
            r = jnp.square(jnp.maximum(r, 0.0))
        o_ref[...] = r.astype(o_ref.dtype)

    if nk == 1:
        finish(product())
        return
    acc_ref = refs[n_a + 2]
    k = pl.program_id(2)

    @pl.when(k == 0)
    def _():
        acc_ref[...] = jnp.zeros_like(acc_ref)

    acc_ref[...] += product()

    @pl.when(k == nk - 1)
    def _():
        finish(acc_ref[...])


def _matmul(a, b, layer, *, tm, tn, tk, out_dtype, act=None, name='matmul'):
    a_list = a if isinstance(a, (tuple, list)) else (a,)
    n_a = len(a_list)
    M = a_list[0].shape[0]
    _, K, N = b.shape
    tm, tn, tk = min(tm, M), min(tn, N), min(tk, K)
    assert M % tm == 0 and N % tn == 0 and K % tk == 0
    nk = K // tk
    assert n_a == 1 or nk == 1
    ka = tk // n_a
    return pl.pallas_call(
        functools.partial(_mm_kernel, n_a=n_a, nk=nk, act=act),
        grid=(M // tm, N // tn, nk),
        in_specs=[pl.BlockSpec((tm, ka), lambda i, j, k: (i, k))] * n_a
                 + [pl.BlockSpec((None, tk, tn), lambda i, j, k: (layer, k, j))],
        out_specs=pl.BlockSpec((tm, tn), lambda i, j, k: (i, j)),
        out_shape=jax.ShapeDtypeStruct((M, N), out_dtype),
        scratch_shapes=[pltpu.VMEM((tm, tn), f32)] if nk > 1 else [],
        compiler_params=_params(("parallel", "parallel", "arbitrary")),
        name=name,
    )(*a_list, b)


def _ln_kernel(*refs, alpha, has_y):
    if has_y:
        x_ref, y_ref, g_ref, b_ref, o_ref, ob_ref = refs
        z = x_ref[...] * alpha + y_ref[...]
    else:
        x_ref, g_ref, b_ref, o_ref, ob_ref = refs
        z = x_ref[...]
    mu = jnp.mean(z, axis=-1, keepdims=True)
    zc = z - mu
    var = jnp.mean(zc * zc, axis=-1, keepdims=True)
    out = zc * lax.rsqrt(var + LN_EPS) * g_ref[...] + b_ref[...]
    o_ref[...] = out
    ob_ref[...] = out.astype(bf16)


def _layer_norm(x, y, g, b, *, alpha=1.0, tm=256):
    M, D = x.shape
    tm = min(tm, M)
    row = pl.BlockSpec((tm, D), lambda i: (i, 0))
    vec = pl.BlockSpec((1, D), lambda i: (0, 0))
    has_y = y is not None
    args = (x, y) if has_y else (x,)
    return pl.pallas_call(
        functools.partial(_ln_kernel, alpha=alpha, has_y=has_y),
        grid=(M // tm,),
        in_specs=[row] * len(args) + [vec, vec],
        out_specs=[row, row],
        out_shape=[jax.ShapeDtypeStruct((M, D), f32), jax.ShapeDtypeStruct((M, D), bf16)],
        compiler_params=_params(("parallel",)),
        name='layer_norm',
    )(*args, g.reshape(1, D), b.reshape(1, D))


def _prep_kernel(aq_ref, ak_ref, av_ref, aiq_ref, aik_ref, aiw_ref,
                 cq_ref, ck_ref, cv_ref, cqh_ref, ckh_ref, cvh_ref,
                 dq_ref, dk_ref, dv_ref, cos_ref, sin_ref, cw_ref,
                 oaq, oak, oavt, oaiq, oaik, oaiwt, ocq, ock, ocv,
                 odq, odqh, odql, odk, odvt, odkmh, odkml, *, TT):
    cosf, sinf = cos_ref[...], sin_ref[...]

    def rope(x):
        return x * cosf + pltpu.roll(x, HEAD_DIM // 2, 1) * sinf

    def head(h):
        return slice(h * HEAD_DIM, (h + 1) * HEAD_DIM)

    for h in range(A_HEADS):
        oaq[0, :, head(h)] = (rope(aq_ref[0, :, head(h)]) * ATTN_Q_SCALE).astype(bf16)
    for g in range(A_KV_HEADS):
        oak[0, :, head(g)] = rope(ak_ref[0, :, head(g)]).astype(bf16)
        oavt[0, 0, head(g), :] = av_ref[0, :, head(g)].T.astype(bf16)
    for h in range(IDX_HEADS):
        oaiq[0, :, head(h)] = rope(aiq_ref[0, :, head(h)]).astype(bf16)
    oaik[0] = rope(aik_ref[0]).astype(bf16)
    oaiwt[0] = aiw_ref[0].T[:SUBLANES, :] * (IDX_HEADS ** -0.5 * IDX_DIM ** -0.5)

    first = pl.program_id(1) == 0
    for j, (x_ref, h_ref, o_ref) in enumerate(((cq_ref, cqh_ref, ocq), (ck_ref, ckh_ref, ock),
                                                (cv_ref, cvh_ref, ocv))):
        width = x_ref.shape[2]
        halo = jnp.where(first, 0.0, h_ref[0])
        cat = jnp.concatenate([halo, x_ref[0]], axis=0)
        w = cw_ref[:, j * width:(j + 1) * width]
        y = cat[SUBLANES - CONV_K + 1:SUBLANES - CONV_K + 1 + TT] * w[0:1]
        for i in range(1, CONV_K):
            s0 = SUBLANES - CONV_K + 1 + i
            y = y + cat[s0:s0 + TT] * w[i:i + 1]
        y = y * _sigmoid(y)
        if j == 2:
            o_ref[0] = y
        else:
            post = HEAD_DIM ** -0.5 if j == 0 else 1.0
            for h in range(GDN_HEADS):
                yh = y[:, head(h)]
                o_ref[0, :, head(h)] = yh * (lax.rsqrt(jnp.sum(yh * yh, axis=-1, keepdims=True) + NORM_EPS) * post)

    for h in range(MOBA_HEADS):
        qr = rope(dq_ref[0, :, head(h)])
        odq[0, :, head(h)] = (qr * ATTN_Q_SCALE).astype(bf16)
        qh, ql = _split2(qr)
        odqh[0, :, head(h)] = qh
        odql[0, :, head(h)] = ql
        kr = rope(dk_ref[0, :, head(h)])
        odk[0, :, head(h)] = kr.astype(bf16)
        kmh, kml = _split2(jnp.mean(kr, axis=0, keepdims=True))
        odkmh[0, 0, :, head(h)] = kmh
        odkml[0, 0, :, head(h)] = kml
        odvt[0, 0, head(h), :] = dv_ref[0, :, head(h)].T.astype(bf16)


def _prep(p, cosf, sinf, conv_w):
    B, T, _ = p.shape
    TT = min(TOK_TILE, T)
    NT = T // TT

    def seg(name, rows=TT):
        off, w = P_OFF[name]
        return pl.BlockSpec((1, rows, w), lambda b, i: (b, i, off // w))

    def halo(name):
        off, w = P_OFF[name]
        return pl.BlockSpec((1, SUBLANES, w),
                            lambda b, i: (b, jnp.maximum(i * (TT // SUBLANES) - 1, 0), off // w))

    in_names = ('a_q', 'a_k', 'a_v', 'a_iq', 'a_ik', 'a_iw', 'c_q', 'c_k', 'c_v')
    in_specs = ([seg(n) for n in in_names] + [halo(n) for n in ('c_q', 'c_k', 'c_v')]
                + [seg(n) for n in ('d_q', 'd_k', 'd_v')]
                + [pl.BlockSpec((TT, HEAD_DIM), lambda b, i: (i, 0))] * 2
                + [pl.BlockSpec(conv_w.shape, lambda b, i: (0, 0))])

    def tok(w, dt):
        return pl.BlockSpec((1, TT, w), lambda b, i: (b, i, 0)), jax.ShapeDtypeStruct((B, T, w), dt)

    def tposed(rows, dt):
        return (pl.BlockSpec((1, 1, rows, TT), lambda b, i: (b, i, 0, 0)),
                jax.ShapeDtypeStruct((B, NT, rows, TT), dt))

    def per_tile(w, dt):
        return (pl.BlockSpec((1, 1, 1, w), lambda b, i: (b, i, 0, 0)),
                jax.ShapeDtypeStruct((B, NT, 1, w), dt))

    outs = [
        tok(1024, bf16), tok(256, bf16), tposed(256, bf16), tok(512, bf16), tok(128, bf16),
        (pl.BlockSpec((1, SUBLANES, TT), lambda b, i: (b, 0, i)), jax.ShapeDtypeStruct((B, SUBLANES, T), f32)),
        tok(1024, f32), tok(1024, f32), tok(1024, f32),
        tok(1024, bf16), tok(1024, bf16), tok(1024, bf16), tok(1024, bf16), tposed(1024, bf16),
        per_tile(1024, bf16), per_tile(1024, bf16),
    ]
    res = pl.pallas_call(
        functools.partial(_prep_kernel, TT=TT),
        grid=(B, NT),
        in_specs=in_specs,
        out_specs=[o[0] for o in outs],
        out_shape=[o[1] for o in outs],
        compiler_params=_params(("parallel", "arbitrary")),
        name='mixer_prep',
    )(*([p] * 15), cosf, sinf, conv_w)
    names = ('aq', 'ak', 'avt', 'aiq', 'aik', 'aiwt', 'cq', 'ck', 'cv', 'dq', 'dqh', 'dql', 'dk', 'dvt',
             'dkmh', 'dkml')
    return dict(zip(names, res))


def _tiles_by_pairs(n, produce, consume, state):
    odd = n % 2

    def single(j, st):
        return consume(j, produce(j), st)

    def pair(m, st):
        j = odd + 2 * m
        first, second = produce(j), produce(j + 1)
        return consume(j + 1, second, consume(j, first, st))

    state = lax.fori_loop(0, odd, single, state)
    return lax.fori_loop(0, n // 2, pair, state)


def _flash_update(p, m_prev, m_new, l_prev, acc_prev, vt, ones):
    pb = p.astype(bf16)
    a = jnp.exp2(m_prev - m_new)
    l_new = a * l_prev + _dot(ones, pb)[0:1]
    acc_new = a * acc_prev + _dot(vt, pb)
    return m_new, l_new, acc_new


def _flash_step(s, mask, m_prev, l_prev, acc_prev, vt, ones):
    s = jnp.where(mask, s, NEG_BIG)
    m_new = jnp.maximum(m_prev, jnp.max(s, axis=0, keepdims=True))
    return _flash_update(jnp.exp2(s - m_new), m_prev, m_new, l_prev, acc_prev, vt, ones)


def _flash_step_cols(s, cols, m_prev, l_prev, acc_prev, vt, ones):
    m_new = jnp.where(cols, jnp.maximum(m_prev, jnp.max(s, axis=0, keepdims=True)), m_prev)
    p = jnp.exp2(s - jnp.where(cols, m_new, -NEG_BIG))
    return _flash_update(p, m_prev, m_new, l_prev, acc_prev, vt, ones)


def _dsa_kernel(iq_ref, iw_ref, q_ref, ik_ref, k_ref, vt_ref, o_ref, keys_ref, acc_ref, *, TT, topk):
    i = pl.program_id(1)
    n_kt = i + 1
    t_idx = i * TT + lax.broadcasted_iota(i32, (1, TT), 1)
    row = lax.broadcasted_iota(i32, (TT, 1), 0)
    iw = iw_ref[0]
    iq = iq_ref[0]

    def tile(kt):
        return pl.ds(pl.multiple_of(kt * TT, TT), TT)

    def head_scores(kt):
        ik = ik_ref[0, tile(kt), :]
        return [_dot_nt(ik, iq[:, h * IDX_DIM:(h + 1) * IDX_DIM]) for h in range(IDX_HEADS)]

    def store_keys(kt, xs, carry):
        acc = jnp.zeros((TT, TT), f32)
        for h in range(IDX_HEADS):
            acc = acc + iw[h:h + 1, :] * jnp.maximum(xs[h], 0.0)
        bits = lax.bitcast_convert_type(acc, i32)
        key = bits ^ ((bits >> 31) & jnp.int32(0x7FFFFFFF))
        key = jnp.where(key == -1, 0, key)
        key = jnp.where(kt * TT + row <= t_idx, key, jnp.int32(INT_MIN))
        keys_ref[tile(kt), :] = key
        return carry

    _tiles_by_pairs(n_kt, head_scores, store_keys, 0)

    def count(pred_fn):
        def body(kt, c):
            ind = pred_fn(keys_ref[tile(kt), :]).astype(i32)
            return c + jnp.sum(ind.reshape(TT // SUBLANES, SUBLANES, TT), axis=0)
        c = lax.fori_loop(0, n_kt, body, jnp.zeros((SUBLANES, TT), i32))
        return jnp.sum(c, axis=0, keepdims=True)

    n_nonneg = count(lambda key: key >= 0)
    nonneg = n_nonneg >= topk
    settled = (t_idx + 1 <= topk) | (nonneg & (count(lambda key: key > 0) < topk))

    def pending(cnt_cur):
        return jnp.max(jnp.where((cnt_cur == topk) | settled, 0, 1))

    def bit_pass(state):
        bi, cur, cnt_cur, _ = state
        cand = cur + lax.shift_left(jnp.int32(1), 31 - bi)
        cnt = count(lambda key: key >= cand)
        ok = cnt >= topk
        cur = jnp.where(ok, cand, cur)
        cnt_cur = jnp.where(ok, cnt, cnt_cur)
        return bi + 1, cur, cnt_cur, pending(cnt_cur)

    cnt0 = jnp.where(nonneg, n_nonneg, n_kt * TT)
    state = (jnp.int32(1), jnp.where(nonneg, 0, jnp.int32(INT_MIN)), cnt0, pending(cnt0))
    _, thr, _, _ = lax.while_loop(lambda st: (st[0] < 32) & (st[3] > 0), bit_pass, state)
    need = (topk - count(lambda key: key > thr)).astype(f32)

    tri = (lax.broadcasted_iota(i32, (TT, TT), 0) > lax.broadcasted_iota(i32, (TT, TT), 1)).astype(bf16)
    ones = jnp.ones((2 * SUBLANES, TT), bf16)
    group = A_HEADS // A_KV_HEADS
    acc_ref[...] = jnp.zeros_like(acc_ref)

    def attend_tile(kt, carry):
        tie_carry, stats = carry
        key = keys_ref[tile(kt), :]
        eq = (key == thr) & (key != INT_MIN)
        eq_f = jnp.where(eq, 1.0, 0.0)
        tie_rank = _dot(tri, eq_f.astype(bf16)) + tie_carry
        mask = (key > thr) | (eq & (tie_rank < need))
        ktile = k_ref[0, tile(kt), :]
        logits = [_dot_nt(ktile[:, (h // group) * HEAD_DIM:(h // group + 1) * HEAD_DIM],
                          q_ref[0, :, h * HEAD_DIM:(h + 1) * HEAD_DIM]) for h in range(A_HEADS)]
        new_stats = []
        for h in range(A_HEADS):
            g = h // group
            m_new, l_new, acc_new = _flash_step(logits[h], mask, *stats[h], acc_ref[h],
                                                vt_ref[0, kt, g * HEAD_DIM:(g + 1) * HEAD_DIM, :], ones)
            acc_ref[h] = acc_new
            new_stats.append((m_new, l_new))
        return tie_carry + jnp.sum(eq_f, axis=0, keepdims=True), tuple(new_stats)

    init = (jnp.zeros((1, TT), f32),
            tuple((jnp.full((1, TT), NEG_BIG, f32), jnp.zeros((1, TT), f32)) for _ in range(A_HEADS)))
    _, stats = lax.fori_loop(0, n_kt, attend_tile, init)
    for h in range(A_HEADS):
        o_ref[0, :, h * HEAD_DIM:(h + 1) * HEAD_DIM] = (acc_ref[h] / stats[h][1]).T.astype(o_ref.dtype)


def _dsa(ops):
    B, T, W = ops['aq'].shape
    TT = min(TOK_TILE, T)
    NT = T // TT
    topk = min(DSA_TOPK, T // 4)
    return pl.pallas_call(
        functools.partial(_dsa_kernel, TT=TT, topk=topk),
        grid=(B, NT),
        in_specs=[
            pl.BlockSpec((1, TT, IDX_HEADS * IDX_DIM), lambda b, i: (b, i, 0)),
            pl.BlockSpec((1, SUBLANES, TT), lambda b, i: (b, 0, i)),
            pl.BlockSpec((1, TT, W), lambda b, i: (b, i, 0)),
            pl.BlockSpec((1, T, IDX_DIM), lambda b, i: (b, 0, 0)),
            pl.BlockSpec((1, T, A_KV_HEADS * HEAD_DIM), lambda b, i: (b, 0, 0)),
            pl.BlockSpec((1, NT, A_KV_HEADS * HEAD_DIM, TT), lambda b, i: (b, 0, 0, 0)),
        ],
        out_specs=pl.BlockSpec((1, TT, W), lambda b, i: (b, i, 0)),
        out_shape=jax.ShapeDtypeStruct((B, T, W), bf16),
        scratch_shapes=[pltpu.VMEM((T, TT), i32), pltpu.VMEM((A_HEADS, HEAD_DIM, TT), f32)],
        compiler_params=_params(("parallel", "arbitrary")),
        name='dsa_attention',
    )(ops['aiq'], ops['aiwt'], ops['aq'], ops['aik'], ops['ak'], ops['avt'])


def _moba_kernel(q_ref, qh_ref, ql_ref, kmh_ref, kml_ref, k_ref, vt_ref, o_ref, sel_ref, acc_ref,
                 *, NB, BS, n_sel, HP):
    i = pl.program_id(2)
    n_idx = lax.broadcasted_iota(i32, (NB, 1), 0)
    past = n_idx < i

    def head(h):
        return slice(h * HEAD_DIM, (h + 1) * HEAD_DIM)

    for h in range(HP):
        kmh, kml = kmh_ref[0, :, head(h)], kml_ref[0, :, head(h)]
        qh, ql = qh_ref[0, :, head(h)], ql_ref[0, :, head(h)]
        gate = _dot_nt(kmh, qh) + _dot_nt(kmh, ql) + _dot_nt(kml, qh)
        g = jnp.where(past, gate, -jnp.inf)
        sel = jnp.zeros((NB, BS), f32)
        for _ in range(n_sel):
            m = jnp.max(g, axis=0, keepdims=True)
            first = jnp.min(jnp.where(g == m, n_idx, NB), axis=0, keepdims=True)
            pick = (n_idx == first) & (m > -jnp.inf)
            sel = jnp.where(pick, 1.0, sel)
            g = jnp.where(pick, -jnp.inf, g)
        sel_ref[h] = sel

    acc_ref[...] = jnp.zeros_like(acc_ref)
    ones = jnp.ones((2 * SUBLANES, BS), bf16)

    def logits_of(n):
        ktile = k_ref[0, pl.ds(pl.multiple_of(n * BS, BS), BS), :]
        return tuple(_dot_nt(ktile[:, head(h)], q_ref[0, :, head(h)]) for h in range(HP))

    def block(n, logits, stats, step_fn):
        new_stats = []
        for h in range(HP):
            m_new, l_new, acc_new = step_fn(h, logits[h], *stats[h], acc_ref[h], vt_ref[0, n, head(h), :], ones)
            acc_ref[h] = acc_new
            new_stats.append((m_new, l_new))
        return tuple(new_stats)

    def past_block(n, logits, stats):
        return block(n, logits, stats,
                     lambda h, s, *rest: _flash_step_cols(s, sel_ref[h, pl.ds(n, 1), :] > 0.5, *rest))

    init = tuple((jnp.full((1, BS), NEG_BIG, f32), jnp.zeros((1, BS), f32)) for _ in range(HP))
    stats = _tiles_by_pairs(i, logits_of, past_block, init)
    causal = lax.broadcasted_iota(i32, (BS, BS), 0) <= lax.broadcasted_iota(i32, (BS, BS), 1)
    stats = block(i, logits_of(i), stats, lambda h, s, *rest: _flash_step(s, causal, *rest))
    for h in range(HP):
        o_ref[0, :, head(h)] = (acc_ref[h] / stats[h][1]).T.astype(o_ref.dtype)


def _moba(ops, *, HP=4):
    B, T, W = ops['dq'].shape
    BS = MOBA_BLOCK
    assert T % BS == 0 and TOK_TILE == BS
    NB = T // BS
    n_sel = min(MOBA_TOPK, NB - 1)
    WP = HP * HEAD_DIM
    kmh = ops['dkmh'].reshape(B, NB, W)
    kml = ops['dkml'].reshape(B, NB, W)
    qspec = pl.BlockSpec((1, BS, WP), lambda b, hg, i: (b, i, hg))
    kmspec = pl.BlockSpec((1, NB, WP), lambda b, hg, i: (b, 0, hg))
    return pl.pallas_call(
        functools.partial(_moba_kernel, NB=NB, BS=BS, n_sel=n_sel, HP=HP),
        grid=(B, W // WP, NB),
        in_specs=[qspec, qspec, qspec, kmspec, kmspec,
                  pl.BlockSpec((1, T, WP), lambda b, hg, i: (b, 0, hg)),
                  pl.BlockSpec((1, NB, WP, BS), lambda b, hg, i: (b, 0, hg, 0))],
        out_specs=qspec,
        out_shape=jax.ShapeDtypeStruct((B, T, W), bf16),
        scratch_shapes=[pltpu.VMEM((HP, NB, BS), f32), pltpu.VMEM((HP, HEAD_DIM, BS), f32)],
        compiler_params=_params(("parallel", "parallel", "arbitrary")),
        name='moba_attention',
    )(ops['dq'], ops['dqh'], ops['dql'], kmh, kml, ops['dk'], ops['dvt'])


def _gla_kernel(q_ref, k_ref, v_ref, glr_ref, r_ref, w2h_ref, w2l_ref, gb_ref, ng_ref, o_ref, st_ref, *, C, R):
    @pl.when(pl.program_id(1) == 0)
    def _():
        st_ref[...] = jnp.zeros_like(st_ref)

    gh, gl = _split2(glr_ref[0])
    logit = _dot(gh, w2h_ref[...]) + _dot(gh, w2l_ref[...]) + _dot(gl, w2h_ref[...]) + gb_ref[...]
    log_a = -_softplus(-logit) * (1.0 / GLA_TAU)
    tril = (lax.broadcasted_iota(i32, (C, C), 0) >= lax.broadcasted_iota(i32, (C, C), 1)).astype(bf16)
    a1, a2, a3 = _split3(log_a)
    b_all = _dot(tril, a1) + _dot(tril, a2) + _dot(tril, a3)
    for h in range(GLA_HEADS):
        ks = slice(h * GLA_DK, (h + 1) * GLA_DK)
        vs = slice(h * GLA_DV, (h + 1) * GLA_DV)
        b = b_all[:, ks]
        qh = q_ref[0, :, ks] * GLA_DK ** -0.5
        kh = k_ref[0, :, ks]
        vb = v_ref[0, :, vs].astype(bf16)
        st = st_ref[h]
        o = _dot_nt((qh * jnp.exp(b)).astype(bf16), st.astype(bf16))
        rows = []
        for blk in range(C // R):
            r0, r1 = blk * R, (blk + 1) * R
            b0 = b[r0 - 1:r0, :] if blk else jnp.zeros((1, GLA_DK), f32)
            qe = qh[r0:r1] * jnp.exp(b[r0:r1] - b0)
            ke = kh[:r1] * jnp.exp(b0 - b[:r1])
            att = _dot_nt(qe.astype(bf16), ke.astype(bf16))
            keep = lax.broadcasted_iota(i32, (R, r1), 1) <= lax.broadcasted_iota(i32, (R, r1), 0) + r0
            att = jnp.where(keep, att, 0.0)
            rows.append(_dot(att.astype(bf16), vb[:r1]))
        o = o + jnp.concatenate(rows, axis=0)
        b_last = b[C - 1:C, :]
        k_dec = kh * jnp.exp(b_last - b)
        st_ref[h] = st * jnp.exp(b_last) + _dot_tn(vb, k_dec.astype(bf16))
        y = o * lax.rsqrt(jnp.mean(o * o, axis=-1, keepdims=True) + NORM_EPS) * ng_ref[...]
        rh = r_ref[0, :, vs]
        o_ref[0, :, vs] = (y * (rh * _sigmoid(rh))).astype(o_ref.dtype)


def _gla(p, w2, gate_b, norm_g, *, C=128, R=32):
    B, T, _ = p.shape
    C = min(C, T)
    W = GLA_HEADS * GLA_DV
    w2p = jnp.zeros((LANES, GLA_HEADS * GLA_DK), f32).at[:GLA_GATE_RANK].set(w2)
    w2h, w2l = _split2(w2p)

    def seg(name):
        off, w = P_OFF[name]
        return pl.BlockSpec((1, C, w), lambda b, c: (b, c, off // w))

    full = lambda a: pl.BlockSpec(a.shape, lambda b, c: (0,) * a.ndim)
    gb = gate_b.reshape(1, -1)
    ng = norm_g.reshape(1, -1)
    return pl.pallas_call(
        functools.partial(_gla_kernel, C=C, R=min(R, C)),
        grid=(B, T // C),
        in_specs=[seg('b_q'), seg('b_k'), seg('b_v'), seg('b_glr'), seg('b_r'),
                  full(w2h), full(w2l), full(gb), full(ng)],
        out_specs=pl.BlockSpec((1, C, W), lambda b, c: (b, c, 0)),
        out_shape=jax.ShapeDtypeStruct((B, T, W), bf16),
        scratch_shapes=[pltpu.VMEM((GLA_HEADS, GLA_DV, GLA_DK), f32)],
        compiler_params=_params(("parallel", "arbitrary")),
        name='gla',
    )(p, p, p, p, p, w2h, w2l, gb, ng)


def _unit_lower_inverses(mats, row, col):
    C = mats[0].shape[0]
    eye = jnp.where(row == col, 1.0, 0.0)
    blk = (row >> 3) == (col >> 3)
    n1 = [jnp.where(blk, a, 0.0) for a in mats]
    n2 = [_mm3(x, x) for x in n1]
    n4 = [_mm3(x, x) for x in n2]
    t = [_mm3(eye - x1, eye + x2) for x1, x2 in zip(n1, n2)]
    t = [_mm3(x, eye + x4) for x, x4 in zip(t, n4)]
    s = SUBLANES
    while s < C:
        sh = s.bit_length() - 1
        sel = ((row >> (sh + 1)) == (col >> (sh + 1))) & ((row >> sh) != (col >> sh))
        left = [_mm1(x, jnp.where(sel, a, 0.0)) for x, a in zip(t, mats)]
        t = [x - _mm1(y, x) for x, y in zip(t, left)]
        s *= 2
    return t


def _gdn_kernel(q_ref, k_ref, v_ref, ab_ref, gate_ref, alog_ref, dtb_ref, ng_ref, o_ref, s_ref, *, C):
    @pl.when(pl.program_id(1) == 0)
    def _():
        s_ref[...] = jnp.zeros_like(s_ref)

    H = GDN_HEADS
    row = lax.broadcasted_iota(i32, (C, C), 0)
    col = lax.broadcasted_iota(i32, (C, C), 1)
    ab = ab_ref[0]
    g_all = -jnp.exp(alog_ref[...]) * _softplus(ab + dtb_ref[...])
    beta_all = _sigmoid(ab)
    tril = (row >= col).astype(bf16)
    g1, g2, g3 = _split3(g_all)
    gam_all = _dot(tril, g1) + _dot(tril, g2) + _dot(tril, g3)
    gam_t = gam_all.T
    heads = range(H)
    sls = [slice(h * HEAD_DIM, (h + 1) * HEAD_DIM) for h in heads]
    q = [q_ref[0, :, sl] for sl in sls]
    k = [k_ref[0, :, sl] for sl in sls]
    kb = [x.astype(bf16) for x in k]
    gcol = [gam_all[:, h:h + 1] for h in heads]
    bcol = [beta_all[:, H + h:H + h + 1] for h in heads]
    egam = [jnp.exp(x) for x in gcol]
    dec = [jnp.where(row >= col, jnp.exp(jnp.minimum(gcol[h] - gam_t[h:h + 1, :], 0.0)), 0.0) for h in heads]
    kk = [_dot_nt(x, x) for x in kb]
    qk = [_dot_nt(q[h].astype(bf16), kb[h]) for h in heads]
    a = [jnp.where(row > col, bcol[h] * dec[h] * kk[h], 0.0) for h in heads]
    tinv = _unit_lower_inverses(a, row, col)
    sol = [_mm1(tinv[h], jnp.concatenate([bcol[h] * v_ref[0, :, sls[h]], (bcol[h] * egam[h]) * k[h]], axis=1))
           for h in heads]
    sb = [s_ref[h].astype(bf16) for h in heads]
    ws = [_dot(sol[h][:, HEAD_DIM:].astype(bf16), sb[h]) for h in heads]
    qs = [_dot((q[h] * egam[h]).astype(bf16), sb[h]) for h in heads]
    db = [(sol[h][:, :HEAD_DIM] - ws[h]).astype(bf16) for h in heads]
    o = [qs[h] + _dot((qk[h] * dec[h]).astype(bf16), db[h]) for h in heads]
    for h in heads:
        g_last = gcol[h][C - 1:C, :]
        k_dec = k[h] * jnp.exp(g_last - gcol[h])
        s_ref[h] = jnp.exp(g_last) * s_ref[h] + _dot_tn(k_dec.astype(bf16), db[h])
    for h in heads:
        y = o[h] * lax.rsqrt(jnp.mean(o[h] * o[h], axis=-1, keepdims=True) + NORM_EPS) * ng_ref[...]
        gt = gate_ref[0, :, sls[h]]
        o_ref[0, :, sls[h]] = (y * (gt * _sigmoid(gt))).astype(o_ref.dtype)


def _gdn(cq, ck, cv, p, a_log, dt_bias, norm_g, *, C=128):
    B, T, W = cq.shape
    C = min(C, T)
    H = GDN_HEADS
    alog = jnp.zeros((1, LANES), f32).at[0, :H].set(a_log)
    dtb = jnp.zeros((1, LANES), f32).at[0, :H].set(dt_bias)
    ng = norm_g.reshape(1, -1)
    tok = pl.BlockSpec((1, C, W), lambda b, c: (b, c, 0))

    def seg(name):
        off, w = P_OFF[name]
        return pl.BlockSpec((1, C, w), lambda b, c: (b, c, off // w))

    full = lambda a: pl.BlockSpec(a.shape, lambda b, c: (0,) * a.ndim)
    return pl.pallas_call(
        functools.partial(_gdn_kernel, C=C),
        grid=(B, T // C),
        in_specs=[tok, tok, tok, seg('c_ab'), seg('c_g'), full(alog), full(dtb), full(ng)],
        out_specs=tok,
        out_shape=jax.ShapeDtypeStruct((B, T, W), bf16),
        scratch_shapes=[pltpu.VMEM((H, HEAD_DIM, HEAD_DIM), f32)],
        compiler_params=_params(("parallel", "arbitrary")),
        name='gated_delta_rule',
    )(cq, ck, cv, p, p, alog, dtb, ng)


def _pack_kernel(w_ref, o_ref):
    for name, width in P_LAYOUT:
        dst, _ = P_OFF[name]
        src, w = (REF_OFF['c_a'][0], 2 * GDN_HEADS) if name == 'c_ab' else REF_OFF[name]
        if w == width:
            o_ref[0, :, dst:dst + width] = w_ref[0, :, src:src + w].astype(bf16)
        else:
            base = src // LANES * LANES
            win = w_ref[0, :, base:base + LANES]
            lane = lax.broadcasted_iota(i32, win.shape, 1)
            win = jnp.where((lane >= src - base) & (lane < src - base + w), win, 0.0)
            o_ref[0, :, dst:dst + width] = pltpu.roll(win, (LANES - (src - base)) % LANES, 1).astype(bf16)


def _pack_w_in(w_in, *, tr=256):
    L, D, d_in = w_in.shape
    tr = min(tr, D)
    for name, width in P_LAYOUT:
        src, w = (REF_OFF['c_a'][0], 2 * GDN_HEADS) if name == 'c_ab' else REF_OFF[name]
        assert w == width or (width == LANES and src // LANES == (src + w - 1) // LANES)
    return pl.pallas_call(
        _pack_kernel,
        grid=(L, D // tr),
        in_specs=[pl.BlockSpec((1, tr, d_in), lambda l, i: (l, i, 0))],
        out_specs=pl.BlockSpec((1, tr, P_WIDTH), lambda l, i: (l, i, 0)),
        out_shape=jax.ShapeDtypeStruct((L, D, P_WIDTH), bf16),
        compiler_params=_params(("parallel", "parallel")),
        name='pack_w_in',
    )(w_in)


def _token_mixers(hb, B, T, layer, w_in, gla_gate_w2, gla_gate_b, gla_norm_g, gdn_conv_w, gdn_a_log, gdn_dt_bias,
                  gdn_norm_g, cosf, sinf):
    p = _matmul(hb, w_in, layer, tm=1024, tn=512, tk=hb.shape[1], out_dtype=f32, name='in_proj')
    p = p.reshape(B, T, P_WIDTH)
    ops = _prep(p, cosf, sinf, gdn_conv_w)
    y_a = _dsa(ops)
    y_b = _gla(p, gla_gate_w2, gla_gate_b, gla_norm_g)
    y_c = _gdn(ops['cq'], ops['ck'], ops['cv'], p, gdn_a_log, gdn_dt_bias, gdn_norm_g)
    y_d = _moba(ops)
    return tuple(y.reshape(B * T, -1) for y in (y_a, y_b, y_c, y_d))


def _rope_tables(T):
    inv = ROPE_THETA ** (-jnp.arange(0, HEAD_DIM, 2, dtype=f32) / HEAD_DIM)
    ang = jnp.arange(T, dtype=f32)[:, None] * inv[None, :]
    cos, sin = jnp.cos(ang), jnp.sin(ang)
    return jnp.concatenate([cos, cos], axis=1), jnp.concatenate([-sin, sin], axis=1)


def kernel(x, ln_in_g, ln_in_b, w_in, w_out, ln1_g, ln1_b, gla_gate_w2, gla_gate_b, gla_norm_g, gdn_conv_w,
           gdn_a_log, gdn_dt_bias, gdn_norm_g, w_up, w_down, ln2_g, ln2_b):
    B, T, D = x.shape
    depth = w_in.shape[0]
    alpha = (2 * depth) ** 0.25
    cosf, sinf = _rope_tables(T)
    h, hb = _layer_norm(x.reshape(B * T, D), None, ln_in_g, ln_in_b)
    w_in_p = _pack_w_in(w_in)
    w_out_b, w_up_b, w_down_b = w_out.astype(bf16), w_up.astype(bf16), w_down.astype(bf16)
    for l in range(depth):
        mix = _token_mixers(hb, B, T, l, w_in_p, gla_gate_w2[l], gla_gate_b[l], gla_norm_g[l], gdn_conv_w[l],
                            gdn_a_log[l], gdn_dt_bias[l], gdn_norm_g[l], cosf, sinf)
        y = _matmul(mix, w_out_b, l, tm=1024, tn=512, tk=w_out.shape[1], out_dtype=f32, name='out_proj')
        h, hb = _layer_norm(h, y, ln1_g[l], ln1_b[l], alpha=alpha)
        up = _matmul(hb, w_up_b, l, tm=1024, tn=512, tk=D, out_dtype=bf16, act='relu2', name='mlp_up')
        ff = _matmul(up, w_down_b, l, tm=1024, tn=1024, tk=2048, out_dtype=f32, name='mlp_down')
        h, hb = _layer_norm(h, ff, ln2_g[l], ln2_b[l], alpha=alpha)
    return h.reshape(B, T, D)
```

```python
import functools
import math

import jax
import jax.numpy as jnp
from jax import lax
from jax.experimental import pallas as pl
from jax.experimental.pallas import tpu as pltpu

f32 = jnp.float32
bf16 = jnp.bfloat16
i32 = jnp.int32

HEAD_DIM = 128
A_HEADS = 8
A_KV_HEADS = 2
IDX_HEADS = 4
IDX_DIM = 128
DSA_TOPK = 256
GLA_HEADS = 4
GLA_DK = 128
GLA_DV = 256
GLA_GATE_RANK = 16
GLA_TAU = 16.0
GDN_HEADS = 8
CONV_K = 4
MOBA_HEADS = 8
MOBA_BLOCK = 256
MOBA_TOPK = 3
ROPE_THETA = 10000.0
LN_EPS = 1e-5
NORM_EPS = 1e-6

V7X_VMEM_BYTES = 64 * 2**20
VMEM_LIMIT = V7X_VMEM_BYTES * 3 // 4
LANES = 128
SUBLANES = 8

INT_MIN = -2**31
NEG_BIG = -1e30
LOG2E = math.log2(math.e)
ATTN_Q_SCALE = HEAD_DIM ** -0.5 * LOG2E

TOK_TILE = MOBA_BLOCK

P_LAYOUT = (
    ('a_q', 1024), ('b_v', 1024), ('b_r', 1024), ('c_q', 1024), ('c_k', 1024), ('c_v', 1024), ('c_g', 1024),
    ('d_q', 1024), ('d_k', 1024), ('d_v', 1024),
    ('a_iq', 512), ('b_q', 512), ('b_k', 512),
    ('a_k', 256), ('a_v', 256),
    ('a_ik', 128), ('a_iw', 128), ('b_glr', 128), ('c_ab', 128),
)
P_OFF = {}
_o = 0
for _n, _w in P_LAYOUT:
    P_OFF[_n] = (_o, _w)
    _o += _w
P_WIDTH = _o

REF_SPLITS = (
    ('a_q', 1024), ('a_k', 256), ('a_v', 256), ('a_iq', 512), ('a_ik', 128), ('a_iw', 4),
    ('b_q', 512), ('b_k', 512), ('b_v', 1024), ('b_glr', 16), ('b_r', 1024),
    ('c_q', 1024), ('c_k', 1024), ('c_v', 1024), ('c_a', 8), ('c_b', 8), ('c_g', 1024),
    ('d_q', 1024), ('d_k', 1024), ('d_v', 1024),
)
REF_OFF = {}
_o = 0
for _n, _w in REF_SPLITS:
    REF_OFF[_n] = (_o, _w)
    _o += _w


def _dot(a, b):
    return jnp.dot(a, b, preferred_element_type=f32)


def _dot_nt(a, b):
    return lax.dot_general(a, b, (((1,), (1,)), ((), ())), preferred_element_type=f32)


def _dot_tn(a, b):
    return lax.dot_general(a, b, (((0,), (0,)), ((), ())), preferred_element_type=f32)


def _split2(x):
    hi = x.astype(bf16)
    lo = (x - hi.astype(f32)).astype(bf16)
    return hi, lo


def _split3(x):
    hi = x.astype(bf16)
    r = x - hi.astype(f32)
    mid = r.astype(bf16)
    lo = (r - mid.astype(f32)).astype(bf16)
    return hi, mid, lo


def _mm1(x, y):
    return _dot(x.astype(bf16), y.astype(bf16))


def _mm3(x, y):
    xh, xl = _split2(x)
    yh, yl = _split2(y)
    return _dot(xh, yh) + _dot(xh, yl) + _dot(xl, yh)


def _sigmoid(x):
    return 1.0 / (1.0 + jnp.exp(-x))


def _softplus(x):
    return jnp.maximum(x, 0.0) + jnp.log1p(jnp.exp(-jnp.abs(x)))


def _params(sem):
    return pltpu.CompilerParams(dimension_semantics=sem, vmem_limit_bytes=VMEM_LIMIT)


def _mm_kernel(*refs, n_a, nk, act):
    a_refs, b_ref, o_ref = refs[:n_a], refs[n_a], refs[n_a + 1]

    def product():
        if n_a == 1:
            return _dot(a_refs[0][...], b_ref[...])
        kw = b_ref.shape[0] // n_a
        out = _dot(a_refs[0][...], b_ref[0:kw, :])
        for g in range(1, n_a):
            out = out + _dot(a_refs[g][...], b_ref[g * kw:(g + 1) * kw, :])
        return out

    def finish(r):
        if act == 'relu2':
            r = jnp.square(jnp.maximum(r, 0.0))
        o_ref[...] = r.astype(o_ref.dtype)

    if nk == 1:
        finish(product())
        return
    acc_ref = refs[n_a + 2]
    k = pl.program_id(2)

    @pl.when(k == 0)
    def _():
        acc_ref[...] = jnp.zeros_like(acc_ref)

    acc_ref[...] += product()

    @pl.when(k == nk - 1)
    def _():
        finish(acc_ref[...])


def _matmul(a, b, layer, *, tm, tn, tk, out_dtype, act=None, name='matmul'):
    a_list = a if isinstance(a, (tuple, list)) else (a,)
    n_a = len(a_list)
    M = a_list[0].shape[0]
    _, K, N = b.shape
    tm, tn, tk = min(tm, M), min(tn, N), min(tk, K)
    assert M % tm == 0 and N % tn == 0 and K % tk == 0
    nk = K // tk
    assert n_a == 1 or nk == 1
    ka = tk // n_a
    return pl.pallas_call(
        functools.partial(_mm_kernel, n_a=n_a, nk=nk, act=act),
        grid=(M // tm, N // tn, nk),
        in_specs=[pl.BlockSpec((tm, ka), lambda i, j, k: (i, k))] * n_a
                 + [pl.BlockSpec((None, tk, tn), lambda i, j, k: (layer, k, j))],
        out_specs=pl.BlockSpec((tm, tn), lambda i, j, k: (i, j)),
        out_shape=jax.ShapeDtypeStruct((M, N), out_dtype),
        scratch_shapes=[pltpu.VMEM((tm, tn), f32)] if nk > 1 else [],
        compiler_params=_params(("parallel", "parallel", "arbitrary")),
        name=name,
    )(*a_list, b)


def _ln_kernel(*refs, alpha, has_y):
    if has_y:
        x_ref, y_ref, g_ref, b_ref, o_ref, ob_ref = refs
        z = x_ref[...] * alpha + y_ref[...]
    else:
        x_ref, g_ref, b_ref, o_ref, ob_ref = refs
        z = x_ref[...]
    mu = jnp.mean(z, axis=-1, keepdims=True)
    zc = z - mu
    var = jnp.mean(zc * zc, axis=-1, keepdims=True)
    out = zc * lax.rsqrt(var + LN_EPS) * g_ref[...] + b_ref[...]
    o_ref[...] = out
    ob_ref[...] = out.astype(bf16)


def _layer_norm(x, y, g, b, *, alpha=1.0, tm=256):
    M, D = x.shape
    tm = min(tm, M)
    row = pl.BlockSpec((tm, D), lambda i: (i, 0))
    vec = pl.BlockSpec((1, D), lambda i: (0, 0))
    has_y = y is not None
    args = (x, y) if has_y else (x,)
    return pl.pallas_call(
        functools.partial(_ln_kernel, alpha=alpha, has_y=has_y),
        grid=(M // tm,),
        in_specs=[row] * len(args) + [vec, vec],
        out_specs=[row, row],
        out_shape=[jax.ShapeDtypeStruct((M, D), f32), jax.ShapeDtypeStruct((M, D), bf16)],
        compiler_params=_params(("parallel",)),
        name='layer_norm',
    )(*args, g.reshape(1, D), b.reshape(1, D))


def _prep_kernel(aq_ref, ak_ref, av_ref, aiq_ref, aik_ref, aiw_ref,
                 cq_ref, ck_ref, cv_ref, cqh_ref, ckh_ref, cvh_ref,
                 dq_ref, dk_ref, dv_ref, cos_ref, sin_ref, cw_ref,
                 oaq, oak, oavt, oaiq, oaik, oaiwt, ocq, ock, ocv,
                 odq, odqh, odql, odk, odvt, odkmh, odkml, *, TT):
    cosf, sinf = cos_ref[...], sin_ref[...]

    def rope(x):
        return x * cosf + pltpu.roll(x, HEAD_DIM // 2, 1) * sinf

    def head(h):
        return slice(h * HEAD_DIM, (h + 1) * HEAD_DIM)

    for h in range(A_HEADS):
        oaq[0, :, head(h)] = (rope(aq_ref[0, :, head(h)]) * ATTN_Q_SCALE).astype(bf16)
    for g in range(A_KV_HEADS):
        oak[0, :, head(g)] = rope(ak_ref[0, :, head(g)]).astype(bf16)
        oavt[0, 0, head(g), :] = av_ref[0, :, head(g)].T.astype(bf16)
    for h in range(IDX_HEADS):
        oaiq[0, :, head(h)] = rope(aiq_ref[0, :, head(h)]).astype(bf16)
    oaik[0] = rope(aik_ref[0]).astype(bf16)
    oaiwt[0] = aiw_ref[0].T[:SUBLANES, :] * (IDX_HEADS ** -0.5 * IDX_DIM ** -0.5)

    first = pl.program_id(1) == 0
    for j, (x_ref, h_ref, o_ref) in enumerate(((cq_ref, cqh_ref, ocq), (ck_ref, ckh_ref, ock),
                                                (cv_ref, cvh_ref, ocv))):
        width = x_ref.shape[2]
        halo = jnp.where(first, 0.0, h_ref[0])
        cat = jnp.concatenate([halo, x_ref[0]], axis=0)
        w = cw_ref[:, j * width:(j + 1) * width]
        y = cat[SUBLANES - CONV_K + 1:SUBLANES - CONV_K + 1 + TT] * w[0:1]
        for i in range(1, CONV_K):
            s0 = SUBLANES - CONV_K + 1 + i
            y = y + cat[s0:s0 + TT] * w[i:i + 1]
        y = y * _sigmoid(y)
        if j == 2:
            o_ref[0] = y
        else:
            post = HEAD_DIM ** -0.5 if j == 0 else 1.0
            for h in range(GDN_HEADS):
                yh = y[:, head(h)]
                o_ref[0, :, head(h)] = yh * (lax.rsqrt(jnp.sum(yh * yh, axis=-1, keepdims=True) + NORM_EPS) * post)

    for h in range(MOBA_HEADS):
        qr = rope(dq_ref[0, :, head(h)])
        odq[0, :, head(h)] = (qr * ATTN_Q_SCALE).astype(bf16)
        qh, ql = _split2(qr)
        odqh[0, :, head(h)] = qh
        odql[0, :, head(h)] = ql
        kr = rope(dk_ref[0, :, head(h)])
        odk[0, :, head(h)] = kr.astype(bf16)
        kmh, kml = _split2(jnp.mean(kr, axis=0, keepdims=True))
        odkmh[0, 0, :, head(h)] = kmh
        odkml[0, 0, :, head(h)] = kml
        odvt[0, 0, head(h), :] = dv_ref[0, :, head(h)].T.astype(bf16)


def _prep(p, cosf, sinf, conv_w):
    B, T, _ = p.shape
    TT = min(TOK_TILE, T)
    NT = T // TT

    def seg(name, rows=TT):
        off, w = P_OFF[name]
        return pl.BlockSpec((1, rows, w), lambda b, i: (b, i, off // w))

    def halo(name):
        off, w = P_OFF[name]
        return pl.BlockSpec((1, SUBLANES, w),
                            lambda b, i: (b, jnp.maximum(i * (TT // SUBLANES) - 1, 0), off // w))

    in_names = ('a_q', 'a_k', 'a_v', 'a_iq', 'a_ik', 'a_iw', 'c_q', 'c_k', 'c_v')
    in_specs = ([seg(n) for n in in_names] + [halo(n) for n in ('c_q', 'c_k', 'c_v')]
                + [seg(n) for n in ('d_q', 'd_k', 'd_v')]
                + [pl.BlockSpec((TT, HEAD_DIM), lambda b, i: (i, 0))] * 2
                + [pl.BlockSpec(conv_w.shape, lambda b, i: (0, 0))])

    def tok(w, dt):
        return pl.BlockSpec((1, TT, w), lambda b, i: (b, i, 0)), jax.ShapeDtypeStruct((B, T, w), dt)

    def tposed(rows, dt):
        return (pl.BlockSpec((1, 1, rows, TT), lambda b, i: (b, i, 0, 0)),
                jax.ShapeDtypeStruct((B, NT, rows, TT), dt))

    def per_tile(w, dt):
        return (pl.BlockSpec((1, 1, 1, w), lambda b, i: (b, i, 0, 0)),
                jax.ShapeDtypeStruct((B, NT, 1, w), dt))

    outs = [
        tok(1024, bf16), tok(256, bf16), tposed(256, bf16), tok(512, bf16), tok(128, bf16),
        (pl.BlockSpec((1, SUBLANES, TT), lambda b, i: (b, 0, i)), jax.ShapeDtypeStruct((B, SUBLANES, T), f32)),
        tok(1024, f32), tok(1024, f32), tok(1024, f32),
        tok(1024, bf16), tok(1024, bf16), tok(1024, bf16), tok(1024, bf16), tposed(1024, bf16),
        per_tile(1024, bf16), per_tile(1024, bf16),
    ]
    res = pl.pallas_call(
        functools.partial(_prep_kernel, TT=TT),
        grid=(B, NT),
        in_specs=in_specs,
        out_specs=[o[0] for o in outs],
        out_shape=[o[1] for o in outs],
        compiler_params=_params(("parallel", "arbitrary")),
        name='mixer_prep',
    )(*([p] * 15), cosf, sinf, conv_w)
    names = ('aq', 'ak', 'avt', 'aiq', 'aik', 'aiwt', 'cq', 'ck', 'cv', 'dq', 'dqh', 'dql', 'dk', 'dvt',
             'dkmh', 'dkml')
    return dict(zip(names, res))


def _tiles_by_pairs(n, produce, consume, state):
    odd = n % 2

    def single(j, st):
        return consume(j, produce(j), st)

    def pair(m, st):
        j = odd + 2 * m
        first, second = produce(j), produce(j + 1)
        return consume(j + 1, second, consume(j, first, st))

    state = lax.fori_loop(0, odd, single, state)
    return lax.fori_loop(0, n // 2, pair, state)


def _flash_update(p, m_prev, m_new, l_prev, acc_prev, vt, ones):
    pb = p.astype(bf16)
    a = jnp.exp2(m_prev - m_new)
    l_new = a * l_prev + _dot(ones, pb)[0:1]
    acc_new = a * acc_prev + _dot(vt, pb)
    return m_new, l_new, acc_new


def _flash_step(s, mask, m_prev, l_prev, acc_prev, vt, ones):
    s = jnp.where(mask, s, NEG_BIG)
    m_new = jnp.maximum(m_prev, jnp.max(s, axis=0, keepdims=True))
    return _flash_update(jnp.exp2(s - m_new), m_prev, m_new, l_prev, acc_prev, vt, ones)


def _flash_step_cols(s, cols, m_prev, l_prev, acc_prev, vt, ones):
    m_new = jnp.where(cols, jnp.maximum(m_prev, jnp.max(s, axis=0, keepdims=True)), m_prev)
    p = jnp.exp2(s - jnp.where(cols, m_new, -NEG_BIG))
    return _flash_update(p, m_prev, m_new, l_prev, acc_prev, vt, ones)


def _dsa_kernel(iq_ref, iw_ref, q_ref, ik_ref, k_ref, vt_ref, o_ref, keys_ref, acc_ref, *, TT, topk):
    i = pl.program_id(1)
    n_kt = i + 1
    t_idx = i * TT + lax.broadcasted_iota(i32, (1, TT), 1)
    row = lax.broadcasted_iota(i32, (TT, 1), 0)
    iw = iw_ref[0]
    iq = iq_ref[0]

    def tile(kt):
        return pl.ds(pl.multiple_of(kt * TT, TT), TT)

    def head_scores(kt):
        ik = ik_ref[0, tile(kt), :]
        return [_dot_nt(ik, iq[:, h * IDX_DIM:(h + 1) * IDX_DIM]) for h in range(IDX_HEADS)]

    def store_keys(kt, xs, carry):
        acc = jnp.zeros((TT, TT), f32)
        for h in range(IDX_HEADS):
            acc = acc + iw[h:h + 1, :] * jnp.maximum(xs[h], 0.0)
        bits = lax.bitcast_convert_type(acc, i32)
        key = bits ^ ((bits >> 31) & jnp.int32(0x7FFFFFFF))
        key = jnp.where(key == -1, 0, key)
        key = jnp.where(kt * TT + row <= t_idx, key, jnp.int32(INT_MIN))
        keys_ref[tile(kt), :] = key
        return carry

    _tiles_by_pairs(n_kt, head_scores, store_keys, 0)

    def count(pred_fn):
        def body(kt, c):
            ind = pred_fn(keys_ref[tile(kt), :]).astype(i32)
            return c + jnp.sum(ind.reshape(TT // SUBLANES, SUBLANES, TT), axis=0)
        c = lax.fori_loop(0, n_kt, body, jnp.zeros((SUBLANES, TT), i32))
        return jnp.sum(c, axis=0, keepdims=True)

    n_nonneg = count(lambda key: key >= 0)
    nonneg = n_nonneg >= topk
    settled = (t_idx + 1 <= topk) | (nonneg & (count(lambda key: key > 0) < topk))

    def pending(cnt_cur):
        return jnp.max(jnp.where((cnt_cur == topk) | settled, 0, 1))

    def bit_pass(state):
        bi, cur, cnt_cur, _ = state
        cand = cur + lax.shift_left(jnp.int32(1), 31 - bi)
        cnt = count(lambda key: key >= cand)
        ok = cnt >= topk
        cur = jnp.where(ok, cand, cur)
        cnt_cur = jnp.where(ok, cnt, cnt_cur)
        return bi + 1, cur, cnt_cur, pending(cnt_cur)

    cnt0 = jnp.where(nonneg, n_nonneg, n_kt * TT)
    state = (jnp.int32(1), jnp.where(nonneg, 0, jnp.int32(INT_MIN)), cnt0, pending(cnt0))
    _, thr, _, _ = lax.while_loop(lambda st: (st[0] < 32) & (st[3] > 0), bit_pass, state)
    need = (topk - count(lambda key: key > thr)).astype(f32)

    tri = (lax.broadcasted_iota(i32, (TT, TT), 0) > lax.broadcasted_iota(i32, (TT, TT), 1)).astype(bf16)
    ones = jnp.ones((2 * SUBLANES, TT), bf16)
    group = A_HEADS // A_KV_HEADS
    acc_ref[...] = jnp.zeros_like(acc_ref)

    def attend_tile(kt, carry):
        tie_carry, stats = carry
        key = keys_ref[tile(kt), :]
        eq = (key == thr) & (key != INT_MIN)
        eq_f = jnp.where(eq, 1.0, 0.0)
        tie_rank = _dot(tri, eq_f.astype(bf16)) + tie_carry
        mask = (key > thr) | (eq & (tie_rank < need))
        ktile = k_ref[0, tile(kt), :]
        logits = [_dot_nt(ktile[:, (h // group) * HEAD_DIM:(h // group + 1) * HEAD_DIM],
                          q_ref[0, :, h * HEAD_DIM:(h + 1) * HEAD_DIM]) for h in range(A_HEADS)]
        new_stats = []
        for h in range(A_HEADS):
            g = h // group
            m_new, l_new, acc_new = _flash_step(logits[h], mask, *stats[h], acc_ref[h],
                                                vt_ref[0, kt, g * HEAD_DIM:(g + 1) * HEAD_DIM, :], ones)
            acc_ref[h] = acc_new
            new_stats.append((m_new, l_new))
        return tie_carry + jnp.sum(eq_f, axis=0, keepdims=True), tuple(new_stats)

    init = (jnp.zeros((1, TT), f32),
            tuple((jnp.full((1, TT), NEG_BIG, f32), jnp.zeros((1, TT), f32)) for _ in range(A_HEADS)))
    _, stats = lax.fori_loop(0, n_kt, attend_tile, init)
    for h in range(A_HEADS):
        o_ref[0, :, h * HEAD_DIM:(h + 1) * HEAD_DIM] = (acc_ref[h] / stats[h][1]).T.astype(o_ref.dtype)


def _dsa(ops):
    B, T, W = ops['aq'].shape
    TT = min(TOK_TILE, T)
    NT = T // TT
    topk = min(DSA_TOPK, T // 4)
    return pl.pallas_call(
        functools.partial(_dsa_kernel, TT=TT, topk=topk),
        grid=(B, NT),
        in_specs=[
            pl.BlockSpec((1, TT, IDX_HEADS * IDX_DIM), lambda b, i: (b, i, 0)),
            pl.BlockSpec((1, SUBLANES, TT), lambda b, i: (b, 0, i)),
            pl.BlockSpec((1, TT, W), lambda b, i: (b, i, 0)),
            pl.BlockSpec((1, T, IDX_DIM), lambda b, i: (b, 0, 0)),
            pl.BlockSpec((1, T, A_KV_HEADS * HEAD_DIM), lambda b, i: (b, 0, 0)),
            pl.BlockSpec((1, NT, A_KV_HEADS * HEAD_DIM, TT), lambda b, i: (b, 0, 0, 0)),
        ],
        out_specs=pl.BlockSpec((1, TT, W), lambda b, i: (b, i, 0)),
        out_shape=jax.ShapeDtypeStruct((B, T, W), bf16),
        scratch_shapes=[pltpu.VMEM((T, TT), i32), pltpu.VMEM((A_HEADS, HEAD_DIM, TT), f32)],
        compiler_params=_params(("parallel", "arbitrary")),
        name='dsa_attention',
    )(ops['aiq'], ops['aiwt'], ops['aq'], ops['aik'], ops['ak'], ops['avt'])


def _moba_kernel(q_ref, qh_ref, ql_ref, kmh_ref, kml_ref, k_ref, vt_ref, o_ref, sel_ref, acc_ref,
                 *, NB, BS, n_sel, HP):
    i = pl.program_id(2)
    n_idx = lax.broadcasted_iota(i32, (NB, 1), 0)
    past = n_idx < i

    def head(h):
        return slice(h * HEAD_DIM, (h + 1) * HEAD_DIM)

    for h in range(HP):
        kmh, kml = kmh_ref[0, :, head(h)], kml_ref[0, :, head(h)]
        qh, ql = qh_ref[0, :, head(h)], ql_ref[0, :, head(h)]
        gate = _dot_nt(kmh, qh) + _dot_nt(kmh, ql) + _dot_nt(kml, qh)
        g = jnp.where(past, gate, -jnp.inf)
        sel = jnp.zeros((NB, BS), f32)
        for _ in range(n_sel):
            m = jnp.max(g, axis=0, keepdims=True)
            first = jnp.min(jnp.where(g == m, n_idx, NB), axis=0, keepdims=True)
            pick = (n_idx == first) & (m > -jnp.inf)
            sel = jnp.where(pick, 1.0, sel)
            g = jnp.where(pick, -jnp.inf, g)
        sel_ref[h] = sel

    acc_ref[...] = jnp.zeros_like(acc_ref)
    ones = jnp.ones((2 * SUBLANES, BS), bf16)

    def logits_of(n):
        ktile = k_ref[0, pl.ds(pl.multiple_of(n * BS, BS), BS), :]
        return tuple(_dot_nt(ktile[:, head(h)], q_ref[0, :, head(h)]) for h in range(HP))

    def block(n, logits, stats, step_fn):
        new_stats = []
        for h in range(HP):
            m_new, l_new, acc_new = step_fn(h, logits[h], *stats[h], acc_ref[h], vt_ref[0, n, head(h), :], ones)
            acc_ref[h] = acc_new
            new_stats.append((m_new, l_new))
        return tuple(new_stats)

    def past_block(n, logits, stats):
        return block(n, logits, stats,
                     lambda h, s, *rest: _flash_step_cols(s, sel_ref[h, pl.ds(n, 1), :] > 0.5, *rest))

    init = tuple((jnp.full((1, BS), NEG_BIG, f32), jnp.zeros((1, BS), f32)) for _ in range(HP))
    stats = _tiles_by_pairs(i, logits_of, past_block, init)
    causal = lax.broadcasted_iota(i32, (BS, BS), 0) <= lax.broadcasted_iota(i32, (BS, BS), 1)
    stats = block(i, logits_of(i), stats, lambda h, s, *rest: _flash_step(s, causal, *rest))
    for h in range(HP):
        o_ref[0, :, head(h)] = (acc_ref[h] / stats[h][1]).T.astype(o_ref.dtype)


def _moba(ops, *, HP=4):
    B, T, W = ops['dq'].shape
    BS = MOBA_BLOCK
    assert T % BS == 0 and TOK_TILE == BS
    NB = T // BS
    n_sel = min(MOBA_TOPK, NB - 1)
    WP = HP * HEAD_DIM
    kmh = ops['dkmh'].reshape(B, NB, W)
    kml = ops['dkml'].reshape(B, NB, W)
    qspec = pl.BlockSpec((1, BS, WP), lambda b, hg, i: (b, i, hg))
    kmspec = pl.BlockSpec((1, NB, WP), lambda b, hg, i: (b, 0, hg))
    return pl.pallas_call(
        functools.partial(_moba_kernel, NB=NB, BS=BS, n_sel=n_sel, HP=HP),
        grid=(B, W // WP, NB),
        in_specs=[qspec, qspec, qspec, kmspec, kmspec,
                  pl.BlockSpec((1, T, WP), lambda b, hg, i: (b, 0, hg)),
                  pl.BlockSpec((1, NB, WP, BS), lambda b, hg, i: (b, 0, hg, 0))],
        out_specs=qspec,
        out_shape=jax.ShapeDtypeStruct((B, T, W), bf16),
        scratch_shapes=[pltpu.VMEM((HP, NB, BS), f32), pltpu.VMEM((HP, HEAD_DIM, BS), f32)],
        compiler_params=_params(("parallel", "parallel", "arbitrary")),
        name='moba_attention',
    )(ops['dq'], ops['dqh'], ops['dql'], kmh, kml, ops['dk'], ops['dvt'])


def _gla_kernel(q_ref, k_ref, v_ref, glr_ref, r_ref, w2h_ref, w2l_ref, gb_ref, ng_ref, o_ref, st_ref, *, C, R):
    @pl.when(pl.program_id(1) == 0)
    def _():
        st_ref[...] = jnp.zeros_like(st_ref)

    gh, gl = _split2(glr_ref[0])
    logit = _dot(gh, w2h_ref[...]) + _dot(gh, w2l_ref[...]) + _dot(gl, w2h_ref[...]) + gb_ref[...]
    log_a = -_softplus(-logit) * (1.0 / GLA_TAU)
    tril = (lax.broadcasted_iota(i32, (C, C), 0) >= lax.broadcasted_iota(i32, (C, C), 1)).astype(bf16)
    a1, a2, a3 = _split3(log_a)
    b_all = _dot(tril, a1) + _dot(tril, a2) + _dot(tril, a3)
    heads = range(GLA_HEADS)
    kss = [slice(h * GLA_DK, (h + 1) * GLA_DK) for h in heads]
    vss = [slice(h * GLA_DV, (h + 1) * GLA_DV) for h in heads]
    b = [b_all[:, ks] for ks in kss]
    q = [q_ref[0, :, ks] * GLA_DK ** -0.5 for ks in kss]
    k = [k_ref[0, :, ks] for ks in kss]
    vb = [v_ref[0, :, vs].astype(bf16) for vs in vss]
    st = [st_ref[h] for h in heads]
    inter = [_dot_nt((q[h] * jnp.exp(b[h])).astype(bf16), st[h].astype(bf16)) for h in heads]
    att = [[None] * (C // R) for _ in heads]
    for blk in range(C // R):
        r0, r1 = blk * R, (blk + 1) * R
        keep = lax.broadcasted_iota(i32, (R, r1), 1) <= lax.broadcasted_iota(i32, (R, r1), 0) + r0
        for h in heads:
            b0 = b[h][r0 - 1:r0, :] if blk else jnp.zeros((1, GLA_DK), f32)
            qe = q[h][r0:r1] * jnp.exp(b[h][r0:r1] - b0)
            ke = k[h][:r1] * jnp.exp(b0 - b[h][:r1])
            att[h][blk] = jnp.where(keep, _dot_nt(qe.astype(bf16), ke.astype(bf16)), 0.0).astype(bf16)
    intra = [jnp.concatenate([_dot(att[h][blk], vb[h][:(blk + 1) * R]) for blk in range(C // R)], axis=0)
             for h in heads]
    for h in heads:
        b_last = b[h][C - 1:C, :]
        k_dec = k[h] * jnp.exp(b_last - b[h])
        st_ref[h] = st[h] * jnp.exp(b_last) + _dot_tn(vb[h], k_dec.astype(bf16))
    for h in heads:
        o = inter[h] + intra[h]
        y = o * lax.rsqrt(jnp.mean(o * o, axis=-1, keepdims=True) + NORM_EPS) * ng_ref[...]
        rh = r_ref[0, :, vss[h]]
        o_ref[0, :, vss[h]] = (y * (rh * _sigmoid(rh))).astype(o_ref.dtype)


def _gla(p, w2, gate_b, norm_g, *, C=128, R=32):
    B, T, _ = p.shape
    C = min(C, T)
    W = GLA_HEADS * GLA_DV
    w2p = jnp.zeros((LANES, GLA_HEADS * GLA_DK), f32).at[:GLA_GATE_RANK].set(w2)
    w2h, w2l = _split2(w2p)

    def seg(name):
        off, w = P_OFF[name]
        return pl.BlockSpec((1, C, w), lambda b, c: (b, c, off // w))

    full = lambda a: pl.BlockSpec(a.shape, lambda b, c: (0,) * a.ndim)
    gb = gate_b.reshape(1, -1)
    ng = norm_g.reshape(1, -1)
    return pl.pallas_call(
        functools.partial(_gla_kernel, C=C, R=min(R, C)),
        grid=(B, T // C),
        in_specs=[seg('b_q'), seg('b_k'), seg('b_v'), seg('b_glr'), seg('b_r'),
                  full(w2h), full(w2l), full(gb), full(ng)],
        out_specs=pl.BlockSpec((1, C, W), lambda b, c: (b, c, 0)),
        out_shape=jax.ShapeDtypeStruct((B, T, W), bf16),
        scratch_shapes=[pltpu.VMEM((GLA_HEADS, GLA_DV, GLA_DK), f32)],
        compiler_params=_params(("parallel", "arbitrary")),
        name='gla',
    )(p, p, p, p, p, w2h, w2l, gb, ng)


def _unit_lower_inverses(mats, row, col):
    C = mats[0].shape[0]
    eye = jnp.where(row == col, 1.0, 0.0)
    blk = (row >> 3) == (col >> 3)
    n1 = [jnp.where(blk, a, 0.0) for a in mats]
    n2 = [_mm3(x, x) for x in n1]
    n4 = [_mm3(x, x) for x in n2]
    t = [_mm3(eye - x1, eye + x2) for x1, x2 in zip(n1, n2)]
    t = [_mm3(x, eye + x4) for x, x4 in zip(t, n4)]
    s = SUBLANES
    while s < C:
        sh = s.bit_length() - 1
        sel = ((row >> (sh + 1)) == (col >> (sh + 1))) & ((row >> sh) != (col >> sh))
        left = [_mm1(x, jnp.where(sel, a, 0.0)) for x, a in zip(t, mats)]
        t = [x - _mm1(y, x) for x, y in zip(t, left)]
        s *= 2
    return t


def _gdn_kernel(q_ref, k_ref, v_ref, ab_ref, gate_ref, alog_ref, dtb_ref, ng_ref, o_ref, s_ref, *, C):
    @pl.when(pl.program_id(1) == 0)
    def _():
        s_ref[...] = jnp.zeros_like(s_ref)

    H = GDN_HEADS
    row = lax.broadcasted_iota(i32, (C, C), 0)
    col = lax.broadcasted_iota(i32, (C, C), 1)
    ab = ab_ref[0]
    g_all = -jnp.exp(alog_ref[...]) * _softplus(ab + dtb_ref[...])
    beta_all = _sigmoid(ab)
    tril = (row >= col).astype(bf16)
    g1, g2, g3 = _split3(g_all)
    gam_all = _dot(tril, g1) + _dot(tril, g2) + _dot(tril, g3)
    gam_t = gam_all.T
    heads = range(H)
    sls = [slice(h * HEAD_DIM, (h + 1) * HEAD_DIM) for h in heads]
    q = [q_ref[0, :, sl] for sl in sls]
    k = [k_ref[0, :, sl] for sl in sls]
    kb = [x.astype(bf16) for x in k]
    gcol = [gam_all[:, h:h + 1] for h in heads]
    bcol = [beta_all[:, H + h:H + h + 1] for h in heads]
    egam = [jnp.exp(x) for x in gcol]
    dec = [jnp.where(row >= col, jnp.exp(jnp.minimum(gcol[h] - gam_t[h:h + 1, :], 0.0)), 0.0) for h in heads]
    kk = [_dot_nt(x, x) for x in kb]
    qk = [_dot_nt(q[h].astype(bf16), kb[h]) for h in heads]
    a = [jnp.where(row > col, bcol[h] * dec[h] * kk[h], 0.0) for h in heads]
    tinv = _unit_lower_inverses(a, row, col)
    sol = [_mm1(tinv[h], jnp.concatenate([bcol[h] * v_ref[0, :, sls[h]], (bcol[h] * egam[h]) * k[h]], axis=1))
           for h in heads]
    sb = [s_ref[h].astype(bf16) for h in heads]
    ws = [_dot(sol[h][:, HEAD_DIM:].astype(bf16), sb[h]) for h in heads]
    qs = [_dot((q[h] * egam[h]).astype(bf16), sb[h]) for h in heads]
    db = [(sol[h][:, :HEAD_DIM] - ws[h]).astype(bf16) for h in heads]
    o = [qs[h] + _dot((qk[h] * dec[h]).astype(bf16), db[h]) for h in heads]
    for h in heads:
        g_last = gcol[h][C - 1:C, :]
        k_dec = k[h] * jnp.exp(g_last - gcol[h])
        s_ref[h] = jnp.exp(g_last) * s_ref[h] + _dot_tn(k_dec.astype(bf16), db[h])
    for h in heads:
        y = o[h] * lax.rsqrt(jnp.mean(o[h] * o[h], axis=-1, keepdims=True) + NORM_EPS) * ng_ref[...]
        gt = gate_ref[0, :, sls[h]]
        o_ref[0, :, sls[h]] = (y * (gt * _sigmoid(gt))).astype(o_ref.dtype)


def _gdn(cq, ck, cv, p, a_log, dt_bias, norm_g, *, C=128):
    B, T, W = cq.shape
    C = min(C, T)
    H = GDN_HEADS
    alog = jnp.zeros((1, LANES), f32).at[0, :H].set(a_log)
    dtb = jnp.zeros((1, LANES), f32).at[0, :H].set(dt_bias)
    ng = norm_g.reshape(1, -1)
    tok = pl.BlockSpec((1, C, W), lambda b, c: (b, c, 0))

    def seg(name):
        off, w = P_OFF[name]
        return pl.BlockSpec((1, C, w), lambda b, c: (b, c, off // w))

    full = lambda a: pl.BlockSpec(a.shape, lambda b, c: (0,) * a.ndim)
    return pl.pallas_call(
        functools.partial(_gdn_kernel, C=C),
        grid=(B, T // C),
        in_specs=[tok, tok, tok, seg('c_ab'), seg('c_g'), full(alog), full(dtb), full(ng)],
        out_specs=tok,
        out_shape=jax.ShapeDtypeStruct((B, T, W), bf16),
        scratch_shapes=[pltpu.VMEM((H, HEAD_DIM, HEAD_DIM), f32)],
        compiler_params=_params(("parallel", "arbitrary")),
        name='gated_delta_rule',
    )(cq, ck, cv, p, p, alog, dtb, ng)


def _pack_kernel(w_ref, o_ref):
    for name, width in P_LAYOUT:
        dst, _ = P_OFF[name]
        src, w = (REF_OFF['c_a'][0], 2 * GDN_HEADS) if name == 'c_ab' else REF_OFF[name]
        if w == width:
            o_ref[0, :, dst:dst + width] = w_ref[0, :, src:src + w].astype(bf16)
        else:
            base = src // LANES * LANES
            win = w_ref[0, :, base:base + LANES]
            lane = lax.broadcasted_iota(i32, win.shape, 1)
            win = jnp.where((lane >= src - base) & (lane < src - base + w), win, 0.0)
            o_ref[0, :, dst:dst + width] = pltpu.roll(win, (LANES - (src - base)) % LANES, 1).astype(bf16)


def _pack_w_in(w_in, *, tr=256):
    L, D, d_in = w_in.shape
    tr = min(tr, D)
    for name, width in P_LAYOUT:
        src, w = (REF_OFF['c_a'][0], 2 * GDN_HEADS) if name == 'c_ab' else REF_OFF[name]
        assert w == width or (width == LANES and src // LANES == (src + w - 1) // LANES)
    return pl.pallas_call(
        _pack_kernel,
        grid=(L, D // tr),
        in_specs=[pl.BlockSpec((1, tr, d_in), lambda l, i: (l, i, 0))],
        out_specs=pl.BlockSpec((1, tr, P_WIDTH), lambda l, i: (l, i, 0)),
        out_shape=jax.ShapeDtypeStruct((L, D, P_WIDTH), bf16),
        compiler_params=_params(("parallel", "parallel")),
        name='pack_w_in',
    )(w_in)


def _token_mixers(hb, B, T, layer, w_in, gla_gate_w2, gla_gate_b, gla_norm_g, gdn_conv_w, gdn_a_log, gdn_dt_bias,
                  gdn_norm_g, cosf, sinf):
    p = _matmul(hb, w_in, layer, tm=1024, tn=512, tk=hb.shape[1], out_dtype=f32, name='in_proj')
    p = p.reshape(B, T, P_WIDTH)
    ops = _prep(p, cosf, sinf, gdn_conv_w)
    y_a = _dsa(ops)
    y_b = _gla(p, gla_gate_w2, gla_gate_b, gla_norm_g)
    y_c = _gdn(ops['cq'], ops['ck'], ops['cv'], p, gdn_a_log, gdn_dt_bias, gdn_norm_g)
    y_d = _moba(ops)
    return tuple(y.reshape(B * T, -1) for y in (y_a, y_b, y_c, y_d))


def _rope_tables(T):
    inv = ROPE_THETA ** (-jnp.arange(0, HEAD_DIM, 2, dtype=f32) / HEAD_DIM)
    ang = jnp.arange(T, dtype=f32)[:, None] * inv[None, :]
    cos, sin = jnp.cos(ang), jnp.sin(ang)
    return jnp.concatenate([cos, cos], axis=1), jnp.concatenate([-sin, sin], axis=1)


def kernel(x, ln_in_g, ln_in_b, w_in, w_out, ln1_g, ln1_b, gla_gate_w2, gla_gate_b, gla_norm_g, gdn_conv_w,
           gdn_a_log, gdn_dt_bias, gdn_norm_g, w_up, w_down, ln2_g, ln2_b):
    B, T, D = x.shape
    depth = w_in.shape[0]
    alpha = (2 * depth) ** 0.25
    cosf, sinf = _rope_tables(T)
    h, hb = _layer_norm(x.reshape(B * T, D), None, ln_in_g, ln_in_b)
    w_in_p = _pack_w_in(w_in)
    w_out_b, w_up_b, w_down_b = w_out.astype(bf16), w_up.astype(bf16), w_down.astype(bf16)
    for l in range(depth):
        mix = _token_mixers(hb, B, T, l, w_in_p, gla_gate_w2[l], gla_gate_b[l], gla_norm_g[l], gdn_conv_w[l],
                            gdn_a_log[l], gdn_dt_bias[l], gdn_norm_g[l], cosf, sinf)
        y = _matmul(mix, w_out_b, l, tm=1024, tn=512, tk=w_out.shape[1], out_dtype=f32, name='out_proj')
        h, hb = _layer_norm(h, y, ln1_g[l], ln1_b[l], alpha=alpha)
        up = _matmul(hb, w_up_b, l, tm=1024, tn=512, tk=D, out_dtype=bf16, act='relu2', name='mlp_up')
        ff = _matmul(up, w_down_b, l, tm=1024, tn=512, tk=4096, out_dtype=f32, name='mlp_down')
        h, hb = _layer_norm(h, ff, ln2_g[l], ln2_b[l], alpha=alpha)
    return h.reshape(B, T, D)
```

```python
import functools
import math

import jax
import jax.numpy as jnp
from jax import lax
from jax.experimental import pallas as pl
from jax.experimental.pallas import tpu as pltpu

f32 = jnp.float32
bf16 = jnp.bfloat16
i32 = jnp.int32

HEAD_DIM = 128
A_HEADS = 8
A_KV_HEADS = 2
IDX_HEADS = 4
IDX_DIM = 128
DSA_TOPK = 256
GLA_HEADS = 4
GLA_DK = 128
GLA_DV = 256
GLA_GATE_RANK = 16
GLA_TAU = 16.0
GDN_HEADS = 8
CONV_K = 4
MOBA_HEADS = 8
MOBA_BLOCK = 256
MOBA_TOPK = 3
ROPE_THETA = 10000.0
LN_EPS = 1e-5
NORM_EPS = 1e-6

V7X_VMEM_BYTES = 64 * 2**20
VMEM_LIMIT = V7X_VMEM_BYTES * 3 // 4
LANES = 128
SUBLANES = 8

INT_MIN = -2**31
NEG_BIG = -1e30
LOG2E = math.log2(math.e)
ATTN_Q_SCALE = HEAD_DIM ** -0.5 * LOG2E

TOK_TILE = MOBA_BLOCK

P_LAYOUT = (
    ('a_q', 1024), ('b_v', 1024), ('b_r', 1024), ('c_q', 1024), ('c_k', 1024), ('c_v', 1024), ('c_g', 1024),
    ('d_q', 1024), ('d_k', 1024), ('d_v', 1024),
    ('a_iq', 512), ('b_q', 512), ('b_k', 512),
    ('a_k', 256), ('a_v', 256),
    ('a_ik', 128), ('a_iw', 128), ('b_glr', 128), ('c_ab', 128),
)
P_OFF = {}
_o = 0
for _n, _w in P_LAYOUT:
    P_OFF[_n] = (_o, _w)
    _o += _w
P_WIDTH = _o

REF_SPLITS = (
    ('a_q', 1024), ('a_k', 256), ('a_v', 256), ('a_iq', 512), ('a_ik', 128), ('a_iw', 4),
    ('b_q', 512), ('b_k', 512), ('b_v', 1024), ('b_glr', 16), ('b_r', 1024),
    ('c_q', 1024), ('c_k', 1024), ('c_v', 1024), ('c_a', 8), ('c_b', 8), ('c_g', 1024),
    ('d_q', 1024), ('d_k', 1024), ('d_v', 1024),
)
REF_OFF = {}
_o = 0
for _n, _w in REF_SPLITS:
    REF_OFF[_n] = (_o, _w)
    _o += _w


def _dot(a, b):
    return jnp.dot(a, b, preferred_element_type=f32)


def _dot_nt(a, b):
    return lax.dot_general(a, b, (((1,), (1,)), ((), ())), preferred_element_type=f32)


def _dot_tn(a, b):
    return lax.dot_general(a, b, (((0,), (0,)), ((), ())), preferred_element_type=f32)


def _split2(x):
    hi = x.astype(bf16)
    lo = (x - hi.astype(f32)).astype(bf16)
    return hi, lo


def _split3(x):
    hi = x.astype(bf16)
    r = x - hi.astype(f32)
    mid = r.astype(bf16)
    lo = (r - mid.astype(f32)).astype(bf16)
    return hi, mid, lo


def _mm1(x, y):
    return _dot(x.astype(bf16), y.astype(bf16))


def _mm3(x, y):
    xh, xl = _split2(x)
    yh, yl = _split2(y)
    return _dot(xh, yh) + _dot(xh, yl) + _dot(xl, yh)


def _sigmoid(x):
    return 1.0 / (1.0 + jnp.exp(-x))


def _softplus(x):
    return jnp.maximum(x, 0.0) + jnp.log1p(jnp.exp(-jnp.abs(x)))


def _params(sem):
    return pltpu.CompilerParams(dimension_semantics=sem, vmem_limit_bytes=VMEM_LIMIT)


def _mm_kernel(*refs, n_a, nk, act, b_transposed):
    a_refs, b_ref, o_ref = refs[:n_a], refs[n_a], refs[n_a + 1]

    def product():
        if b_transposed:
            return _dot_nt(a_refs[0][...], b_ref[...])
        if n_a == 1:
            return _dot(a_refs[0][...], b_ref[...])
        kw = b_ref.shape[0] // n_a
        out = _dot(a_refs[0][...], b_ref[0:kw, :])
        for g in range(1, n_a):
            out = out + _dot(a_refs[g][...], b_ref[g * kw:(g + 1) * kw, :])
        return out

    def finish(r):
        if act == 'relu2':
            r = jnp.square(jnp.maximum(r, 0.0))
        o_ref[...] = r.astype(o_ref.dtype)

    if nk == 1:
        finish(product())
        return
    acc_ref = refs[n_a + 2]
    k = pl.program_id(2)

    @pl.when(k == 0)
    def _():
        acc_ref[...] = jnp.zeros_like(acc_ref)

    acc_ref[...] += product()

    @pl.when(k == nk - 1)
    def _():
        finish(acc_ref[...])


def _matmul(a, b, layer, *, tm, tn, tk, out_dtype, act=None, b_transposed=False, name='matmul'):
    a_list = a if isinstance(a, (tuple, list)) else (a,)
    n_a = len(a_list)
    M = a_list[0].shape[0]
    _, K, N = b.shape
    if b_transposed:
        assert n_a == 1
        K, N = N, K
    tm, tn, tk = min(tm, M), min(tn, N), min(tk, K)
    assert M % tm == 0 and N % tn == 0 and K % tk == 0
    nk = K // tk
    assert n_a == 1 or nk == 1
    ka = tk // n_a
    b_spec = (pl.BlockSpec((None, tn, tk), lambda i, j, k: (layer, j, k)) if b_transposed else
              pl.BlockSpec((None, tk, tn), lambda i, j, k: (layer, k, j)))
    return pl.pallas_call(
        functools.partial(_mm_kernel, n_a=n_a, nk=nk, act=act, b_transposed=b_transposed),
        grid=(M // tm, N // tn, nk),
        in_specs=[pl.BlockSpec((tm, ka), lambda i, j, k: (i, k))] * n_a + [b_spec],
        out_specs=pl.BlockSpec((tm, tn), lambda i, j, k: (i, j)),
        out_shape=jax.ShapeDtypeStruct((M, N), out_dtype),
        scratch_shapes=[pltpu.VMEM((tm, tn), f32)] if nk > 1 else [],
        compiler_params=_params(("parallel", "parallel", "arbitrary")),
        name=name,
    )(*a_list, b)


def _ln_kernel(*refs, alpha, has_y):
    if has_y:
        x_ref, y_ref, g_ref, b_ref, o_ref, ob_ref = refs
        z = x_ref[...] * alpha + y_ref[...]
    else:
        x_ref, g_ref, b_ref, o_ref, ob_ref = refs
        z = x_ref[...]
    mu = jnp.mean(z, axis=-1, keepdims=True)
    zc = z - mu
    var = jnp.mean(zc * zc, axis=-1, keepdims=True)
    out = zc * lax.rsqrt(var + LN_EPS) * g_ref[...] + b_ref[...]
    o_ref[...] = out
    ob_ref[...] = out.astype(bf16)


def _layer_norm(x, y, g, b, *, alpha=1.0, tm=256):
    M, D = x.shape
    tm = min(tm, M)
    row = pl.BlockSpec((tm, D), lambda i: (i, 0))
    vec = pl.BlockSpec((1, D), lambda i: (0, 0))
    has_y = y is not None
    args = (x, y) if has_y else (x,)
    return pl.pallas_call(
        functools.partial(_ln_kernel, alpha=alpha, has_y=has_y),
        grid=(M // tm,),
        in_specs=[row] * len(args) + [vec, vec],
        out_specs=[row, row],
        out_shape=[jax.ShapeDtypeStruct((M, D), f32), jax.ShapeDtypeStruct((M, D), bf16)],
        compiler_params=_params(("parallel",)),
        name='layer_norm',
    )(*args, g.reshape(1, D), b.reshape(1, D))


def _prep_kernel(aq_ref, ak_ref, av_ref, aiq_ref, aik_ref, aiw_ref,
                 cq_ref, ck_ref, cv_ref, cqh_ref, ckh_ref, cvh_ref,
                 dq_ref, dk_ref, dv_ref, cos_ref, sin_ref, cw_ref,
                 oaq, oak, oavt, oaiq, oaik, oaiwt, ocq, ock, ocv,
                 odq, odqh, odql, odk, odvt, odkmh, odkml, *, TT):
    cosf, sinf = cos_ref[...], sin_ref[...]

    def rope(x):
        return x * cosf + pltpu.roll(x, HEAD_DIM // 2, 1) * sinf

    def head(h):
        return slice(h * HEAD_DIM, (h + 1) * HEAD_DIM)

    for h in range(A_HEADS):
        oaq[0, :, head(h)] = (rope(aq_ref[0, :, head(h)]) * ATTN_Q_SCALE).astype(bf16)
    for g in range(A_KV_HEADS):
        oak[0, :, head(g)] = rope(ak_ref[0, :, head(g)]).astype(bf16)
        oavt[0, 0, head(g), :] = av_ref[0, :, head(g)].T.astype(bf16)
    for h in range(IDX_HEADS):
        oaiq[0, :, head(h)] = rope(aiq_ref[0, :, head(h)]).astype(bf16)
    oaik[0] = rope(aik_ref[0]).astype(bf16)
    oaiwt[0] = aiw_ref[0].T[:SUBLANES, :] * (IDX_HEADS ** -0.5 * IDX_DIM ** -0.5)

    first = pl.program_id(1) == 0
    for j, (x_ref, h_ref, o_ref) in enumerate(((cq_ref, cqh_ref, ocq), (ck_ref, ckh_ref, ock),
                                                (cv_ref, cvh_ref, ocv))):
        width = x_ref.shape[2]
        halo = jnp.where(first, 0.0, h_ref[0])
        cat = jnp.concatenate([halo, x_ref[0]], axis=0)
        w = cw_ref[:, j * width:(j + 1) * width]
        y = cat[SUBLANES - CONV_K + 1:SUBLANES - CONV_K + 1 + TT] * w[0:1]
        for i in range(1, CONV_K):
            s0 = SUBLANES - CONV_K + 1 + i
            y = y + cat[s0:s0 + TT] * w[i:i + 1]
        y = y * _sigmoid(y)
        if j == 2:
            o_ref[0] = y
        else:
            post = HEAD_DIM ** -0.5 if j == 0 else 1.0
            for h in range(GDN_HEADS):
                yh = y[:, head(h)]
                o_ref[0, :, head(h)] = yh * (lax.rsqrt(jnp.sum(yh * yh, axis=-1, keepdims=True) + NORM_EPS) * post)

    for h in range(MOBA_HEADS):
        qr = rope(dq_ref[0, :, head(h)])
        odq[0, :, head(h)] = (qr * ATTN_Q_SCALE).astype(bf16)
        qh, ql = _split2(qr)
        odqh[0, :, head(h)] = qh
        odql[0, :, head(h)] = ql
        kr = rope(dk_ref[0, :, head(h)])
        odk[0, :, head(h)] = kr.astype(bf16)
        kmh, kml = _split2(jnp.mean(kr, axis=0, keepdims=True))
        odkmh[0, 0, :, head(h)] = kmh
        odkml[0, 0, :, head(h)] = kml
        odvt[0, 0, head(h), :] = dv_ref[0, :, head(h)].T.astype(bf16)


def _prep(p, cosf, sinf, conv_w):
    B, T, _ = p.shape
    TT = min(TOK_TILE, T)
    NT = T // TT

    def seg(name, rows=TT):
        off, w = P_OFF[name]
        return pl.BlockSpec((1, rows, w), lambda b, i: (b, i, off // w))

    def halo(name):
        off, w = P_OFF[name]
        return pl.BlockSpec((1, SUBLANES, w),
                            lambda b, i: (b, jnp.maximum(i * (TT // SUBLANES) - 1, 0), off // w))

    in_names = ('a_q', 'a_k', 'a_v', 'a_iq', 'a_ik', 'a_iw', 'c_q', 'c_k', 'c_v')
    in_specs = ([seg(n) for n in in_names] + [halo(n) for n in ('c_q', 'c_k', 'c_v')]
                + [seg(n) for n in ('d_q', 'd_k', 'd_v')]
                + [pl.BlockSpec((TT, HEAD_DIM), lambda b, i: (i, 0))] * 2
                + [pl.BlockSpec(conv_w.shape, lambda b, i: (0, 0))])

    def tok(w, dt):
        return pl.BlockSpec((1, TT, w), lambda b, i: (b, i, 0)), jax.ShapeDtypeStruct((B, T, w), dt)

    def tposed(rows, dt):
        return (pl.BlockSpec((1, 1, rows, TT), lambda b, i: (b, i, 0, 0)),
                jax.ShapeDtypeStruct((B, NT, rows, TT), dt))

    def per_tile(w, dt):
        return (pl.BlockSpec((1, 1, 1, w), lambda b, i: (b, i, 0, 0)),
                jax.ShapeDtypeStruct((B, NT, 1, w), dt))

    outs = [
        tok(1024, bf16), tok(256, bf16), tposed(256, bf16), tok(512, bf16), tok(128, bf16),
        (pl.BlockSpec((1, SUBLANES, TT), lambda b, i: (b, 0, i)), jax.ShapeDtypeStruct((B, SUBLANES, T), f32)),
        tok(1024, f32), tok(1024, f32), tok(1024, f32),
        tok(1024, bf16), tok(1024, bf16), tok(1024, bf16), tok(1024, bf16), tposed(1024, bf16),
        per_tile(1024, bf16), per_tile(1024, bf16),
    ]
    res = pl.pallas_call(
        functools.partial(_prep_kernel, TT=TT),
        grid=(B, NT),
        in_specs=in_specs,
        out_specs=[o[0] for o in outs],
        out_shape=[o[1] for o in outs],
        compiler_params=_params(("parallel", "arbitrary")),
        name='mixer_prep',
    )(*([p] * 15), cosf, sinf, conv_w)
    names = ('aq', 'ak', 'avt', 'aiq', 'aik', 'aiwt', 'cq', 'ck', 'cv', 'dq', 'dqh', 'dql', 'dk', 'dvt',
             'dkmh', 'dkml')
    return dict(zip(names, res))


def _tiles_by_pairs(n, produce, consume, state):
    odd = n % 2

    def single(j, st):
        return consume(j, produce(j), st)

    def pair(m, st):
        j = odd + 2 * m
        first, second = produce(j), produce(j + 1)
        return consume(j + 1, second, consume(j, first, st))

    state = lax.fori_loop(0, odd, single, state)
    return lax.fori_loop(0, n // 2, pair, state)


def _flash_update(p, m_prev, m_new, l_prev, acc_prev, vt, ones):
    pb = p.astype(bf16)
    a = jnp.exp2(m_prev - m_new)
    l_new = a * l_prev + _dot(ones, pb)[0:1]
    acc_new = a * acc_prev + _dot(vt, pb)
    return m_new, l_new, acc_new


def _flash_step(s, mask, m_prev, l_prev, acc_prev, vt, ones):
    s = jnp.where(mask, s, NEG_BIG)
    m_new = jnp.maximum(m_prev, jnp.max(s, axis=0, keepdims=True))
    return _flash_update(jnp.exp2(s - m_new), m_prev, m_new, l_prev, acc_prev, vt, ones)


def _flash_step_cols(s, cols, m_prev, l_prev, acc_prev, vt, ones):
    m_new = jnp.where(cols, jnp.maximum(m_prev, jnp.max(s, axis=0, keepdims=True)), m_prev)
    p = jnp.exp2(s - jnp.where(cols, m_new, -NEG_BIG))
    return _flash_update(p, m_prev, m_new, l_prev, acc_prev, vt, ones)


def _dsa_kernel(iq_ref, iw_ref, q_ref, ik_ref, k_ref, vt_ref, o_ref, keys_ref, acc_ref, *, TT, topk):
    i = pl.program_id(1)
    n_kt = i + 1
    t_idx = i * TT + lax.broadcasted_iota(i32, (1, TT), 1)
    row = lax.broadcasted_iota(i32, (TT, 1), 0)
    iw = iw_ref[0]
    iq = iq_ref[0]

    def tile(kt):
        return pl.ds(pl.multiple_of(kt * TT, TT), TT)

    def head_scores(kt):
        ik = ik_ref[0, tile(kt), :]
        return [_dot_nt(ik, iq[:, h * IDX_DIM:(h + 1) * IDX_DIM]) for h in range(IDX_HEADS)]

    def store_keys(kt, xs, carry):
        acc = jnp.zeros((TT, TT), f32)
        for h in range(IDX_HEADS):
            acc = acc + iw[h:h + 1, :] * jnp.maximum(xs[h], 0.0)
        bits = lax.bitcast_convert_type(acc, i32)
        key = bits ^ ((bits >> 31) & jnp.int32(0x7FFFFFFF))
        key = jnp.where(key == -1, 0, key)
        key = jnp.where(kt * TT + row <= t_idx, key, jnp.int32(INT_MIN))
        keys_ref[tile(kt), :] = key
        return carry

    _tiles_by_pairs(n_kt, head_scores, store_keys, 0)

    def count(pred_fn):
        def body(kt, c):
            ind = pred_fn(keys_ref[tile(kt), :]).astype(i32)
            return c + jnp.sum(ind.reshape(TT // SUBLANES, SUBLANES, TT), axis=0)
        c = lax.fori_loop(0, n_kt, body, jnp.zeros((SUBLANES, TT), i32))
        return jnp.sum(c, axis=0, keepdims=True)

    n_nonneg = count(lambda key: key >= 0)
    nonneg = n_nonneg >= topk
    settled = (t_idx + 1 <= topk) | (nonneg & (count(lambda key: key > 0) < topk))

    def pending(cnt_cur):
        return jnp.max(jnp.where((cnt_cur == topk) | settled, 0, 1))

    def bit_pass(state):
        bi, cur, cnt_cur, _ = state
        cand = cur + lax.shift_left(jnp.int32(1), 31 - bi)
        cnt = count(lambda key: key >= cand)
        ok = cnt >= topk
        cur = jnp.where(ok, cand, cur)
        cnt_cur = jnp.where(ok, cnt, cnt_cur)
        return bi + 1, cur, cnt_cur, pending(cnt_cur)

    cnt0 = jnp.where(nonneg, n_nonneg, n_kt * TT)
    state = (jnp.int32(1), jnp.where(nonneg, 0, jnp.int32(INT_MIN)), cnt0, pending(cnt0))
    _, thr, _, _ = lax.while_loop(lambda st: (st[0] < 32) & (st[3] > 0), bit_pass, state)
    need = (topk - count(lambda key: key > thr)).astype(f32)

    tri = (lax.broadcasted_iota(i32, (TT, TT), 0) > lax.broadcasted_iota(i32, (TT, TT), 1)).astype(bf16)
    ones = jnp.ones((2 * SUBLANES, TT), bf16)
    group = A_HEADS // A_KV_HEADS
    acc_ref[...] = jnp.zeros_like(acc_ref)

    def attend_tile(kt, carry):
        tie_carry, stats = carry
        key = keys_ref[tile(kt), :]
        eq = (key == thr) & (key != INT_MIN)
        eq_f = jnp.where(eq, 1.0, 0.0)
        tie_rank = _dot(tri, eq_f.astype(bf16)) + tie_carry
        mask = (key > thr) | (eq & (tie_rank < need))
        ktile = k_ref[0, tile(kt), :]
        logits = [_dot_nt(ktile[:, (h // group) * HEAD_DIM:(h // group + 1) * HEAD_DIM],
                          q_ref[0, :, h * HEAD_DIM:(h + 1) * HEAD_DIM]) for h in range(A_HEADS)]
        new_stats = []
        for h in range(A_HEADS):
            g = h // group
            m_new, l_new, acc_new = _flash_step(logits[h], mask, *stats[h], acc_ref[h],
                                                vt_ref[0, kt, g * HEAD_DIM:(g + 1) * HEAD_DIM, :], ones)
            acc_ref[h] = acc_new
            new_stats.append((m_new, l_new))
        return tie_carry + jnp.sum(eq_f, axis=0, keepdims=True), tuple(new_stats)

    init = (jnp.zeros((1, TT), f32),
            tuple((jnp.full((1, TT), NEG_BIG, f32), jnp.zeros((1, TT), f32)) for _ in range(A_HEADS)))
    _, stats = lax.fori_loop(0, n_kt, attend_tile, init)
    for h in range(A_HEADS):
        o_ref[0, :, h * HEAD_DIM:(h + 1) * HEAD_DIM] = (acc_ref[h] / stats[h][1]).T.astype(o_ref.dtype)


def _dsa(ops):
    B, T, W = ops['aq'].shape
    TT = min(TOK_TILE, T)
    NT = T // TT
    topk = min(DSA_TOPK, T // 4)
    return pl.pallas_call(
        functools.partial(_dsa_kernel, TT=TT, topk=topk),
        grid=(B, NT),
        in_specs=[
            pl.BlockSpec((1, TT, IDX_HEADS * IDX_DIM), lambda b, i: (b, i, 0)),
            pl.BlockSpec((1, SUBLANES, TT), lambda b, i: (b, 0, i)),
            pl.BlockSpec((1, TT, W), lambda b, i: (b, i, 0)),
            pl.BlockSpec((1, T, IDX_DIM), lambda b, i: (b, 0, 0)),
            pl.BlockSpec((1, T, A_KV_HEADS * HEAD_DIM), lambda b, i: (b, 0, 0)),
            pl.BlockSpec((1, NT, A_KV_HEADS * HEAD_DIM, TT), lambda b, i: (b, 0, 0, 0)),
        ],
        out_specs=pl.BlockSpec((1, TT, W), lambda b, i: (b, i, 0)),
        out_shape=jax.ShapeDtypeStruct((B, T, W), bf16),
        scratch_shapes=[pltpu.VMEM((T, TT), i32), pltpu.VMEM((A_HEADS, HEAD_DIM, TT), f32)],
        compiler_params=_params(("parallel", "arbitrary")),
        name='dsa_attention',
    )(ops['aiq'], ops['aiwt'], ops['aq'], ops['aik'], ops['ak'], ops['avt'])


def _moba_kernel(q_ref, qh_ref, ql_ref, kmh_ref, kml_ref, k_ref, vt_ref, o_ref, sel_ref, acc_ref,
                 *, NB, BS, n_sel, HP):
    i = pl.program_id(2)
    n_idx = lax.broadcasted_iota(i32, (NB, 1), 0)
    past = n_idx < i

    def head(h):
        return slice(h * HEAD_DIM, (h + 1) * HEAD_DIM)

    for h in range(HP):
        kmh, kml = kmh_ref[0, :, head(h)], kml_ref[0, :, head(h)]
        qh, ql = qh_ref[0, :, head(h)], ql_ref[0, :, head(h)]
        gate = _dot_nt(kmh, qh) + _dot_nt(kmh, ql) + _dot_nt(kml, qh)
        g = jnp.where(past, gate, -jnp.inf)
        sel = jnp.zeros((NB, BS), f32)
        for _ in range(n_sel):
            m = jnp.max(g, axis=0, keepdims=True)
            first = jnp.min(jnp.where(g == m, n_idx, NB), axis=0, keepdims=True)
            pick = (n_idx == first) & (m > -jnp.inf)
            sel = jnp.where(pick, 1.0, sel)
            g = jnp.where(pick, -jnp.inf, g)
        sel_ref[h] = sel

    acc_ref[...] = jnp.zeros_like(acc_ref)
    ones = jnp.ones((2 * SUBLANES, BS), bf16)

    def logits_of(n):
        ktile = k_ref[0, pl.ds(pl.multiple_of(n * BS, BS), BS), :]
        return tuple(_dot_nt(ktile[:, head(h)], q_ref[0, :, head(h)]) for h in range(HP))

    def block(n, logits, stats, step_fn):
        new_stats = []
        for h in range(HP):
            m_new, l_new, acc_new = step_fn(h, logits[h], *stats[h], acc_ref[h], vt_ref[0, n, head(h), :], ones)
            acc_ref[h] = acc_new
            new_stats.append((m_new, l_new))
        return tuple(new_stats)

    def past_block(n, logits, stats):
        return block(n, logits, stats,
                     lambda h, s, *rest: _flash_step_cols(s, sel_ref[h, pl.ds(n, 1), :] > 0.5, *rest))

    init = tuple((jnp.full((1, BS), NEG_BIG, f32), jnp.zeros((1, BS), f32)) for _ in range(HP))
    stats = _tiles_by_pairs(i, logits_of, past_block, init)
    causal = lax.broadcasted_iota(i32, (BS, BS), 0) <= lax.broadcasted_iota(i32, (BS, BS), 1)
    stats = block(i, logits_of(i), stats, lambda h, s, *rest: _flash_step(s, causal, *rest))
    for h in range(HP):
        o_ref[0, :, head(h)] = (acc_ref[h] / stats[h][1]).T.astype(o_ref.dtype)


def _moba(ops, *, HP=4):
    B, T, W = ops['dq'].shape
    BS = MOBA_BLOCK
    assert T % BS == 0 and TOK_TILE == BS
    NB = T // BS
    n_sel = min(MOBA_TOPK, NB - 1)
    WP = HP * HEAD_DIM
    kmh = ops['dkmh'].reshape(B, NB, W)
    kml = ops['dkml'].reshape(B, NB, W)
    qspec = pl.BlockSpec((1, BS, WP), lambda b, hg, i: (b, i, hg))
    kmspec = pl.BlockSpec((1, NB, WP), lambda b, hg, i: (b, 0, hg))
    return pl.pallas_call(
        functools.partial(_moba_kernel, NB=NB, BS=BS, n_sel=n_sel, HP=HP),
        grid=(B, W // WP, NB),
        in_specs=[qspec, qspec, qspec, kmspec, kmspec,
                  pl.BlockSpec((1, T, WP), lambda b, hg, i: (b, 0, hg)),
                  pl.BlockSpec((1, NB, WP, BS), lambda b, hg, i: (b, 0, hg, 0))],
        out_specs=qspec,
        out_shape=jax.ShapeDtypeStruct((B, T, W), bf16),
        scratch_shapes=[pltpu.VMEM((HP, NB, BS), f32), pltpu.VMEM((HP, HEAD_DIM, BS), f32)],
        compiler_params=_params(("parallel", "parallel", "arbitrary")),
        name='moba_attention',
    )(ops['dq'], ops['dqh'], ops['dql'], kmh, kml, ops['dk'], ops['dvt'])


def _gla_kernel(q_ref, k_ref, v_ref, glr_ref, r_ref, w2h_ref, w2l_ref, gb_ref, ng_ref, o_ref, st_ref, *, C, R):
    @pl.when(pl.program_id(1) == 0)
    def _():
        st_ref[...] = jnp.zeros_like(st_ref)

    gh, gl = _split2(glr_ref[0])
    logit = _dot(gh, w2h_ref[...]) + _dot(gh, w2l_ref[...]) + _dot(gl, w2h_ref[...]) + gb_ref[...]
    log_a = -_softplus(-logit) * (1.0 / GLA_TAU)
    tril = (lax.broadcasted_iota(i32, (C, C), 0) >= lax.broadcasted_iota(i32, (C, C), 1)).astype(bf16)
    a1, a2, a3 = _split3(log_a)
    b_all = _dot(tril, a1) + _dot(tril, a2) + _dot(tril, a3)
    heads = range(GLA_HEADS)
    kss = [slice(h * GLA_DK, (h + 1) * GLA_DK) for h in heads]
    vss = [slice(h * GLA_DV, (h + 1) * GLA_DV) for h in heads]
    b = [b_all[:, ks] for ks in kss]
    q = [q_ref[0, :, ks] * GLA_DK ** -0.5 for ks in kss]
    k = [k_ref[0, :, ks] for ks in kss]
    vb = [v_ref[0, :, vs].astype(bf16) for vs in vss]
    st = [st_ref[h] for h in heads]
    inter = [_dot_nt((q[h] * jnp.exp(b[h])).astype(bf16), st[h].astype(bf16)) for h in heads]
    att = [[None] * (C // R) for _ in heads]
    for blk in range(C // R):
        r0, r1 = blk * R, (blk + 1) * R
        keep = lax.broadcasted_iota(i32, (R, r1), 1) <= lax.broadcasted_iota(i32, (R, r1), 0) + r0
        for h in heads:
            b0 = b[h][r0 - 1:r0, :] if blk else jnp.zeros((1, GLA_DK), f32)
            qe = q[h][r0:r1] * jnp.exp(b[h][r0:r1] - b0)
            ke = k[h][:r1] * jnp.exp(b0 - b[h][:r1])
            att[h][blk] = jnp.where(keep, _dot_nt(qe.astype(bf16), ke.astype(bf16)), 0.0).astype(bf16)
    intra = [jnp.concatenate([_dot(att[h][blk], vb[h][:(blk + 1) * R]) for blk in range(C // R)], axis=0)
             for h in heads]
    for h in heads:
        b_last = b[h][C - 1:C, :]
        k_dec = k[h] * jnp.exp(b_last - b[h])
        st_ref[h] = st[h] * jnp.exp(b_last) + _dot_tn(vb[h], k_dec.astype(bf16))
    for h in heads:
        o = inter[h] + intra[h]
        y = o * lax.rsqrt(jnp.mean(o * o, axis=-1, keepdims=True) + NORM_EPS) * ng_ref[...]
        rh = r_ref[0, :, vss[h]]
        o_ref[0, :, vss[h]] = (y * (rh * _sigmoid(rh))).astype(o_ref.dtype)


def _gla(p, w2, gate_b, norm_g, *, C=128, R=32):
    B, T, _ = p.shape
    C = min(C, T)
    W = GLA_HEADS * GLA_DV
    w2p = jnp.zeros((LANES, GLA_HEADS * GLA_DK), f32).at[:GLA_GATE_RANK].set(w2)
    w2h, w2l = _split2(w2p)

    def seg(name):
        off, w = P_OFF[name]
        return pl.BlockSpec((1, C, w), lambda b, c: (b, c, off // w))

    full = lambda a: pl.BlockSpec(a.shape, lambda b, c: (0,) * a.ndim)
    gb = gate_b.reshape(1, -1)
    ng = norm_g.reshape(1, -1)
    return pl.pallas_call(
        functools.partial(_gla_kernel, C=C, R=min(R, C)),
        grid=(B, T // C),
        in_specs=[seg('b_q'), seg('b_k'), seg('b_v'), seg('b_glr'), seg('b_r'),
                  full(w2h), full(w2l), full(gb), full(ng)],
        out_specs=pl.BlockSpec((1, C, W), lambda b, c: (b, c, 0)),
        out_shape=jax.ShapeDtypeStruct((B, T, W), bf16),
        scratch_shapes=[pltpu.VMEM((GLA_HEADS, GLA_DV, GLA_DK), f32)],
        compiler_params=_params(("parallel", "arbitrary")),
        name='gla',
    )(p, p, p, p, p, w2h, w2l, gb, ng)


def _unit_lower_inverses(mats, row, col):
    C = mats[0].shape[0]
    eye = jnp.where(row == col, 1.0, 0.0)
    blk = (row >> 3) == (col >> 3)
    n1 = [jnp.where(blk, a, 0.0) for a in mats]
    n2 = [_mm3(x, x) for x in n1]
    n4 = [_mm3(x, x) for x in n2]
    t = [_mm3(eye - x1, eye + x2) for x1, x2 in zip(n1, n2)]
    t = [_mm3(x, eye + x4) for x, x4 in zip(t, n4)]
    s = SUBLANES
    while s < C:
        sh = s.bit_length() - 1
        sel = ((row >> (sh + 1)) == (col >> (sh + 1))) & ((row >> sh) != (col >> sh))
        left = [_mm1(x, jnp.where(sel, a, 0.0)) for x, a in zip(t, mats)]
        t = [x - _mm1(y, x) for x, y in zip(t, left)]
        s *= 2
    return t


def _gdn_kernel(q_ref, k_ref, v_ref, ab_ref, gate_ref, alog_ref, dtb_ref, ng_ref, o_ref, s_ref, *, C):
    @pl.when(pl.program_id(1) == 0)
    def _():
        s_ref[...] = jnp.zeros_like(s_ref)

    H = GDN_HEADS
    row = lax.broadcasted_iota(i32, (C, C), 0)
    col = lax.broadcasted_iota(i32, (C, C), 1)
    ab = ab_ref[0]
    g_all = -jnp.exp(alog_ref[...]) * _softplus(ab + dtb_ref[...])
    beta_all = _sigmoid(ab)
    tril = (row >= col).astype(bf16)
    g1, g2, g3 = _split3(g_all)
    gam_all = _dot(tril, g1) + _dot(tril, g2) + _dot(tril, g3)
    gam_t = gam_all.T
    heads = range(H)
    sls = [slice(h * HEAD_DIM, (h + 1) * HEAD_DIM) for h in heads]
    q = [q_ref[0, :, sl] for sl in sls]
    k = [k_ref[0, :, sl] for sl in sls]
    kb = [x.astype(bf16) for x in k]
    gcol = [gam_all[:, h:h + 1] for h in heads]
    bcol = [beta_all[:, H + h:H + h + 1] for h in heads]
    egam = [jnp.exp(x) for x in gcol]
    dec = [jnp.where(row >= col, jnp.exp(jnp.minimum(gcol[h] - gam_t[h:h + 1, :], 0.0)), 0.0) for h in heads]
    kk = [_dot_nt(x, x) for x in kb]
    qk = [_dot_nt(q[h].astype(bf16), kb[h]) for h in heads]
    a = [jnp.where(row > col, bcol[h] * dec[h] * kk[h], 0.0) for h in heads]
    tinv = _unit_lower_inverses(a, row, col)
    sol = [_mm1(tinv[h], jnp.concatenate([bcol[h] * v_ref[0, :, sls[h]], (bcol[h] * egam[h]) * k[h]], axis=1))
           for h in heads]
    sb = [s_ref[h].astype(bf16) for h in heads]
    ws = [_dot(sol[h][:, HEAD_DIM:].astype(bf16), sb[h]) for h in heads]
    qs = [_dot((q[h] * egam[h]).astype(bf16), sb[h]) for h in heads]
    db = [(sol[h][:, :HEAD_DIM] - ws[h]).astype(bf16) for h in heads]
    o = [qs[h] + _dot((qk[h] * dec[h]).astype(bf16), db[h]) for h in heads]
    for h in heads:
        g_last = gcol[h][C - 1:C, :]
        k_dec = k[h] * jnp.exp(g_last - gcol[h])
        s_ref[h] = jnp.exp(g_last) * s_ref[h] + _dot_tn(k_dec.astype(bf16), db[h])
    for h in heads:
        y = o[h] * lax.rsqrt(jnp.mean(o[h] * o[h], axis=-1, keepdims=True) + NORM_EPS) * ng_ref[...]
        gt = gate_ref[0, :, sls[h]]
        o_ref[0, :, sls[h]] = (y * (gt * _sigmoid(gt))).astype(o_ref.dtype)


def _gdn(cq, ck, cv, p, a_log, dt_bias, norm_g, *, C=128):
    B, T, W = cq.shape
    C = min(C, T)
    H = GDN_HEADS
    alog = jnp.zeros((1, LANES), f32).at[0, :H].set(a_log)
    dtb = jnp.zeros((1, LANES), f32).at[0, :H].set(dt_bias)
    ng = norm_g.reshape(1, -1)
    tok = pl.BlockSpec((1, C, W), lambda b, c: (b, c, 0))

    def seg(name):
        off, w = P_OFF[name]
        return pl.BlockSpec((1, C, w), lambda b, c: (b, c, off // w))

    full = lambda a: pl.BlockSpec(a.shape, lambda b, c: (0,) * a.ndim)
    return pl.pallas_call(
        functools.partial(_gdn_kernel, C=C),
        grid=(B, T // C),
        in_specs=[tok, tok, tok, seg('c_ab'), seg('c_g'), full(alog), full(dtb), full(ng)],
        out_specs=tok,
        out_shape=jax.ShapeDtypeStruct((B, T, W), bf16),
        scratch_shapes=[pltpu.VMEM((H, HEAD_DIM, HEAD_DIM), f32)],
        compiler_params=_params(("parallel", "arbitrary")),
        name='gated_delta_rule',
    )(cq, ck, cv, p, p, alog, dtb, ng)


def _pack_kernel(src_ref, valid_ref, w_ref, o_ref):
    rows = lax.broadcasted_iota(i32, (LANES, 1), 0)
    keep = rows < valid_ref[pl.program_id(0)]
    for l in range(o_ref.shape[0]):
        o_ref[l] = jnp.where(keep, w_ref[:, l, :], 0.0).astype(bf16)


def _pack_w_in(w_in):
    L, D, d_in = w_in.shape
    wt = jnp.transpose(w_in, (2, 0, 1))
    src, valid = [], []
    for name, width in P_LAYOUT:
        s0, w = (REF_OFF['c_a'][0], 2 * GDN_HEADS) if name == 'c_ab' else REF_OFF[name]
        for r in range(0, width, LANES):
            assert s0 + r + LANES <= d_in
            src.append(s0 + r)
            valid.append(min(max(w - r, 0), LANES))
    grid_spec = pltpu.PrefetchScalarGridSpec(
        num_scalar_prefetch=2,
        grid=(P_WIDTH // LANES,),
        in_specs=[pl.BlockSpec((pl.Element(LANES), pl.Element(L), pl.Element(D)),
                               lambda j, src, valid: (src[j], 0, 0))],
        out_specs=pl.BlockSpec((L, LANES, D), lambda j, src, valid: (0, j, 0)),
    )
    return pl.pallas_call(
        _pack_kernel,
        grid_spec=grid_spec,
        out_shape=jax.ShapeDtypeStruct((L, P_WIDTH, D), bf16),
        compiler_params=_params(("arbitrary",)),
        name='pack_w_in',
    )(jnp.asarray(src, i32), jnp.asarray(valid, i32), wt)


def _token_mixers(hb, B, T, layer, w_in, gla_gate_w2, gla_gate_b, gla_norm_g, gdn_conv_w, gdn_a_log, gdn_dt_bias,
                  gdn_norm_g, cosf, sinf):
    p = _matmul(hb, w_in, layer, tm=1024, tn=512, tk=hb.shape[1], out_dtype=f32, b_transposed=True,
                name='in_proj')
    p = p.reshape(B, T, P_WIDTH)
    ops = _prep(p, cosf, sinf, gdn_conv_w)
    y_a = _dsa(ops)
    y_b = _gla(p, gla_gate_w2, gla_gate_b, gla_norm_g)
    y_c = _gdn(ops['cq'], ops['ck'], ops['cv'], p, gdn_a_log, gdn_dt_bias, gdn_norm_g)
    y_d = _moba(ops)
    return tuple(y.reshape(B * T, -1) for y in (y_a, y_b, y_c, y_d))


def _rope_tables(T):
    inv = ROPE_THETA ** (-jnp.arange(0, HEAD_DIM, 2, dtype=f32) / HEAD_DIM)
    ang = jnp.arange(T, dtype=f32)[:, None] * inv[None, :]
    cos, sin = jnp.cos(ang), jnp.sin(ang)
    return jnp.concatenate([cos, cos], axis=1), jnp.concatenate([-sin, sin], axis=1)


def kernel(x, ln_in_g, ln_in_b, w_in, w_out, ln1_g, ln1_b, gla_gate_w2, gla_gate_b, gla_norm_g, gdn_conv_w,
           gdn_a_log, gdn_dt_bias, gdn_norm_g, w_up, w_down, ln2_g, ln2_b):
    B, T, D = x.shape
    depth = w_in.shape[0]
    alpha = (2 * depth) ** 0.25
    cosf, sinf = _rope_tables(T)
    h, hb = _layer_norm(x.reshape(B * T, D), None, ln_in_g, ln_in_b)
    w_in_p = _pack_w_in(w_in)
    w_out_b, w_up_b, w_down_b = w_out.astype(bf16), w_up.astype(bf16), w_down.astype(bf16)
    for l in range(depth):
        mix = _token_mixers(hb, B, T, l, w_in_p, gla_gate_w2[l], gla_gate_b[l], gla_norm_g[l], gdn_conv_w[l],
                            gdn_a_log[l], gdn_dt_bias[l], gdn_norm_g[l], cosf, sinf)
        y = _matmul(mix, w_out_b, l, tm=1024, tn=512, tk=w_out.shape[1], out_dtype=f32, name='out_proj')
        h, hb = _layer_norm(h, y, ln1_g[l], ln1_b[l], alpha=alpha)
        up = _matmul(hb, w_up_b, l, tm=1024, tn=512, tk=D, out_dtype=bf16, act='relu2', name='mlp_up')
        ff = _matmul(up, w_down_b, l, tm=1024, tn=1024, tk=2048, out_dtype=f32, name='mlp_down')
        h, hb = _layer_norm(h, ff, ln2_g[l], ln2_b[l], alpha=alpha)
    return h.reshape(B, T, D)
```

```python
import functools
import math

import jax
import jax.numpy as jnp
from jax import lax
from jax.experimental import pallas as pl
from jax.experimental.pallas import tpu as pltpu

f32 = jnp.float32
bf16 = jnp.bfloat16
i32 = jnp.int32

HEAD_DIM = 128
A_HEADS = 8
A_KV_HEADS = 2
IDX_HEADS = 4
IDX_DIM = 128
DSA_TOPK = 256
GLA_HEADS = 4
GLA_DK = 128
GLA_DV = 256
GLA_GATE_RANK = 16
GLA_TAU = 16.0
GDN_HEADS = 8
CONV_K = 4
MOBA_HEADS = 8
MOBA_BLOCK = 256
MOBA_TOPK = 3
ROPE_THETA = 10000.0
LN_EPS = 1e-5
NORM_EPS = 1e-6

V7X_VMEM_BYTES = 64 * 2**20
VMEM_LIMIT = V7X_VMEM_BYTES * 3 // 4
LANES = 128
SUBLANES = 8

INT_MIN = -2**31
NEG_BIG = -1e30
LOG2E = math.log2(math.e)
ATTN_Q_SCALE = HEAD_DIM ** -0.5 * LOG2E

TOK_TILE = MOBA_BLOCK

P_LAYOUT = (
    ('a_q', 1024), ('b_v', 1024), ('b_r', 1024), ('c_q', 1024), ('c_k', 1024), ('c_v', 1024), ('c_g', 1024),
    ('d_q', 1024), ('d_k', 1024), ('d_v', 1024),
    ('a_iq', 512), ('b_q', 512), ('b_k', 512),
    ('a_k', 256), ('a_v', 256),
    ('a_ik', 128), ('a_iw', 128), ('b_glr', 128), ('c_ab', 128),
)
P_OFF = {}
_o = 0
for _n, _w in P_LAYOUT:
    P_OFF[_n] = (_o, _w)
    _o += _w
P_WIDTH = _o

REF_SPLITS = (
    ('a_q', 1024), ('a_k', 256), ('a_v', 256), ('a_iq', 512), ('a_ik', 128), ('a_iw', 4),
    ('b_q', 512), ('b_k', 512), ('b_v', 1024), ('b_glr', 16), ('b_r', 1024),
    ('c_q', 1024), ('c_k', 1024), ('c_v', 1024), ('c_a', 8), ('c_b', 8), ('c_g', 1024),
    ('d_q', 1024), ('d_k', 1024), ('d_v', 1024),
)
REF_OFF = {}
_o = 0
for _n, _w in REF_SPLITS:
    REF_OFF[_n] = (_o, _w)
    _o += _w


def _dot(a, b):
    return jnp.dot(a, b, preferred_element_type=f32)


def _dot_nt(a, b):
    return lax.dot_general(a, b, (((1,), (1,)), ((), ())), preferred_element_type=f32)


def _dot_tn(a, b):
    return lax.dot_general(a, b, (((0,), (0,)), ((), ())), preferred_element_type=f32)


def _split2(x):
    hi = x.astype(bf16)
    lo = (x - hi.astype(f32)).astype(bf16)
    return hi, lo


def _split3(x):
    hi = x.astype(bf16)
    r = x - hi.astype(f32)
    mid = r.astype(bf16)
    lo = (r - mid.astype(f32)).astype(bf16)
    return hi, mid, lo


def _mm1(x, y):
    return _dot(x.astype(bf16), y.astype(bf16))


def _mm3(x, y):
    xh, xl = _split2(x)
    yh, yl = _split2(y)
    return _dot(xh, yh) + _dot(xh, yl) + _dot(xl, yh)


def _sigmoid(x):
    return 1.0 / (1.0 + jnp.exp(-x))


def _softplus(x):
    return jnp.maximum(x, 0.0) + jnp.log1p(jnp.exp(-jnp.abs(x)))


def _params(sem):
    return pltpu.CompilerParams(dimension_semantics=sem, vmem_limit_bytes=VMEM_LIMIT)


def _mm_kernel(*refs, n_a, nk, act, b_transposed):
    a_refs, b_ref, o_ref = refs[:n_a], refs[n_a], refs[n_a + 1]

    def product():
        if b_transposed:
            return _dot_nt(a_refs[0][...], b_ref[...])
        if n_a == 1:
            return _dot(a_refs[0][...], b_ref[...])
        kw = b_ref.shape[0] // n_a
        out = _dot(a_refs[0][...], b_ref[0:kw, :])
        for g in range(1, n_a):
            out = out + _dot(a_refs[g][...], b_ref[g * kw:(g + 1) * kw, :])
        return out

    def finish(r):
        if act == 'relu2':
            r = jnp.square(jnp.maximum(r, 0.0))
        o_ref[...] = r.astype(o_ref.dtype)

    if nk == 1:
        finish(product())
        return
    acc_ref = refs[n_a + 2]
    k = pl.program_id(2)

    @pl.when(k == 0)
    def _():
        acc_ref[...] = jnp.zeros_like(acc_ref)

    acc_ref[...] += product()

    @pl.when(k == nk - 1)
    def _():
        finish(acc_ref[...])


def _matmul(a, b, layer, *, tm, tn, tk, out_dtype, act=None, b_transposed=False, name='matmul'):
    a_list = a if isinstance(a, (tuple, list)) else (a,)
    n_a = len(a_list)
    M = a_list[0].shape[0]
    _, K, N = b.shape
    if b_transposed:
        assert n_a == 1
        K, N = N, K
    tm, tn, tk = min(tm, M), min(tn, N), min(tk, K)
    assert M % tm == 0 and N % tn == 0 and K % tk == 0
    nk = K // tk
    assert n_a == 1 or nk == 1
    ka = tk // n_a
    b_spec = (pl.BlockSpec((None, tn, tk), lambda i, j, k: (layer, j, k)) if b_transposed else
              pl.BlockSpec((None, tk, tn), lambda i, j, k: (layer, k, j)))
    return pl.pallas_call(
        functools.partial(_mm_kernel, n_a=n_a, nk=nk, act=act, b_transposed=b_transposed),
        grid=(M // tm, N // tn, nk),
        in_specs=[pl.BlockSpec((tm, ka), lambda i, j, k: (i, k))] * n_a + [b_spec],
        out_specs=pl.BlockSpec((tm, tn), lambda i, j, k: (i, j)),
        out_shape=jax.ShapeDtypeStruct((M, N), out_dtype),
        scratch_shapes=[pltpu.VMEM((tm, tn), f32)] if nk > 1 else [],
        compiler_params=_params(("parallel", "parallel", "arbitrary")),
        name=name,
    )(*a_list, b)


def _ln_kernel(*refs, alpha, has_y):
    if has_y:
        x_ref, y_ref, g_ref, b_ref, o_ref, ob_ref = refs
        z = x_ref[...] * alpha + y_ref[...]
    else:
        x_ref, g_ref, b_ref, o_ref, ob_ref = refs
        z = x_ref[...]
    mu = jnp.mean(z, axis=-1, keepdims=True)
    zc = z - mu
    var = jnp.mean(zc * zc, axis=-1, keepdims=True)
    out = zc * lax.rsqrt(var + LN_EPS) * g_ref[...] + b_ref[...]
    o_ref[...] = out
    ob_ref[...] = out.astype(bf16)


def _layer_norm(x, y, g, b, *, alpha=1.0, tm=256):
    M, D = x.shape
    tm = min(tm, M)
    row = pl.BlockSpec((tm, D), lambda i: (i, 0))
    vec = pl.BlockSpec((1, D), lambda i: (0, 0))
    has_y = y is not None
    args = (x, y) if has_y else (x,)
    return pl.pallas_call(
        functools.partial(_ln_kernel, alpha=alpha, has_y=has_y),
        grid=(M // tm,),
        in_specs=[row] * len(args) + [vec, vec],
        out_specs=[row, row],
        out_shape=[jax.ShapeDtypeStruct((M, D), f32), jax.ShapeDtypeStruct((M, D), bf16)],
        compiler_params=_params(("parallel",)),
        name='layer_norm',
    )(*args, g.reshape(1, D), b.reshape(1, D))


def _prep_kernel(aq_ref, ak_ref, av_ref, aiq_ref, aik_ref, aiw_ref,
                 cq_ref, ck_ref, cv_ref, cqh_ref, ckh_ref, cvh_ref,
                 dq_ref, dk_ref, dv_ref, cos_ref, sin_ref, cw_ref,
                 oaq, oak, oavt, oaiq, oaik, oaiwt, ocq, ock, ocv,
                 odq, odqh, odql, odk, odvt, odkmh, odkml, *, TT):
    cosf, sinf = cos_ref[...], sin_ref[...]

    def rope(x):
        return x * cosf + pltpu.roll(x, HEAD_DIM // 2, 1) * sinf

    def head(h):
        return slice(h * HEAD_DIM, (h + 1) * HEAD_DIM)

    for h in range(A_HEADS):
        oaq[0, :, head(h)] = (rope(aq_ref[0, :, head(h)]) * ATTN_Q_SCALE).astype(bf16)
    for g in range(A_KV_HEADS):
        oak[0, :, head(g)] = rope(ak_ref[0, :, head(g)]).astype(bf16)
        oavt[0, 0, head(g), :] = av_ref[0, :, head(g)].T.astype(bf16)
    for h in range(IDX_HEADS):
        oaiq[0, :, head(h)] = rope(aiq_ref[0, :, head(h)]).astype(bf16)
    oaik[0] = rope(aik_ref[0]).astype(bf16)
    oaiwt[0] = aiw_ref[0].T[:SUBLANES, :] * (IDX_HEADS ** -0.5 * IDX_DIM ** -0.5)

    first = pl.program_id(1) == 0
    for j, (x_ref, h_ref, o_ref) in enumerate(((cq_ref, cqh_ref, ocq), (ck_ref, ckh_ref, ock),
                                                (cv_ref, cvh_ref, ocv))):
        width = x_ref.shape[2]
        halo = jnp.where(first, 0.0, h_ref[0])
        cat = jnp.concatenate([halo, x_ref[0]], axis=0)
        w = cw_ref[:, j * width:(j + 1) * width]
        y = cat[SUBLANES - CONV_K + 1:SUBLANES - CONV_K + 1 + TT] * w[0:1]
        for i in range(1, CONV_K):
            s0 = SUBLANES - CONV_K + 1 + i
            y = y + cat[s0:s0 + TT] * w[i:i + 1]
        y = y * _sigmoid(y)
        if j == 2:
            o_ref[0] = y
        else:
            post = HEAD_DIM ** -0.5 if j == 0 else 1.0
            for h in range(GDN_HEADS):
                yh = y[:, head(h)]
                o_ref[0, :, head(h)] = yh * (lax.rsqrt(jnp.sum(yh * yh, axis=-1, keepdims=True) + NORM_EPS) * post)

    for h in range(MOBA_HEADS):
        qr = rope(dq_ref[0, :, head(h)])
        odq[0, :, head(h)] = (qr * ATTN_Q_SCALE).astype(bf16)
        qh, ql = _split2(qr)
        odqh[0, :, head(h)] = qh
        odql[0, :, head(h)] = ql
        kr = rope(dk_ref[0, :, head(h)])
        odk[0, :, head(h)] = kr.astype(bf16)
        kmh, kml = _split2(jnp.mean(kr, axis=0, keepdims=True))
        odkmh[0, 0, :, head(h)] = kmh
        odkml[0, 0, :, head(h)] = kml
        odvt[0, 0, head(h), :] = dv_ref[0, :, head(h)].T.astype(bf16)


def _prep(p, cosf, sinf, conv_w):
    B, T, _ = p.shape
    TT = min(TOK_TILE, T)
    NT = T // TT

    def seg(name, rows=TT):
        off, w = P_OFF[name]
        return pl.BlockSpec((1, rows, w), lambda b, i: (b, i, off // w))

    def halo(name):
        off, w = P_OFF[name]
        return pl.BlockSpec((1, SUBLANES, w),
                            lambda b, i: (b, jnp.maximum(i * (TT // SUBLANES) - 1, 0), off // w))

    in_names = ('a_q', 'a_k', 'a_v', 'a_iq', 'a_ik', 'a_iw', 'c_q', 'c_k', 'c_v')
    in_specs = ([seg(n) for n in in_names] + [halo(n) for n in ('c_q', 'c_k', 'c_v')]
                + [seg(n) for n in ('d_q', 'd_k', 'd_v')]
                + [pl.BlockSpec((TT, HEAD_DIM), lambda b, i: (i, 0))] * 2
                + [pl.BlockSpec(conv_w.shape, lambda b, i: (0, 0))])

    def tok(w, dt):
        return pl.BlockSpec((1, TT, w), lambda b, i: (b, i, 0)), jax.ShapeDtypeStruct((B, T, w), dt)

    def tposed(rows, dt):
        return (pl.BlockSpec((1, 1, rows, TT), lambda b, i: (b, i, 0, 0)),
                jax.ShapeDtypeStruct((B, NT, rows, TT), dt))

    def per_tile(w, dt):
        return (pl.BlockSpec((1, 1, 1, w), lambda b, i: (b, i, 0, 0)),
                jax.ShapeDtypeStruct((B, NT, 1, w), dt))

    outs = [
        tok(1024, bf16), tok(256, bf16), tposed(256, bf16), tok(512, bf16), tok(128, bf16),
        (pl.BlockSpec((1, SUBLANES, TT), lambda b, i: (b, 0, i)), jax.ShapeDtypeStruct((B, SUBLANES, T), f32)),
        tok(1024, f32), tok(1024, f32), tok(1024, f32),
        tok(1024, bf16), tok(1024, bf16), tok(1024, bf16), tok(1024, bf16), tposed(1024, bf16),
        per_tile(1024, bf16), per_tile(1024, bf16),
    ]
    res = pl.pallas_call(
        functools.partial(_prep_kernel, TT=TT),
        grid=(B, NT),
        in_specs=in_specs,
        out_specs=[o[0] for o in outs],
        out_shape=[o[1] for o in outs],
        compiler_params=_params(("parallel", "arbitrary")),
        name='mixer_prep',
    )(*([p] * 15), cosf, sinf, conv_w)
    names = ('aq', 'ak', 'avt', 'aiq', 'aik', 'aiwt', 'cq', 'ck', 'cv', 'dq', 'dqh', 'dql', 'dk', 'dvt',
             'dkmh', 'dkml')
    return dict(zip(names, res))


def _tiles_by_pairs(n, produce, consume, state):
    odd = n % 2

    def single(j, st):
        return consume(j, produce(j), st)

    def pair(m, st):
        j = odd + 2 * m
        first, second = produce(j), produce(j + 1)
        return consume(j + 1, second, consume(j, first, st))

    state = lax.fori_loop(0, odd, single, state)
    return lax.fori_loop(0, n // 2, pair, state)


def _flash_update(p, m_prev, m_new, l_prev, acc_prev, vt, ones):
    pb = p.astype(bf16)
    a = jnp.exp2(m_prev - m_new)
    l_new = a * l_prev + _dot(ones, pb)[0:1]
    acc_new = a * acc_prev + _dot(vt, pb)
    return m_new, l_new, acc_new


def _flash_step(s, mask, m_prev, l_prev, acc_prev, vt, ones):
    s = jnp.where(mask, s, NEG_BIG)
    m_new = jnp.maximum(m_prev, jnp.max(s, axis=0, keepdims=True))
    return _flash_update(jnp.exp2(s - m_new), m_prev, m_new, l_prev, acc_prev, vt, ones)


def _flash_step_cols(s, cols, m_prev, l_prev, acc_prev, vt, ones):
    m_new = jnp.where(cols, jnp.maximum(m_prev, jnp.max(s, axis=0, keepdims=True)), m_prev)
    p = jnp.exp2(s - jnp.where(cols, m_new, -NEG_BIG))
    return _flash_update(p, m_prev, m_new, l_prev, acc_prev, vt, ones)


def _dsa_kernel(iq_ref, iw_ref, q_ref, ik_ref, k_ref, vt_ref, o_ref, keys_ref, gmax_ref, acc_ref, *, TT, topk):
    i = pl.program_id(1)
    n_kt = i + 1
    t_idx = i * TT + lax.broadcasted_iota(i32, (1, TT), 1)
    row = lax.broadcasted_iota(i32, (TT, 1), 0)
    iw = iw_ref[0]
    iq = iq_ref[0]

    def tile(kt):
        return pl.ds(pl.multiple_of(kt * TT, TT), TT)

    def head_scores(kt):
        ik = ik_ref[0, tile(kt), :]
        return [_dot_nt(ik, iq[:, h * IDX_DIM:(h + 1) * IDX_DIM]) for h in range(IDX_HEADS)]

    def store_keys(kt, xs, carry):
        acc = jnp.zeros((TT, TT), f32)
        for h in range(IDX_HEADS):
            acc = acc + iw[h:h + 1, :] * jnp.maximum(xs[h], 0.0)
        bits = lax.bitcast_convert_type(acc, i32)
        key = bits ^ ((bits >> 31) & jnp.int32(0x7FFFFFFF))
        key = jnp.where(key == -1, 0, key)
        key = jnp.where(kt * TT + row <= t_idx, key, jnp.int32(INT_MIN))
        keys_ref[tile(kt), :] = key
        gmax_ref[...] = jnp.maximum(gmax_ref[...], key)
        return carry

    gmax_ref[...] = jnp.full((TT, TT), INT_MIN, i32)
    _tiles_by_pairs(n_kt, head_scores, store_keys, 0)

    def count(pred_fn):
        def body(kt, c):
            ind = pred_fn(keys_ref[tile(kt), :]).astype(i32)
            return c + jnp.sum(ind.reshape(TT // SUBLANES, SUBLANES, TT), axis=0)
        c = lax.fori_loop(0, n_kt, body, jnp.zeros((SUBLANES, TT), i32))
        return jnp.sum(c, axis=0, keepdims=True)

    n_nonneg = count(lambda key: key >= 0)
    nonneg = n_nonneg >= topk
    takes_all = t_idx + 1 <= topk
    zero_thr = nonneg & (count(lambda key: key > 0) < topk)
    gmax = gmax_ref[...]
    lo = jnp.min(gmax, axis=0, keepdims=True)
    hi = jnp.max(gmax, axis=0, keepdims=True) + 1
    cnt_lo = jnp.where(nonneg & (lo <= 0), n_nonneg, -1)
    lo, hi = jnp.where(nonneg, jnp.maximum(lo, 0), lo), jnp.where(nonneg, hi, jnp.minimum(hi, 0))

    def pending(lo, hi, cnt_lo):
        settled = takes_all | zero_thr | (cnt_lo == topk) | (hi - lo == 1)
        return jnp.max(jnp.where(settled, 0, 1))

    def bisect(state):
        it, lo, hi, cnt_lo, _ = state
        mid = lo + lax.shift_right_logical(hi - lo, 1)
        cnt = count(lambda key: key >= mid)
        ok = cnt >= topk
        lo, hi, cnt_lo = jnp.where(ok, mid, lo), jnp.where(ok, hi, mid), jnp.where(ok, cnt, cnt_lo)
        return it + 1, lo, hi, cnt_lo, pending(lo, hi, cnt_lo)

    state = (jnp.int32(0), lo, hi, cnt_lo, pending(lo, hi, cnt_lo))
    _, lo, _, _, _ = lax.while_loop(lambda st: (st[0] < 33) & (st[4] > 0), bisect, state)
    thr = jnp.where(takes_all, jnp.int32(INT_MIN), jnp.where(zero_thr, 0, lo))
    need = (topk - count(lambda key: key > thr)).astype(f32)

    tri = (lax.broadcasted_iota(i32, (TT, TT), 0) > lax.broadcasted_iota(i32, (TT, TT), 1)).astype(bf16)
    ones = jnp.ones((2 * SUBLANES, TT), bf16)
    group = A_HEADS // A_KV_HEADS
    acc_ref[...] = jnp.zeros_like(acc_ref)

    def attend_tile(kt, carry):
        tie_carry, stats = carry
        key = keys_ref[tile(kt), :]
        eq = (key == thr) & (key != INT_MIN)
        eq_f = jnp.where(eq, 1.0, 0.0)
        tie_rank = _dot(tri, eq_f.astype(bf16)) + tie_carry
        mask = (key > thr) | (eq & (tie_rank < need))
        ktile = k_ref[0, tile(kt), :]
        logits = [_dot_nt(ktile[:, (h // group) * HEAD_DIM:(h // group + 1) * HEAD_DIM],
                          q_ref[0, :, h * HEAD_DIM:(h + 1) * HEAD_DIM]) for h in range(A_HEADS)]
        new_stats = []
        for h in range(A_HEADS):
            g = h // group
            m_new, l_new, acc_new = _flash_step(logits[h], mask, *stats[h], acc_ref[h],
                                                vt_ref[0, kt, g * HEAD_DIM:(g + 1) * HEAD_DIM, :], ones)
            acc_ref[h] = acc_new
            new_stats.append((m_new, l_new))
        return tie_carry + jnp.sum(eq_f, axis=0, keepdims=True), tuple(new_stats)

    init = (jnp.zeros((1, TT), f32),
            tuple((jnp.full((1, TT), NEG_BIG, f32), jnp.zeros((1, TT), f32)) for _ in range(A_HEADS)))
    _, stats = lax.fori_loop(0, n_kt, attend_tile, init)
    for h in range(A_HEADS):
        o_ref[0, :, h * HEAD_DIM:(h + 1) * HEAD_DIM] = (acc_ref[h] / stats[h][1]).T.astype(o_ref.dtype)


def _dsa(ops):
    B, T, W = ops['aq'].shape
    TT = min(TOK_TILE, T)
    NT = T // TT
    topk = min(DSA_TOPK, T // 4)
    assert topk <= TT
    return pl.pallas_call(
        functools.partial(_dsa_kernel, TT=TT, topk=topk),
        grid=(B, NT),
        in_specs=[
            pl.BlockSpec((1, TT, IDX_HEADS * IDX_DIM), lambda b, i: (b, i, 0)),
            pl.BlockSpec((1, SUBLANES, TT), lambda b, i: (b, 0, i)),
            pl.BlockSpec((1, TT, W), lambda b, i: (b, i, 0)),
            pl.BlockSpec((1, T, IDX_DIM), lambda b, i: (b, 0, 0)),
            pl.BlockSpec((1, T, A_KV_HEADS * HEAD_DIM), lambda b, i: (b, 0, 0)),
            pl.BlockSpec((1, NT, A_KV_HEADS * HEAD_DIM, TT), lambda b, i: (b, 0, 0, 0)),
        ],
        out_specs=pl.BlockSpec((1, TT, W), lambda b, i: (b, i, 0)),
        out_shape=jax.ShapeDtypeStruct((B, T, W), bf16),
        scratch_shapes=[pltpu.VMEM((T, TT), i32), pltpu.VMEM((TT, TT), i32),
                        pltpu.VMEM((A_HEADS, HEAD_DIM, TT), f32)],
        compiler_params=_params(("parallel", "arbitrary")),
        name='dsa_attention',
    )(ops['aiq'], ops['aiwt'], ops['aq'], ops['aik'], ops['ak'], ops['avt'])


def _moba_kernel(q_ref, qh_ref, ql_ref, kmh_ref, kml_ref, k_ref, vt_ref, o_ref, sel_ref, acc_ref,
                 *, NB, BS, n_sel, HP):
    i = pl.program_id(2)
    n_idx = lax.broadcasted_iota(i32, (NB, 1), 0)
    past = n_idx < i

    def head(h):
        return slice(h * HEAD_DIM, (h + 1) * HEAD_DIM)

    for h in range(HP):
        kmh, kml = kmh_ref[0, :, head(h)], kml_ref[0, :, head(h)]
        qh, ql = qh_ref[0, :, head(h)], ql_ref[0, :, head(h)]
        gate = _dot_nt(kmh, qh) + _dot_nt(kmh, ql) + _dot_nt(kml, qh)
        g = jnp.where(past, gate, -jnp.inf)
        sel = jnp.zeros((NB, BS), f32)
        for _ in range(n_sel):
            m = jnp.max(g, axis=0, keepdims=True)
            first = jnp.min(jnp.where(g == m, n_idx, NB), axis=0, keepdims=True)
            pick = (n_idx == first) & (m > -jnp.inf)
            sel = jnp.where(pick, 1.0, sel)
            g = jnp.where(pick, -jnp.inf, g)
        sel_ref[h] = sel

    acc_ref[...] = jnp.zeros_like(acc_ref)
    ones = jnp.ones((2 * SUBLANES, BS), bf16)

    def logits_of(n):
        ktile = k_ref[0, pl.ds(pl.multiple_of(n * BS, BS), BS), :]
        return tuple(_dot_nt(ktile[:, head(h)], q_ref[0, :, head(h)]) for h in range(HP))

    def block(n, logits, stats, step_fn):
        new_stats = []
        for h in range(HP):
            m_new, l_new, acc_new = step_fn(h, logits[h], *stats[h], acc_ref[h], vt_ref[0, n, head(h), :], ones)
            acc_ref[h] = acc_new
            new_stats.append((m_new, l_new))
        return tuple(new_stats)

    def past_block(n, logits, stats):
        return block(n, logits, stats,
                     lambda h, s, *rest: _flash_step_cols(s, sel_ref[h, pl.ds(n, 1), :] > 0.5, *rest))

    init = tuple((jnp.full((1, BS), NEG_BIG, f32), jnp.zeros((1, BS), f32)) for _ in range(HP))
    stats = _tiles_by_pairs(i, logits_of, past_block, init)
    causal = lax.broadcasted_iota(i32, (BS, BS), 0) <= lax.broadcasted_iota(i32, (BS, BS), 1)
    stats = block(i, logits_of(i), stats, lambda h, s, *rest: _flash_step(s, causal, *rest))
    for h in range(HP):
        o_ref[0, :, head(h)] = (acc_ref[h] / stats[h][1]).T.astype(o_ref.dtype)


def _moba(ops, *, HP=4):
    B, T, W = ops['dq'].shape
    BS = MOBA_BLOCK
    assert T % BS == 0 and TOK_TILE == BS
    NB = T // BS
    n_sel = min(MOBA_TOPK, NB - 1)
    WP = HP * HEAD_DIM
    kmh = ops['dkmh'].reshape(B, NB, W)
    kml = ops['dkml'].reshape(B, NB, W)
    qspec = pl.BlockSpec((1, BS, WP), lambda b, hg, i: (b, i, hg))
    kmspec = pl.BlockSpec((1, NB, WP), lambda b, hg, i: (b, 0, hg))
    return pl.pallas_call(
        functools.partial(_moba_kernel, NB=NB, BS=BS, n_sel=n_sel, HP=HP),
        grid=(B, W // WP, NB),
        in_specs=[qspec, qspec, qspec, kmspec, kmspec,
                  pl.BlockSpec((1, T, WP), lambda b, hg, i: (b, 0, hg)),
                  pl.BlockSpec((1, NB, WP, BS), lambda b, hg, i: (b, 0, hg, 0))],
        out_specs=qspec,
        out_shape=jax.ShapeDtypeStruct((B, T, W), bf16),
        scratch_shapes=[pltpu.VMEM((HP, NB, BS), f32), pltpu.VMEM((HP, HEAD_DIM, BS), f32)],
        compiler_params=_params(("parallel", "parallel", "arbitrary")),
        name='moba_attention',
    )(ops['dq'], ops['dqh'], ops['dql'], kmh, kml, ops['dk'], ops['dvt'])


def _gla_kernel(q_ref, k_ref, v_ref, glr_ref, r_ref, w2h_ref, w2l_ref, gb_ref, ng_ref, o_ref, st_ref, *, C, R):
    @pl.when(pl.program_id(1) == 0)
    def _():
        st_ref[...] = jnp.zeros_like(st_ref)

    gh, gl = _split2(glr_ref[0])
    logit = _dot(gh, w2h_ref[...]) + _dot(gh, w2l_ref[...]) + _dot(gl, w2h_ref[...]) + gb_ref[...]
    log_a = -_softplus(-logit) * (1.0 / GLA_TAU)
    tril = (lax.broadcasted_iota(i32, (C, C), 0) >= lax.broadcasted_iota(i32, (C, C), 1)).astype(bf16)
    a1, a2, a3 = _split3(log_a)
    b_all = _dot(tril, a1) + _dot(tril, a2) + _dot(tril, a3)
    heads = range(GLA_HEADS)
    kss = [slice(h * GLA_DK, (h + 1) * GLA_DK) for h in heads]
    vss = [slice(h * GLA_DV, (h + 1) * GLA_DV) for h in heads]
    b = [b_all[:, ks] for ks in kss]
    q = [q_ref[0, :, ks] * GLA_DK ** -0.5 for ks in kss]
    k = [k_ref[0, :, ks] for ks in kss]
    vb = [v_ref[0, :, vs].astype(bf16) for vs in vss]
    st = [st_ref[h] for h in heads]
    inter = [_dot_nt((q[h] * jnp.exp(b[h])).astype(bf16), st[h].astype(bf16)) for h in heads]
    att = [[None] * (C // R) for _ in heads]
    for blk in range(C // R):
        r0, r1 = blk * R, (blk + 1) * R
        keep = lax.broadcasted_iota(i32, (R, r1), 1) <= lax.broadcasted_iota(i32, (R, r1), 0) + r0
        for h in heads:
            b0 = b[h][r0 - 1:r0, :] if blk else jnp.zeros((1, GLA_DK), f32)
            qe = q[h][r0:r1] * jnp.exp(b[h][r0:r1] - b0)
            ke = k[h][:r1] * jnp.exp(b0 - b[h][:r1])
            att[h][blk] = jnp.where(keep, _dot_nt(qe.astype(bf16), ke.astype(bf16)), 0.0).astype(bf16)
    intra = [jnp.concatenate([_dot(att[h][blk], vb[h][:(blk + 1) * R]) for blk in range(C // R)], axis=0)
             for h in heads]
    for h in heads:
        b_last = b[h][C - 1:C, :]
        k_dec = k[h] * jnp.exp(b_last - b[h])
        st_ref[h] = st[h] * jnp.exp(b_last) + _dot_tn(vb[h], k_dec.astype(bf16))
    for h in heads:
        o = inter[h] + intra[h]
        y = o * lax.rsqrt(jnp.mean(o * o, axis=-1, keepdims=True) + NORM_EPS) * ng_ref[...]
        rh = r_ref[0, :, vss[h]]
        o_ref[0, :, vss[h]] = (y * (rh * _sigmoid(rh))).astype(o_ref.dtype)


def _gla(p, w2, gate_b, norm_g, *, C=128, R=32):
    B, T, _ = p.shape
    C = min(C, T)
    W = GLA_HEADS * GLA_DV
    w2p = jnp.zeros((LANES, GLA_HEADS * GLA_DK), f32).at[:GLA_GATE_RANK].set(w2)
    w2h, w2l = _split2(w2p)

    def seg(name):
        off, w = P_OFF[name]
        return pl.BlockSpec((1, C, w), lambda b, c: (b, c, off // w))

    full = lambda a: pl.BlockSpec(a.shape, lambda b, c: (0,) * a.ndim)
    gb = gate_b.reshape(1, -1)
    ng = norm_g.reshape(1, -1)
    return pl.pallas_call(
        functools.partial(_gla_kernel, C=C, R=min(R, C)),
        grid=(B, T // C),
        in_specs=[seg('b_q'), seg('b_k'), seg('b_v'), seg('b_glr'), seg('b_r'),
                  full(w2h), full(w2l), full(gb), full(ng)],
        out_specs=pl.BlockSpec((1, C, W), lambda b, c: (b, c, 0)),
        out_shape=jax.ShapeDtypeStruct((B, T, W), bf16),
        scratch_shapes=[pltpu.VMEM((GLA_HEADS, GLA_DV, GLA_DK), f32)],
        compiler_params=_params(("parallel", "arbitrary")),
        name='gla',
    )(p, p, p, p, p, w2h, w2l, gb, ng)


def _unit_lower_inverses(mats, row, col):
    C = mats[0].shape[0]
    eye = jnp.where(row == col, 1.0, 0.0)
    blk = (row >> 3) == (col >> 3)
    n1 = [jnp.where(blk, a, 0.0) for a in mats]
    n2 = [_mm3(x, x) for x in n1]
    n4 = [_mm3(x, x) for x in n2]
    t = [_mm3(eye - x1, eye + x2) for x1, x2 in zip(n1, n2)]
    t = [_mm3(x, eye + x4) for x, x4 in zip(t, n4)]
    s = SUBLANES
    while s < C:
        sh = s.bit_length() - 1
        sel = ((row >> (sh + 1)) == (col >> (sh + 1))) & ((row >> sh) != (col >> sh))
        left = [_mm1(x, jnp.where(sel, a, 0.0)) for x, a in zip(t, mats)]
        t = [x - _mm1(y, x) for x, y in zip(t, left)]
        s *= 2
    return t


def _gdn_kernel(q_ref, k_ref, v_ref, ab_ref, gate_ref, alog_ref, dtb_ref, ng_ref, o_ref, s_ref, *, C):
    @pl.when(pl.program_id(1) == 0)
    def _():
        s_ref[...] = jnp.zeros_like(s_ref)

    H = GDN_HEADS
    row = lax.broadcasted_iota(i32, (C, C), 0)
    col = lax.broadcasted_iota(i32, (C, C), 1)
    ab = ab_ref[0]
    g_all = -jnp.exp(alog_ref[...]) * _softplus(ab + dtb_ref[...])
    beta_all = _sigmoid(ab)
    tril = (row >= col).astype(bf16)
    g1, g2, g3 = _split3(g_all)
    gam_all = _dot(tril, g1) + _dot(tril, g2) + _dot(tril, g3)
    gam_t = gam_all.T
    heads = range(H)
    sls = [slice(h * HEAD_DIM, (h + 1) * HEAD_DIM) for h in heads]
    q = [q_ref[0, :, sl] for sl in sls]
    k = [k_ref[0, :, sl] for sl in sls]
    kb = [x.astype(bf16) for x in k]
    gcol = [gam_all[:, h:h + 1] for h in heads]
    bcol = [beta_all[:, H + h:H + h + 1] for h in heads]
    egam = [jnp.exp(x) for x in gcol]
    dec = [jnp.where(row >= col, jnp.exp(jnp.minimum(gcol[h] - gam_t[h:h + 1, :], 0.0)), 0.0) for h in heads]
    kk = [_dot_nt(x, x) for x in kb]
    qk = [_dot_nt(q[h].astype(bf16), kb[h]) for h in heads]
    a = [jnp.where(row > col, bcol[h] * dec[h] * kk[h], 0.0) for h in heads]
    tinv = _unit_lower_inverses(a, row, col)
    sol = [_mm1(tinv[h], jnp.concatenate([bcol[h] * v_ref[0, :, sls[h]], (bcol[h] * egam[h]) * k[h]], axis=1))
           for h in heads]
    sb = [s_ref[h].astype(bf16) for h in heads]
    ws = [_dot(sol[h][:, HEAD_DIM:].astype(bf16), sb[h]) for h in heads]
    qs = [_dot((q[h] * egam[h]).astype(bf16), sb[h]) for h in heads]
    db = [(sol[h][:, :HEAD_DIM] - ws[h]).astype(bf16) for h in heads]
    o = [qs[h] + _dot((qk[h] * dec[h]).astype(bf16), db[h]) for h in heads]
    for h in heads:
        g_last = gcol[h][C - 1:C, :]
        k_dec = k[h] * jnp.exp(g_last - gcol[h])
        s_ref[h] = jnp.exp(g_last) * s_ref[h] + _dot_tn(k_dec.astype(bf16), db[h])
    for h in heads:
        y = o[h] * lax.rsqrt(jnp.mean(o[h] * o[h], axis=-1, keepdims=True) + NORM_EPS) * ng_ref[...]
        gt = gate_ref[0, :, sls[h]]
        o_ref[0, :, sls[h]] = (y * (gt * _sigmoid(gt))).astype(o_ref.dtype)


def _gdn(cq, ck, cv, p, a_log, dt_bias, norm_g, *, C=128):
    B, T, W = cq.shape
    C = min(C, T)
    H = GDN_HEADS
    alog = jnp.zeros((1, LANES), f32).at[0, :H].set(a_log)
    dtb = jnp.zeros((1, LANES), f32).at[0, :H].set(dt_bias)
    ng = norm_g.reshape(1, -1)
    tok = pl.BlockSpec((1, C, W), lambda b, c: (b, c, 0))

    def seg(name):
        off, w = P_OFF[name]
        return pl.BlockSpec((1, C, w), lambda b, c: (b, c, off // w))

    full = lambda a: pl.BlockSpec(a.shape, lambda b, c: (0,) * a.ndim)
    return pl.pallas_call(
        functools.partial(_gdn_kernel, C=C),
        grid=(B, T // C),
        in_specs=[tok, tok, tok, seg('c_ab'), seg('c_g'), full(alog), full(dtb), full(ng)],
        out_specs=tok,
        out_shape=jax.ShapeDtypeStruct((B, T, W), bf16),
        scratch_shapes=[pltpu.VMEM((H, HEAD_DIM, HEAD_DIM), f32)],
        compiler_params=_params(("parallel", "arbitrary")),
        name='gated_delta_rule',
    )(cq, ck, cv, p, p, alog, dtb, ng)


def _pack_kernel(src_ref, valid_ref, w_ref, o_ref):
    rows = lax.broadcasted_iota(i32, (LANES, 1), 0)
    keep = rows < valid_ref[pl.program_id(0)]
    for l in range(o_ref.shape[0]):
        o_ref[l] = jnp.where(keep, w_ref[:, l, :], 0.0).astype(bf16)


def _pack_w_in(w_in):
    L, D, d_in = w_in.shape
    wt = jnp.transpose(w_in, (2, 0, 1))
    src, valid = [], []
    for name, width in P_LAYOUT:
        s0, w = (REF_OFF['c_a'][0], 2 * GDN_HEADS) if name == 'c_ab' else REF_OFF[name]
        for r in range(0, width, LANES):
            assert s0 + r + LANES <= d_in
            src.append(s0 + r)
            valid.append(min(max(w - r, 0), LANES))
    grid_spec = pltpu.PrefetchScalarGridSpec(
        num_scalar_prefetch=2,
        grid=(P_WIDTH // LANES,),
        in_specs=[pl.BlockSpec((pl.Element(LANES), pl.Element(L), pl.Element(D)),
                               lambda j, src, valid: (src[j], 0, 0))],
        out_specs=pl.BlockSpec((L, LANES, D), lambda j, src, valid: (0, j, 0)),
    )
    return pl.pallas_call(
        _pack_kernel,
        grid_spec=grid_spec,
        out_shape=jax.ShapeDtypeStruct((L, P_WIDTH, D), bf16),
        compiler_params=_params(("arbitrary",)),
        name='pack_w_in',
    )(jnp.asarray(src, i32), jnp.asarray(valid, i32), wt)


def _token_mixers(hb, B, T, layer, w_in, gla_gate_w2, gla_gate_b, gla_norm_g, gdn_conv_w, gdn_a_log, gdn_dt_bias,
                  gdn_norm_g, cosf, sinf):
    p = _matmul(hb, w_in, layer, tm=1024, tn=512, tk=hb.shape[1], out_dtype=f32, b_transposed=True,
                name='in_proj')
    p = p.reshape(B, T, P_WIDTH)
    ops = _prep(p, cosf, sinf, gdn_conv_w)
    y_a = _dsa(ops)
    y_b = _gla(p, gla_gate_w2, gla_gate_b, gla_norm_g)
    y_c = _gdn(ops['cq'], ops['ck'], ops['cv'], p, gdn_a_log, gdn_dt_bias, gdn_norm_g)
    y_d = _moba(ops)
    return tuple(y.reshape(B * T, -1) for y in (y_a, y_b, y_c, y_d))


def _rope_tables(T):
    inv = ROPE_THETA ** (-jnp.arange(0, HEAD_DIM, 2, dtype=f32) / HEAD_DIM)
    ang = jnp.arange(T, dtype=f32)[:, None] * inv[None, :]
    cos, sin = jnp.cos(ang), jnp.sin(ang)
    return jnp.concatenate([cos, cos], axis=1), jnp.concatenate([-sin, sin], axis=1)


def kernel(x, ln_in_g, ln_in_b, w_in, w_out, ln1_g, ln1_b, gla_gate_w2, gla_gate_b, gla_norm_g, gdn_conv_w,
           gdn_a_log, gdn_dt_bias, gdn_norm_g, w_up, w_down, ln2_g, ln2_b):
    B, T, D = x.shape
    depth = w_in.shape[0]
    alpha = (2 * depth) ** 0.25
    cosf, sinf = _rope_tables(T)
    h, hb = _layer_norm(x.reshape(B * T, D), None, ln_in_g, ln_in_b)
    w_in_p = _pack_w_in(w_in)
    w_out_b, w_up_b, w_down_b = w_out.astype(bf16), w_up.astype(bf16), w_down.astype(bf16)
    for l in range(depth):
        mix = _token_mixers(hb, B, T, l, w_in_p, gla_gate_w2[l], gla_gate_b[l], gla_norm_g[l], gdn_conv_w[l],
                            gdn_a_log[l], gdn_dt_bias[l], gdn_norm_g[l], cosf, sinf)
        y = _matmul(mix, w_out_b, l, tm=1024, tn=1024, tk=w_out.shape[1], out_dtype=f32, name='out_proj')
        h, hb = _layer_norm(h, y, ln1_g[l], ln1_b[l], alpha=alpha)
        up = _matmul(hb, w_up_b, l, tm=1024, tn=1024, tk=D, out_dtype=bf16, act='relu2', name='mlp_up')
        ff = _matmul(up, w_down_b, l, tm=1024, tn=1024, tk=2048, out_dtype=f32, name='mlp_down')
        h, hb = _layer_norm(h, ff, ln2_g[l], ln2_b[l], alpha=alpha)
    return h.reshape(B, T, D)
```

```python
import functools
import math

import jax
import jax.numpy as jnp
from jax import lax
from jax.experimental import pallas as pl
from jax.experimental.pallas import tpu as pltpu

f32 = jnp.float32
bf16 = jnp.bfloat16
i32 = jnp.int32

HEAD_DIM = 128
A_HEADS = 8
A_KV_HEADS = 2
IDX_HEADS = 4
IDX_DIM = 128
DSA_TOPK = 256
GLA_HEADS = 4
GLA_DK = 128
GLA_DV = 256
GLA_GATE_RANK = 16
GLA_TAU = 16.0
GDN_HEADS = 8
CONV_K = 4
MOBA_HEADS = 8
MOBA_BLOCK = 256
MOBA_TOPK = 3
ROPE_THETA = 10000.0
LN_EPS = 1e-5
NORM_EPS = 1e-6

V7X_VMEM_BYTES = 64 * 2**20
VMEM_LIMIT = V7X_VMEM_BYTES * 3 // 4
LANES = 128
SUBLANES = 8

INT_MIN = -2**31
NEG_BIG = -1e30
LOG2E = math.log2(math.e)
ATTN_Q_SCALE = HEAD_DIM ** -0.5 * LOG2E

TOK_TILE = MOBA_BLOCK

P_LAYOUT = (
    ('a_q', 1024), ('b_v', 1024), ('b_r', 1024), ('c_q', 1024), ('c_k', 1024), ('c_v', 1024), ('c_g', 1024),
    ('d_q', 1024), ('d_k', 1024), ('d_v', 1024),
    ('a_iq', 512), ('b_q', 512), ('b_k', 512),
    ('a_k', 256), ('a_v', 256),
    ('a_ik', 128), ('a_iw', 128), ('b_glr', 128), ('c_ab', 128),
)
P_OFF = {}
_o = 0
for _n, _w in P_LAYOUT:
    P_OFF[_n] = (_o, _w)
    _o += _w
P_WIDTH = _o

REF_SPLITS = (
    ('a_q', 1024), ('a_k', 256), ('a_v', 256), ('a_iq', 512), ('a_ik', 128), ('a_iw', 4),
    ('b_q', 512), ('b_k', 512), ('b_v', 1024), ('b_glr', 16), ('b_r', 1024),
    ('c_q', 1024), ('c_k', 1024), ('c_v', 1024), ('c_a', 8), ('c_b', 8), ('c_g', 1024),
    ('d_q', 1024), ('d_k', 1024), ('d_v', 1024),
)
REF_OFF = {}
_o = 0
for _n, _w in REF_SPLITS:
    REF_OFF[_n] = (_o, _w)
    _o += _w


def _dot(a, b):
    return jnp.dot(a, b, preferred_element_type=f32)


def _dot_nt(a, b):
    return lax.dot_general(a, b, (((1,), (1,)), ((), ())), preferred_element_type=f32)


def _dot_tn(a, b):
    return lax.dot_general(a, b, (((0,), (0,)), ((), ())), preferred_element_type=f32)


def _split2(x):
    hi = x.astype(bf16)
    lo = (x - hi.astype(f32)).astype(bf16)
    return hi, lo


def _split3(x):
    hi = x.astype(bf16)
    r = x - hi.astype(f32)
    mid = r.astype(bf16)
    lo = (r - mid.astype(f32)).astype(bf16)
    return hi, mid, lo


def _mm1(x, y):
    return _dot(x.astype(bf16), y.astype(bf16))


def _mm3(x, y):
    xh, xl = _split2(x)
    yh, yl = _split2(y)
    return _dot(xh, yh) + _dot(xh, yl) + _dot(xl, yh)


def _sigmoid(x):
    return 1.0 / (1.0 + jnp.exp(-x))


def _softplus(x):
    return jnp.maximum(x, 0.0) + jnp.log1p(jnp.exp(-jnp.abs(x)))


def _params(sem):
    return pltpu.CompilerParams(dimension_semantics=sem, vmem_limit_bytes=VMEM_LIMIT)


def _mm_kernel(*refs, n_a, nk, act, b_transposed, side_cast):
    a_refs, b_ref = refs[:n_a], refs[n_a]
    if side_cast:
        c_ref, o_ref, co_ref = refs[n_a + 1:n_a + 4]
        co_ref[...] = c_ref[...].astype(bf16)
        scratch = refs[n_a + 4:]
    else:
        o_ref = refs[n_a + 1]
        scratch = refs[n_a + 2:]

    def product():
        if b_transposed:
            return _dot_nt(a_refs[0][...], b_ref[...])
        if n_a == 1:
            return _dot(a_refs[0][...], b_ref[...])
        kw = b_ref.shape[0] // n_a
        out = _dot(a_refs[0][...], b_ref[0:kw, :])
        for g in range(1, n_a):
            out = out + _dot(a_refs[g][...], b_ref[g * kw:(g + 1) * kw, :])
        return out

    def finish(r):
        if act == 'relu2':
            r = jnp.square(jnp.maximum(r, 0.0))
        o_ref[...] = r.astype(o_ref.dtype)

    if nk == 1:
        finish(product())
        return
    acc_ref, = scratch
    k = pl.program_id(2)

    @pl.when(k == 0)
    def _():
        acc_ref[...] = jnp.zeros_like(acc_ref)

    acc_ref[...] += product()

    @pl.when(k == nk - 1)
    def _():
        finish(acc_ref[...])


def _matmul(a, b, layer, *, tm, tn, tk, out_dtype, act=None, b_transposed=False, cast=None, name='matmul'):
    a_list = a if isinstance(a, (tuple, list)) else (a,)
    n_a = len(a_list)
    M = a_list[0].shape[0]
    _, K, N = b.shape
    if b_transposed:
        assert n_a == 1
        K, N = N, K
    tm, tn, tk = min(tm, M), min(tn, N), min(tk, K)
    assert M % tm == 0 and N % tn == 0 and K % tk == 0
    nk = K // tk
    assert n_a == 1 or nk == 1
    ka = tk // n_a
    gj = N // tn
    b_spec = (pl.BlockSpec((None, tn, tk), lambda i, j, k: (layer, j, k)) if b_transposed else
              pl.BlockSpec((None, tk, tn), lambda i, j, k: (layer, k, j)))
    in_specs = [pl.BlockSpec((tm, ka), lambda i, j, k: (i, k))] * n_a + [b_spec]
    out_specs = [pl.BlockSpec((tm, tn), lambda i, j, k: (i, j))]
    out_shape = [jax.ShapeDtypeStruct((M, N), out_dtype)]
    args = list(a_list) + [b]
    if cast is not None:
        w, wl = cast
        _, R, C = w.shape
        steps = (M // tm) * gj * nk
        assert R % steps == 0 and (R // steps) % (2 * SUBLANES) == 0
        rc = R // steps
        in_specs.append(pl.BlockSpec((None, rc, C), lambda i, j, k: (wl, (i * gj + j) * nk + k, 0)))
        out_specs.append(pl.BlockSpec((None, rc, C), lambda i, j, k: (0, (i * gj + j) * nk + k, 0)))
        out_shape.append(jax.ShapeDtypeStruct((1, R, C), bf16))
        args.append(w)
    res = pl.pallas_call(
        functools.partial(_mm_kernel, n_a=n_a, nk=nk, act=act, b_transposed=b_transposed,
                          side_cast=cast is not None),
        grid=(M // tm, gj, nk),
        in_specs=in_specs,
        out_specs=out_specs,
        out_shape=out_shape,
        scratch_shapes=[pltpu.VMEM((tm, tn), f32)] if nk > 1 else [],
        compiler_params=_params(("parallel", "parallel", "arbitrary")),
        name=name,
    )(*args)
    return res if cast is not None else res[0]


def _ln_kernel(*refs, alpha, has_y):
    if has_y:
        x_ref, y_ref, g_ref, b_ref, o_ref, ob_ref = refs
        z = x_ref[...] * alpha + y_ref[...]
    else:
        x_ref, g_ref, b_ref, o_ref, ob_ref = refs
        z = x_ref[...]
    mu = jnp.mean(z, axis=-1, keepdims=True)
    zc = z - mu
    var = jnp.mean(zc * zc, axis=-1, keepdims=True)
    out = zc * lax.rsqrt(var + LN_EPS) * g_ref[...] + b_ref[...]
    o_ref[...] = out
    ob_ref[...] = out.astype(bf16)


def _layer_norm(x, y, g, b, *, alpha=1.0, tm=256):
    M, D = x.shape
    tm = min(tm, M)
    row = pl.BlockSpec((tm, D), lambda i: (i, 0))
    vec = pl.BlockSpec((1, D), lambda i: (0, 0))
    has_y = y is not None
    args = (x, y) if has_y else (x,)
    return pl.pallas_call(
        functools.partial(_ln_kernel, alpha=alpha, has_y=has_y),
        grid=(M // tm,),
        in_specs=[row] * len(args) + [vec, vec],
        out_specs=[row, row],
        out_shape=[jax.ShapeDtypeStruct((M, D), f32), jax.ShapeDtypeStruct((M, D), bf16)],
        compiler_params=_params(("parallel",)),
        name='layer_norm',
    )(*args, g.reshape(1, D), b.reshape(1, D))


def _prep_kernel(aq_ref, ak_ref, av_ref, aiq_ref, aik_ref, aiw_ref,
                 cq_ref, ck_ref, cv_ref, cqh_ref, ckh_ref, cvh_ref,
                 dq_ref, dk_ref, dv_ref, cos_ref, sin_ref, cw_ref,
                 oaq, oak, oavt, oaiq, oaik, oaiwt, ocq, ock, ocv,
                 odq, odqh, odql, odk, odvt, odkmh, odkml, *, TT):
    cosf, sinf = cos_ref[...], sin_ref[...]

    def rope(x):
        return x * cosf + pltpu.roll(x, HEAD_DIM // 2, 1) * sinf

    def head(h):
        return slice(h * HEAD_DIM, (h + 1) * HEAD_DIM)

    for h in range(A_HEADS):
        oaq[0, :, head(h)] = (rope(aq_ref[0, :, head(h)]) * ATTN_Q_SCALE).astype(bf16)
    for g in range(A_KV_HEADS):
        oak[0, :, head(g)] = rope(ak_ref[0, :, head(g)]).astype(bf16)
        oavt[0, 0, head(g), :] = av_ref[0, :, head(g)].T.astype(bf16)
    for h in range(IDX_HEADS):
        oaiq[0, :, head(h)] = rope(aiq_ref[0, :, head(h)]).astype(bf16)
    oaik[0] = rope(aik_ref[0]).astype(bf16)
    oaiwt[0] = aiw_ref[0].T[:SUBLANES, :] * (IDX_HEADS ** -0.5 * IDX_DIM ** -0.5)

    first = pl.program_id(1) == 0
    for j, (x_ref, h_ref, o_ref) in enumerate(((cq_ref, cqh_ref, ocq), (ck_ref, ckh_ref, ock),
                                                (cv_ref, cvh_ref, ocv))):
        width = x_ref.shape[2]
        halo = jnp.where(first, 0.0, h_ref[0])
        cat = jnp.concatenate([halo, x_ref[0]], axis=0)
        w = cw_ref[:, j * width:(j + 1) * width]
        y = cat[SUBLANES - CONV_K + 1:SUBLANES - CONV_K + 1 + TT] * w[0:1]
        for i in range(1, CONV_K):
            s0 = SUBLANES - CONV_K + 1 + i
            y = y + cat[s0:s0 + TT] * w[i:i + 1]
        y = y * _sigmoid(y)
        if j == 2:
            o_ref[0] = y
        else:
            post = HEAD_DIM ** -0.5 if j == 0 else 1.0
            for h in range(GDN_HEADS):
                yh = y[:, head(h)]
                o_ref[0, :, head(h)] = yh * (lax.rsqrt(jnp.sum(yh * yh, axis=-1, keepdims=True) + NORM_EPS) * post)

    for h in range(MOBA_HEADS):
        qr = rope(dq_ref[0, :, head(h)])
        odq[0, :, head(h)] = (qr * ATTN_Q_SCALE).astype(bf16)
        qh, ql = _split2(qr)
        odqh[0, :, head(h)] = qh
        odql[0, :, head(h)] = ql
        kr = rope(dk_ref[0, :, head(h)])
        odk[0, :, head(h)] = kr.astype(bf16)
        kmh, kml = _split2(jnp.mean(kr, axis=0, keepdims=True))
        odkmh[0, 0, :, head(h)] = kmh
        odkml[0, 0, :, head(h)] = kml
        odvt[0, 0, head(h), :] = dv_ref[0, :, head(h)].T.astype(bf16)


def _prep(p, cosf, sinf, conv_w):
    B, T, _ = p.shape
    TT = min(TOK_TILE, T)
    NT = T // TT

    def seg(name, rows=TT):
        off, w = P_OFF[name]
        return pl.BlockSpec((1, rows, w), lambda b, i: (b, i, off // w))

    def halo(name):
        off, w = P_OFF[name]
        return pl.BlockSpec((1, SUBLANES, w),
                            lambda b, i: (b, jnp.maximum(i * (TT // SUBLANES) - 1, 0), off // w))

    in_names = ('a_q', 'a_k', 'a_v', 'a_iq', 'a_ik', 'a_iw', 'c_q', 'c_k', 'c_v')
    in_specs = ([seg(n) for n in in_names] + [halo(n) for n in ('c_q', 'c_k', 'c_v')]
                + [seg(n) for n in ('d_q', 'd_k', 'd_v')]
                + [pl.BlockSpec((TT, HEAD_DIM), lambda b, i: (i, 0))] * 2
                + [pl.BlockSpec(conv_w.shape, lambda b, i: (0, 0))])

    def tok(w, dt):
        return pl.BlockSpec((1, TT, w), lambda b, i: (b, i, 0)), jax.ShapeDtypeStruct((B, T, w), dt)

    def tposed(rows, dt):
        return (pl.BlockSpec((1, 1, rows, TT), lambda b, i: (b, i, 0, 0)),
                jax.ShapeDtypeStruct((B, NT, rows, TT), dt))

    def per_tile(w, dt):
        return (pl.BlockSpec((1, 1, 1, w), lambda b, i: (b, i, 0, 0)),
                jax.ShapeDtypeStruct((B, NT, 1, w), dt))

    outs = [
        tok(1024, bf16), tok(256, bf16), tposed(256, bf16), tok(512, bf16), tok(128, bf16),
        (pl.BlockSpec((1, SUBLANES, TT), lambda b, i: (b, 0, i)), jax.ShapeDtypeStruct((B, SUBLANES, T), f32)),
        tok(1024, f32), tok(1024, f32), tok(1024, f32),
        tok(1024, bf16), tok(1024, bf16), tok(1024, bf16), tok(1024, bf16), tposed(1024, bf16),
        per_tile(1024, bf16), per_tile(1024, bf16),
    ]
    res = pl.pallas_call(
        functools.partial(_prep_kernel, TT=TT),
        grid=(B, NT),
        in_specs=in_specs,
        out_specs=[o[0] for o in outs],
        out_shape=[o[1] for o in outs],
        compiler_params=_params(("parallel", "arbitrary")),
        name='mixer_prep',
    )(*([p] * 15), cosf, sinf, conv_w)
    names = ('aq', 'ak', 'avt', 'aiq', 'aik', 'aiwt', 'cq', 'ck', 'cv', 'dq', 'dqh', 'dql', 'dk', 'dvt',
             'dkmh', 'dkml')
    return dict(zip(names, res))


def _tiles_by_pairs(n, produce, consume, state):
    odd = n % 2

    def single(j, st):
        return consume(j, produce(j), st)

    def pair(m, st):
        j = odd + 2 * m
        first, second = produce(j), produce(j + 1)
        return consume(j + 1, second, consume(j, first, st))

    state = lax.fori_loop(0, odd, single, state)
    return lax.fori_loop(0, n // 2, pair, state)


def _flash_update(p, m_prev, m_new, l_prev, acc_prev, vt, ones):
    pb = p.astype(bf16)
    a = jnp.exp2(m_prev - m_new)
    l_new = a * l_prev + _dot(ones, pb)[0:1]
    acc_new = a * acc_prev + _dot(vt, pb)
    return m_new, l_new, acc_new


def _flash_step(s, mask, m_prev, l_prev, acc_prev, vt, ones):
    s = jnp.where(mask, s, NEG_BIG)
    m_new = jnp.maximum(m_prev, jnp.max(s, axis=0, keepdims=True))
    return _flash_update(jnp.exp2(s - m_new), m_prev, m_new, l_prev, acc_prev, vt, ones)


def _flash_step_cols(s, cols, m_prev, l_prev, acc_prev, vt, ones):
    m_new = jnp.where(cols, jnp.maximum(m_prev, jnp.max(s, axis=0, keepdims=True)), m_prev)
    p = jnp.exp2(s - jnp.where(cols, m_new, -NEG_BIG))
    return _flash_update(p, m_prev, m_new, l_prev, acc_prev, vt, ones)


def _dsa_kernel(iq_ref, iw_ref, q_ref, ik_ref, k_ref, vt_ref, o_ref, keys_ref, gmax_ref, acc_ref, *, TT, topk):
    i = pl.program_id(1)
    n_kt = i + 1
    t_idx = i * TT + lax.broadcasted_iota(i32, (1, TT), 1)
    row = lax.broadcasted_iota(i32, (TT, 1), 0)
    iw = iw_ref[0]
    iq = iq_ref[0]

    def tile(kt):
        return pl.ds(pl.multiple_of(kt * TT, TT), TT)

    def head_scores(kt):
        ik = ik_ref[0, tile(kt), :]
        return [_dot_nt(ik, iq[:, h * IDX_DIM:(h + 1) * IDX_DIM]) for h in range(IDX_HEADS)]

    def store_keys(kt, xs, carry):
        acc = jnp.zeros((TT, TT), f32)
        for h in range(IDX_HEADS):
            acc = acc + iw[h:h + 1, :] * jnp.maximum(xs[h], 0.0)
        bits = lax.bitcast_convert_type(acc, i32)
        key = bits ^ ((bits >> 31) & jnp.int32(0x7FFFFFFF))
        key = jnp.where(key == -1, 0, key)
        key = jnp.where(kt * TT + row <= t_idx, key, jnp.int32(INT_MIN))
        keys_ref[tile(kt), :] = key
        gmax_ref[...] = jnp.maximum(gmax_ref[...], key)
        return carry

    gmax_ref[...] = jnp.full((TT, TT), INT_MIN, i32)
    _tiles_by_pairs(n_kt, head_scores, store_keys, 0)

    def count(pred_fn):
        def body(kt, c):
            ind = pred_fn(keys_ref[tile(kt), :]).astype(i32)
            return c + jnp.sum(ind.reshape(TT // SUBLANES, SUBLANES, TT), axis=0)
        c = lax.fori_loop(0, n_kt, body, jnp.zeros((SUBLANES, TT), i32))
        return jnp.sum(c, axis=0, keepdims=True)

    n_nonneg = count(lambda key: key >= 0)
    nonneg = n_nonneg >= topk
    takes_all = t_idx + 1 <= topk
    zero_thr = nonneg & (count(lambda key: key > 0) < topk)
    gmax = gmax_ref[...]
    lo = jnp.min(gmax, axis=0, keepdims=True)
    hi = jnp.max(gmax, axis=0, keepdims=True) + 1
    cnt_lo = jnp.where(nonneg & (lo <= 0), n_nonneg, -1)
    lo, hi = jnp.where(nonneg, jnp.maximum(lo, 0), lo), jnp.where(nonneg, hi, jnp.minimum(hi, 0))

    def pending(lo, hi, cnt_lo):
        settled = takes_all | zero_thr | (cnt_lo == topk) | (hi - lo == 1)
        return jnp.max(jnp.where(settled, 0, 1))

    def bisect(state):
        it, lo, hi, cnt_lo, _ = state
        mid = lo + lax.shift_right_logical(hi - lo, 1)
        cnt = count(lambda key: key >= mid)
        ok = cnt >= topk
        lo, hi, cnt_lo = jnp.where(ok, mid, lo), jnp.where(ok, hi, mid), jnp.where(ok, cnt, cnt_lo)
        return it + 1, lo, hi, cnt_lo, pending(lo, hi, cnt_lo)

    state = (jnp.int32(0), lo, hi, cnt_lo, pending(lo, hi, cnt_lo))
    _, lo, _, _, _ = lax.while_loop(lambda st: (st[0] < 33) & (st[4] > 0), bisect, state)
    thr = jnp.where(takes_all, jnp.int32(INT_MIN), jnp.where(zero_thr, 0, lo))
    need = (topk - count(lambda key: key > thr)).astype(f32)

    tri = (lax.broadcasted_iota(i32, (TT, TT), 0) > lax.broadcasted_iota(i32, (TT, TT), 1)).astype(bf16)
    ones = jnp.ones((2 * SUBLANES, TT), bf16)
    group = A_HEADS // A_KV_HEADS
    acc_ref[...] = jnp.zeros_like(acc_ref)

    def attend_tile(kt, carry):
        tie_carry, stats = carry
        key = keys_ref[tile(kt), :]
        eq = (key == thr) & (key != INT_MIN)
        eq_f = jnp.where(eq, 1.0, 0.0)
        tie_rank = _dot(tri, eq_f.astype(bf16)) + tie_carry
        mask = (key > thr) | (eq & (tie_rank < need))
        ktile = k_ref[0, tile(kt), :]
        logits = [_dot_nt(ktile[:, (h // group) * HEAD_DIM:(h // group + 1) * HEAD_DIM],
                          q_ref[0, :, h * HEAD_DIM:(h + 1) * HEAD_DIM]) for h in range(A_HEADS)]
        new_stats = []
        for h in range(A_HEADS):
            g = h // group
            m_new, l_new, acc_new = _flash_step(logits[h], mask, *stats[h], acc_ref[h],
                                                vt_ref[0, kt, g * HEAD_DIM:(g + 1) * HEAD_DIM, :], ones)
            acc_ref[h] = acc_new
            new_stats.append((m_new, l_new))
        return tie_carry + jnp.sum(eq_f, axis=0, keepdims=True), tuple(new_stats)

    init = (jnp.zeros((1, TT), f32),
            tuple((jnp.full((1, TT), NEG_BIG, f32), jnp.zeros((1, TT), f32)) for _ in range(A_HEADS)))
    _, stats = lax.fori_loop(0, n_kt, attend_tile, init)
    for h in range(A_HEADS):
        o_ref[0, :, h * HEAD_DIM:(h + 1) * HEAD_DIM] = (acc_ref[h] / stats[h][1]).T.astype(o_ref.dtype)


def _dsa(ops):
    B, T, W = ops['aq'].shape
    TT = min(TOK_TILE, T)
    NT = T // TT
    topk = min(DSA_TOPK, T // 4)
    assert topk <= TT
    return pl.pallas_call(
        functools.partial(_dsa_kernel, TT=TT, topk=topk),
        grid=(B, NT),
        in_specs=[
            pl.BlockSpec((1, TT, IDX_HEADS * IDX_DIM), lambda b, i: (b, i, 0)),
            pl.BlockSpec((1, SUBLANES, TT), lambda b, i: (b, 0, i)),
            pl.BlockSpec((1, TT, W), lambda b, i: (b, i, 0)),
            pl.BlockSpec((1, T, IDX_DIM), lambda b, i: (b, 0, 0)),
            pl.BlockSpec((1, T, A_KV_HEADS * HEAD_DIM), lambda b, i: (b, 0, 0)),
            pl.BlockSpec((1, NT, A_KV_HEADS * HEAD_DIM, TT), lambda b, i: (b, 0, 0, 0)),
        ],
        out_specs=pl.BlockSpec((1, TT, W), lambda b, i: (b, i, 0)),
        out_shape=jax.ShapeDtypeStruct((B, T, W), bf16),
        scratch_shapes=[pltpu.VMEM((T, TT), i32), pltpu.VMEM((TT, TT), i32),
                        pltpu.VMEM((A_HEADS, HEAD_DIM, TT), f32)],
        compiler_params=_params(("parallel", "arbitrary")),
        name='dsa_attention',
    )(ops['aiq'], ops['aiwt'], ops['aq'], ops['aik'], ops['ak'], ops['avt'])


def _moba_kernel(q_ref, qh_ref, ql_ref, kmh_ref, kml_ref, k_ref, vt_ref, o_ref, sel_ref, acc_ref,
                 *, NB, BS, n_sel, HP):
    i = pl.program_id(2)
    n_idx = lax.broadcasted_iota(i32, (NB, 1), 0)
    past = n_idx < i

    def head(h):
        return slice(h * HEAD_DIM, (h + 1) * HEAD_DIM)

    for h in range(HP):
        kmh, kml = kmh_ref[0, :, head(h)], kml_ref[0, :, head(h)]
        qh, ql = qh_ref[0, :, head(h)], ql_ref[0, :, head(h)]
        gate = _dot_nt(kmh, qh) + _dot_nt(kmh, ql) + _dot_nt(kml, qh)
        g = jnp.where(past, gate, -jnp.inf)
        sel = jnp.zeros((NB, BS), f32)
        for _ in range(n_sel):
            m = jnp.max(g, axis=0, keepdims=True)
            first = jnp.min(jnp.where(g == m, n_idx, NB), axis=0, keepdims=True)
            pick = (n_idx == first) & (m > -jnp.inf)
            sel = jnp.where(pick, 1.0, sel)
            g = jnp.where(pick, -jnp.inf, g)
        sel_ref[h] = sel

    acc_ref[...] = jnp.zeros_like(acc_ref)
    ones = jnp.ones((2 * SUBLANES, BS), bf16)

    def logits_of(n):
        ktile = k_ref[0, pl.ds(pl.multiple_of(n * BS, BS), BS), :]
        return tuple(_dot_nt(ktile[:, head(h)], q_ref[0, :, head(h)]) for h in range(HP))

    def block(n, logits, stats, step_fn):
        new_stats = []
        for h in range(HP):
            m_new, l_new, acc_new = step_fn(h, logits[h], *stats[h], acc_ref[h], vt_ref[0, n, head(h), :], ones)
            acc_ref[h] = acc_new
            new_stats.append((m_new, l_new))
        return tuple(new_stats)

    def past_block(n, logits, stats):
        return block(n, logits, stats,
                     lambda h, s, *rest: _flash_step_cols(s, sel_ref[h, pl.ds(n, 1), :] > 0.5, *rest))

    init = tuple((jnp.full((1, BS), NEG_BIG, f32), jnp.zeros((1, BS), f32)) for _ in range(HP))
    stats = _tiles_by_pairs(i, logits_of, past_block, init)
    causal = lax.broadcasted_iota(i32, (BS, BS), 0) <= lax.broadcasted_iota(i32, (BS, BS), 1)
    stats = block(i, logits_of(i), stats, lambda h, s, *rest: _flash_step(s, causal, *rest))
    for h in range(HP):
        o_ref[0, :, head(h)] = (acc_ref[h] / stats[h][1]).T.astype(o_ref.dtype)


def _moba(ops, *, HP=4):
    B, T, W = ops['dq'].shape
    BS = MOBA_BLOCK
    assert T % BS == 0 and TOK_TILE == BS
    NB = T // BS
    n_sel = min(MOBA_TOPK, NB - 1)
    WP = HP * HEAD_DIM
    kmh = ops['dkmh'].reshape(B, NB, W)
    kml = ops['dkml'].reshape(B, NB, W)
    qspec = pl.BlockSpec((1, BS, WP), lambda b, hg, i: (b, i, hg))
    kmspec = pl.BlockSpec((1, NB, WP), lambda b, hg, i: (b, 0, hg))
    return pl.pallas_call(
        functools.partial(_moba_kernel, NB=NB, BS=BS, n_sel=n_sel, HP=HP),
        grid=(B, W // WP, NB),
        in_specs=[qspec, qspec, qspec, kmspec, kmspec,
                  pl.BlockSpec((1, T, WP), lambda b, hg, i: (b, 0, hg)),
                  pl.BlockSpec((1, NB, WP, BS), lambda b, hg, i: (b, 0, hg, 0))],
        out_specs=qspec,
        out_shape=jax.ShapeDtypeStruct((B, T, W), bf16),
        scratch_shapes=[pltpu.VMEM((HP, NB, BS), f32), pltpu.VMEM((HP, HEAD_DIM, BS), f32)],
        compiler_params=_params(("parallel", "parallel", "arbitrary")),
        name='moba_attention',
    )(ops['dq'], ops['dqh'], ops['dql'], kmh, kml, ops['dk'], ops['dvt'])


def _gla_kernel(q_ref, k_ref, v_ref, glr_ref, r_ref, w2h_ref, w2l_ref, gb_ref, ng_ref, o_ref, st_ref, *, C, R):
    @pl.when(pl.program_id(1) == 0)
    def _():
        st_ref[...] = jnp.zeros_like(st_ref)

    gh, gl = _split2(glr_ref[0])
    logit = _dot(gh, w2h_ref[...]) + _dot(gh, w2l_ref[...]) + _dot(gl, w2h_ref[...]) + gb_ref[...]
    log_a = -_softplus(-logit) * (1.0 / GLA_TAU)
    tril = (lax.broadcasted_iota(i32, (C, C), 0) >= lax.broadcasted_iota(i32, (C, C), 1)).astype(bf16)
    a1, a2, a3 = _split3(log_a)
    b_all = _dot(tril, a1) + _dot(tril, a2) + _dot(tril, a3)
    heads = range(GLA_HEADS)
    kss = [slice(h * GLA_DK, (h + 1) * GLA_DK) for h in heads]
    vss = [slice(h * GLA_DV, (h + 1) * GLA_DV) for h in heads]
    b = [b_all[:, ks] for ks in kss]
    q = [q_ref[0, :, ks] * GLA_DK ** -0.5 for ks in kss]
    k = [k_ref[0, :, ks] for ks in kss]
    vb = [v_ref[0, :, vs].astype(bf16) for vs in vss]
    st = [st_ref[h] for h in heads]
    inter = [_dot_nt((q[h] * jnp.exp(b[h])).astype(bf16), st[h].astype(bf16)) for h in heads]
    att = [[None] * (C // R) for _ in heads]
    for blk in range(C // R):
        r0, r1 = blk * R, (blk + 1) * R
        keep = lax.broadcasted_iota(i32, (R, r1), 1) <= lax.broadcasted_iota(i32, (R, r1), 0) + r0
        for h in heads:
            b0 = b[h][r0 - 1:r0, :] if blk else jnp.zeros((1, GLA_DK), f32)
            qe = q[h][r0:r1] * jnp.exp(b[h][r0:r1] - b0)
            ke = k[h][:r1] * jnp.exp(b0 - b[h][:r1])
            att[h][blk] = jnp.where(keep, _dot_nt(qe.astype(bf16), ke.astype(bf16)), 0.0).astype(bf16)
    intra = [jnp.concatenate([_dot(att[h][blk], vb[h][:(blk + 1) * R]) for blk in range(C // R)], axis=0)
             for h in heads]
    for h in heads:
        b_last = b[h][C - 1:C, :]
        k_dec = k[h] * jnp.exp(b_last - b[h])
        st_ref[h] = st[h] * jnp.exp(b_last) + _dot_tn(vb[h], k_dec.astype(bf16))
    for h in heads:
        o = inter[h] + intra[h]
        y = o * lax.rsqrt(jnp.mean(o * o, axis=-1, keepdims=True) + NORM_EPS) * ng_ref[...]
        rh = r_ref[0, :, vss[h]]
        o_ref[0, :, vss[h]] = (y * (rh * _sigmoid(rh))).astype(o_ref.dtype)


def _gla(p, w2, gate_b, norm_g, *, C=128, R=32):
    B, T, _ = p.shape
    C = min(C, T)
    W = GLA_HEADS * GLA_DV
    w2p = jnp.zeros((LANES, GLA_HEADS * GLA_DK), f32).at[:GLA_GATE_RANK].set(w2)
    w2h, w2l = _split2(w2p)

    def seg(name):
        off, w = P_OFF[name]
        return pl.BlockSpec((1, C, w), lambda b, c: (b, c, off // w))

    full = lambda a: pl.BlockSpec(a.shape, lambda b, c: (0,) * a.ndim)
    gb = gate_b.reshape(1, -1)
    ng = norm_g.reshape(1, -1)
    return pl.pallas_call(
        functools.partial(_gla_kernel, C=C, R=min(R, C)),
        grid=(B, T // C),
        in_specs=[seg('b_q'), seg('b_k'), seg('b_v'), seg('b_glr'), seg('b_r'),
                  full(w2h), full(w2l), full(gb), full(ng)],
        out_specs=pl.BlockSpec((1, C, W), lambda b, c: (b, c, 0)),
        out_shape=jax.ShapeDtypeStruct((B, T, W), bf16),
        scratch_shapes=[pltpu.VMEM((GLA_HEADS, GLA_DV, GLA_DK), f32)],
        compiler_params=_params(("parallel", "arbitrary")),
        name='gla',
    )(p, p, p, p, p, w2h, w2l, gb, ng)


def _unit_lower_inverses(mats, row, col):
    C = mats[0].shape[0]
    eye = jnp.where(row == col, 1.0, 0.0)
    blk = (row >> 3) == (col >> 3)
    n1 = [jnp.where(blk, a, 0.0) for a in mats]
    n2 = [_mm3(x, x) for x in n1]
    n4 = [_mm3(x, x) for x in n2]
    t = [_mm3(eye - x1, eye + x2) for x1, x2 in zip(n1, n2)]
    t = [_mm3(x, eye + x4) for x, x4 in zip(t, n4)]
    s = SUBLANES
    while s < C:
        sh = s.bit_length() - 1
        sel = ((row >> (sh + 1)) == (col >> (sh + 1))) & ((row >> sh) != (col >> sh))
        left = [_mm1(x, jnp.where(sel, a, 0.0)) for x, a in zip(t, mats)]
        t = [x - _mm1(y, x) for x, y in zip(t, left)]
        s *= 2
    return t


def _gdn_kernel(q_ref, k_ref, v_ref, ab_ref, gate_ref, alog_ref, dtb_ref, ng_ref, o_ref, s_ref, *, C):
    @pl.when(pl.program_id(1) == 0)
    def _():
        s_ref[...] = jnp.zeros_like(s_ref)

    H = GDN_HEADS
    row = lax.broadcasted_iota(i32, (C, C), 0)
    col = lax.broadcasted_iota(i32, (C, C), 1)
    ab = ab_ref[0]
    g_all = -jnp.exp(alog_ref[...]) * _softplus(ab + dtb_ref[...])
    beta_all = _sigmoid(ab)
    tril = (row >= col).astype(bf16)
    g1, g2, g3 = _split3(g_all)
    gam_all = _dot(tril, g1) + _dot(tril, g2) + _dot(tril, g3)
    gam_t = gam_all.T
    heads = range(H)
    sls = [slice(h * HEAD_DIM, (h + 1) * HEAD_DIM) for h in heads]
    q = [q_ref[0, :, sl] for sl in sls]
    k = [k_ref[0, :, sl] for sl in sls]
    kb = [x.astype(bf16) for x in k]
    gcol = [gam_all[:, h:h + 1] for h in heads]
    bcol = [beta_all[:, H + h:H + h + 1] for h in heads]
    egam = [jnp.exp(x) for x in gcol]
    dec = [jnp.where(row >= col, jnp.exp(jnp.minimum(gcol[h] - gam_t[h:h + 1, :], 0.0)), 0.0) for h in heads]
    kk = [_dot_nt(x, x) for x in kb]
    qk = [_dot_nt(q[h].astype(bf16), kb[h]) for h in heads]
    a = [jnp.where(row > col, bcol[h] * dec[h] * kk[h], 0.0) for h in heads]
    tinv = _unit_lower_inverses(a, row, col)
    sol = [_mm1(tinv[h], jnp.concatenate([bcol[h] * v_ref[0, :, sls[h]], (bcol[h] * egam[h]) * k[h]], axis=1))
           for h in heads]
    sb = [s_ref[h].astype(bf16) for h in heads]
    ws = [_dot(sol[h][:, HEAD_DIM:].astype(bf16), sb[h]) for h in heads]
    qs = [_dot((q[h] * egam[h]).astype(bf16), sb[h]) for h in heads]
    db = [(sol[h][:, :HEAD_DIM] - ws[h]).astype(bf16) for h in heads]
    o = [qs[h] + _dot((qk[h] * dec[h]).astype(bf16), db[h]) for h in heads]
    for h in heads:
        g_last = gcol[h][C - 1:C, :]
        k_dec = k[h] * jnp.exp(g_last - gcol[h])
        s_ref[h] = jnp.exp(g_last) * s_ref[h] + _dot_tn(k_dec.astype(bf16), db[h])
    for h in heads:
        y = o[h] * lax.rsqrt(jnp.mean(o[h] * o[h], axis=-1, keepdims=True) + NORM_EPS) * ng_ref[...]
        gt = gate_ref[0, :, sls[h]]
        o_ref[0, :, sls[h]] = (y * (gt * _sigmoid(gt))).astype(o_ref.dtype)


def _gdn(cq, ck, cv, p, a_log, dt_bias, norm_g, *, C=128):
    B, T, W = cq.shape
    C = min(C, T)
    H = GDN_HEADS
    alog = jnp.zeros((1, LANES), f32).at[0, :H].set(a_log)
    dtb = jnp.zeros((1, LANES), f32).at[0, :H].set(dt_bias)
    ng = norm_g.reshape(1, -1)
    tok = pl.BlockSpec((1, C, W), lambda b, c: (b, c, 0))

    def seg(name):
        off, w = P_OFF[name]
        return pl.BlockSpec((1, C, w), lambda b, c: (b, c, off // w))

    full = lambda a: pl.BlockSpec(a.shape, lambda b, c: (0,) * a.ndim)
    return pl.pallas_call(
        functools.partial(_gdn_kernel, C=C),
        grid=(B, T // C),
        in_specs=[tok, tok, tok, seg('c_ab'), seg('c_g'), full(alog), full(dtb), full(ng)],
        out_specs=tok,
        out_shape=jax.ShapeDtypeStruct((B, T, W), bf16),
        scratch_shapes=[pltpu.VMEM((H, HEAD_DIM, HEAD_DIM), f32)],
        compiler_params=_params(("parallel", "arbitrary")),
        name='gated_delta_rule',
    )(cq, ck, cv, p, p, alog, dtb, ng)


def _pack_kernel(src_ref, valid_ref, w_ref, o_ref):
    rows = lax.broadcasted_iota(i32, (LANES, 1), 0)
    keep = rows < valid_ref[pl.program_id(0)]
    for l in range(o_ref.shape[0]):
        o_ref[l] = jnp.where(keep, w_ref[:, l, :], 0.0).astype(bf16)


def _pack_w_in(w_in):
    L, D, d_in = w_in.shape
    wt = jnp.transpose(w_in, (2, 0, 1))
    src, valid = [], []
    for name, width in P_LAYOUT:
        s0, w = (REF_OFF['c_a'][0], 2 * GDN_HEADS) if name == 'c_ab' else REF_OFF[name]
        for r in range(0, width, LANES):
            assert s0 + r + LANES <= d_in
            src.append(s0 + r)
            valid.append(min(max(w - r, 0), LANES))
    grid_spec = pltpu.PrefetchScalarGridSpec(
        num_scalar_prefetch=2,
        grid=(P_WIDTH // LANES,),
        in_specs=[pl.BlockSpec((pl.Element(LANES), pl.Element(L), pl.Element(D)),
                               lambda j, src, valid: (src[j], 0, 0))],
        out_specs=pl.BlockSpec((L, LANES, D), lambda j, src, valid: (0, j, 0)),
    )
    return pl.pallas_call(
        _pack_kernel,
        grid_spec=grid_spec,
        out_shape=jax.ShapeDtypeStruct((L, P_WIDTH, D), bf16),
        compiler_params=_params(("arbitrary",)),
        name='pack_w_in',
    )(jnp.asarray(src, i32), jnp.asarray(valid, i32), wt)


def _token_mixers(hb, B, T, layer, w_in, gla_gate_w2, gla_gate_b, gla_norm_g, gdn_conv_w, gdn_a_log, gdn_dt_bias,
                  gdn_norm_g, cosf, sinf):
    p = _matmul(hb, w_in, layer, tm=1024, tn=512, tk=hb.shape[1], out_dtype=f32, b_transposed=True,
                name='in_proj')
    p = p.reshape(B, T, P_WIDTH)
    ops = _prep(p, cosf, sinf, gdn_conv_w)
    y_a = _dsa(ops)
    y_b = _gla(p, gla_gate_w2, gla_gate_b, gla_norm_g)
    y_c = _gdn(ops['cq'], ops['ck'], ops['cv'], p, gdn_a_log, gdn_dt_bias, gdn_norm_g)
    y_d = _moba(ops)
    return tuple(y.reshape(B * T, -1) for y in (y_a, y_b, y_c, y_d))


def _rope_tables(T):
    inv = ROPE_THETA ** (-jnp.arange(0, HEAD_DIM, 2, dtype=f32) / HEAD_DIM)
    ang = jnp.arange(T, dtype=f32)[:, None] * inv[None, :]
    cos, sin = jnp.cos(ang), jnp.sin(ang)
    return jnp.concatenate([cos, cos], axis=1), jnp.concatenate([-sin, sin], axis=1)


def kernel(x, ln_in_g, ln_in_b, w_in, w_out, ln1_g, ln1_b, gla_gate_w2, gla_gate_b, gla_norm_g, gdn_conv_w,
           gdn_a_log, gdn_dt_bias, gdn_norm_g, w_up, w_down, ln2_g, ln2_b):
    B, T, D = x.shape
    depth = w_in.shape[0]
    alpha = (2 * depth) ** 0.25
    cosf, sinf = _rope_tables(T)
    h, hb = _layer_norm(x.reshape(B * T, D), None, ln_in_g, ln_in_b)
    w_in_p = _pack_w_in(w_in)
    w_out_b = w_out.astype(bf16)
    for l in range(depth):
        mix = _token_mixers(hb, B, T, l, w_in_p, gla_gate_w2[l], gla_gate_b[l], gla_norm_g[l], gdn_conv_w[l],
                            gdn_a_log[l], gdn_dt_bias[l], gdn_norm_g[l], cosf, sinf)
        y, w_up_b = _matmul(mix, w_out_b, l, tm=1024, tn=512, tk=w_out.shape[1], out_dtype=f32, cast=(w_up, l),
                            name='out_proj')
        h, hb = _layer_norm(h, y, ln1_g[l], ln1_b[l], alpha=alpha)
        up, w_down_b = _matmul(hb, w_up_b, 0, tm=1024, tn=1024, tk=D, out_dtype=bf16, act='relu2',
                               cast=(w_down, l), name='mlp_up')
        ff = _matmul(up, w_down_b, 0, tm=1024, tn=1024, tk=2048, out_dtype=f32, name='mlp_down')
        h, hb = _layer_norm(h, ff, ln2_g[l], ln2_b[l], alpha=alpha)
    return h.reshape(B, T, D)
```

```python
import functools
import math

import jax
import jax.numpy as jnp
from jax import lax
from jax.experimental import pallas as pl
from jax.experimental.pallas import tpu as pltpu

f32 = jnp.float32
bf16 = jnp.bfloat16
i32 = jnp.int32

HEAD_DIM = 128
A_HEADS = 8
A_KV_HEADS = 2
IDX_HEADS = 4
IDX_DIM = 128
DSA_TOPK = 256
GLA_HEADS = 4
GLA_DK = 128
GLA_DV = 256
GLA_GATE_RANK = 16
GLA_TAU = 16.0
GDN_HEADS = 8
CONV_K = 4
MOBA_HEADS = 8
MOBA_BLOCK = 256
MOBA_TOPK = 3
ROPE_THETA = 10000.0
LN_EPS = 1e-5
NORM_EPS = 1e-6

V7X_VMEM_BYTES = 64 * 2**20
VMEM_LIMIT = V7X_VMEM_BYTES * 3 // 4
LANES = 128
SUBLANES = 8

INT_MIN = -2**31
NEG_BIG = -1e30
LOG2E = math.log2(math.e)
ATTN_Q_SCALE = HEAD_DIM ** -0.5 * LOG2E

TOK_TILE = MOBA_BLOCK

P_LAYOUT = (
    ('a_q', 1024), ('b_v', 1024), ('b_r', 1024), ('c_q', 1024), ('c_k', 1024), ('c_v', 1024), ('c_g', 1024),
    ('d_q', 1024), ('d_k', 1024), ('d_v', 1024),
    ('a_iq', 512), ('b_q', 512), ('b_k', 512),
    ('a_k', 256), ('a_v', 256),
    ('a_ik', 128), ('a_iw', 128), ('b_glr', 128), ('c_ab', 128),
)
P_OFF = {}
_o = 0
for _n, _w in P_LAYOUT:
    P_OFF[_n] = (_o, _w)
    _o += _w
P_WIDTH = _o

REF_SPLITS = (
    ('a_q', 1024), ('a_k', 256), ('a_v', 256), ('a_iq', 512), ('a_ik', 128), ('a_iw', 4),
    ('b_q', 512), ('b_k', 512), ('b_v', 1024), ('b_glr', 16), ('b_r', 1024),
    ('c_q', 1024), ('c_k', 1024), ('c_v', 1024), ('c_a', 8), ('c_b', 8), ('c_g', 1024),
    ('d_q', 1024), ('d_k', 1024), ('d_v', 1024),
)
REF_OFF = {}
_o = 0
for _n, _w in REF_SPLITS:
    REF_OFF[_n] = (_o, _w)
    _o += _w


def _dot(a, b):
    return jnp.dot(a, b, preferred_element_type=f32)


def _dot_nt(a, b):
    return lax.dot_general(a, b, (((1,), (1,)), ((), ())), preferred_element_type=f32)


def _dot_tn(a, b):
    return lax.dot_general(a, b, (((0,), (0,)), ((), ())), preferred_element_type=f32)


def _split2(x):
    hi = x.astype(bf16)
    lo = (x - hi.astype(f32)).astype(bf16)
    return hi, lo


def _split3(x):
    hi = x.astype(bf16)
    r = x - hi.astype(f32)
    mid = r.astype(bf16)
    lo = (r - mid.astype(f32)).astype(bf16)
    return hi, mid, lo


def _mm1(x, y):
    return _dot(x.astype(bf16), y.astype(bf16))


def _mm3(x, y):
    xh, xl = _split2(x)
    yh, yl = _split2(y)
    return _dot(xh, yh) + _dot(xh, yl) + _dot(xl, yh)


def _sigmoid(x):
    return 1.0 / (1.0 + jnp.exp(-x))


def _softplus(x):
    return jnp.maximum(x, 0.0) + jnp.log1p(jnp.exp(-jnp.abs(x)))


def _params(sem):
    return pltpu.CompilerParams(dimension_semantics=sem, vmem_limit_bytes=VMEM_LIMIT)


def _mm_kernel(*refs, n_a, nk, act, b_transposed, side_cast):
    a_refs, b_ref = refs[:n_a], refs[n_a]
    if side_cast:
        c_ref, o_ref, co_ref = refs[n_a + 1:n_a + 4]
        co_ref[...] = c_ref[...].astype(bf16)
        scratch = refs[n_a + 4:]
    else:
        o_ref = refs[n_a + 1]
        scratch = refs[n_a + 2:]

    def product():
        if b_transposed:
            return _dot_nt(a_refs[0][...], b_ref[...])
        if n_a == 1:
            return _dot(a_refs[0][...], b_ref[...])
        kw = b_ref.shape[0] // n_a
        out = _dot(a_refs[0][...], b_ref[0:kw, :])
        for g in range(1, n_a):
            out = out + _dot(a_refs[g][...], b_ref[g * kw:(g + 1) * kw, :])
        return out

    def finish(r):
        if act == 'relu2':
            r = jnp.square(jnp.maximum(r, 0.0))
        o_ref[...] = r.astype(o_ref.dtype)

    if nk == 1:
        finish(product())
        return
    acc_ref, = scratch
    k = pl.program_id(2)

    @pl.when(k == 0)
    def _():
        acc_ref[...] = jnp.zeros_like(acc_ref)

    acc_ref[...] += product()

    @pl.when(k == nk - 1)
    def _():
        finish(acc_ref[...])


def _matmul(a, b, layer, *, tm, tn, tk, out_dtype, act=None, b_transposed=False, cast=None, name='matmul'):
    a_list = a if isinstance(a, (tuple, list)) else (a,)
    n_a = len(a_list)
    M = a_list[0].shape[0]
    _, K, N = b.shape
    if b_transposed:
        assert n_a == 1
        K, N = N, K
    tm, tn, tk = min(tm, M), min(tn, N), min(tk, K)
    assert M % tm == 0 and N % tn == 0 and K % tk == 0
    nk = K // tk
    assert n_a == 1 or nk == 1
    ka = tk // n_a
    gj = N // tn
    b_spec = (pl.BlockSpec((None, tn, tk), lambda i, j, k: (layer, j, k)) if b_transposed else
              pl.BlockSpec((None, tk, tn), lambda i, j, k: (layer, k, j)))
    in_specs = [pl.BlockSpec((tm, ka), lambda i, j, k: (i, k))] * n_a + [b_spec]
    out_specs = [pl.BlockSpec((tm, tn), lambda i, j, k: (i, j))]
    out_shape = [jax.ShapeDtypeStruct((M, N), out_dtype)]
    args = list(a_list) + [b]
    if cast is not None:
        w, wl = cast
        _, R, C = w.shape
        steps = (M // tm) * gj * nk
        assert R % steps == 0 and (R // steps) % (2 * SUBLANES) == 0
        rc = R // steps
        in_specs.append(pl.BlockSpec((None, rc, C), lambda i, j, k: (wl, (i * gj + j) * nk + k, 0)))
        out_specs.append(pl.BlockSpec((None, rc, C), lambda i, j, k: (0, (i * gj + j) * nk + k, 0)))
        out_shape.append(jax.ShapeDtypeStruct((1, R, C), bf16))
        args.append(w)
    res = pl.pallas_call(
        functools.partial(_mm_kernel, n_a=n_a, nk=nk, act=act, b_transposed=b_transposed,
                          side_cast=cast is not None),
        grid=(M // tm, gj, nk),
        in_specs=in_specs,
        out_specs=out_specs,
        out_shape=out_shape,
        scratch_shapes=[pltpu.VMEM((tm, tn), f32)] if nk > 1 else [],
        compiler_params=_params(("parallel", "parallel", "arbitrary")),
        name=name,
    )(*args)
    return res if cast is not None else res[0]


def _ln_kernel(*refs, alpha, has_y, n_out):
    x_ref = refs[0]
    z = x_ref[...] * alpha + refs[1][...] if has_y else x_ref[...]
    g_ref, b_ref = refs[1 + has_y], refs[2 + has_y]
    outs = refs[3 + has_y:]
    mu = jnp.mean(z, axis=-1, keepdims=True)
    zc = z - mu
    var = jnp.mean(zc * zc, axis=-1, keepdims=True)
    out = zc * lax.rsqrt(var + LN_EPS) * g_ref[...] + b_ref[...]
    outs[0][...] = out
    if n_out == 2:
        outs[1][...] = out.astype(bf16)


def _layer_norm(x, y, g, b, *, alpha=1.0, tm=256, with_bf16=True):
    M, D = x.shape
    tm = min(tm, M)
    row = pl.BlockSpec((tm, D), lambda i: (i, 0))
    vec = pl.BlockSpec((1, D), lambda i: (0, 0))
    has_y = y is not None
    args = (x, y) if has_y else (x,)
    n_out = 2 if with_bf16 else 1
    res = pl.pallas_call(
        functools.partial(_ln_kernel, alpha=alpha, has_y=has_y, n_out=n_out),
        grid=(M // tm,),
        in_specs=[row] * len(args) + [vec, vec],
        out_specs=[row] * n_out,
        out_shape=[jax.ShapeDtypeStruct((M, D), f32), jax.ShapeDtypeStruct((M, D), bf16)][:n_out],
        compiler_params=_params(("parallel",)),
        name='layer_norm',
    )(*args, g.reshape(1, D), b.reshape(1, D))
    return res if with_bf16 else (res[0], None)


def _prep_kernel(aq_ref, ak_ref, av_ref, aiq_ref, aik_ref, aiw_ref,
                 cq_ref, ck_ref, cv_ref, cqh_ref, ckh_ref, cvh_ref,
                 dq_ref, dk_ref, dv_ref, cos_ref, sin_ref, cw_ref,
                 oaq, oak, oavt, oaiq, oaik, oaiwt, ocq, ock, ocv,
                 odq, odqh, odql, odk, odvt, odkmh, odkml, *, TT):
    cosf, sinf = cos_ref[...], sin_ref[...]

    def rope(x):
        return x * cosf + pltpu.roll(x, HEAD_DIM // 2, 1) * sinf

    def head(h):
        return slice(h * HEAD_DIM, (h + 1) * HEAD_DIM)

    for h in range(A_HEADS):
        oaq[0, :, head(h)] = (rope(aq_ref[0, :, head(h)]) * ATTN_Q_SCALE).astype(bf16)
    for g in range(A_KV_HEADS):
        oak[0, :, head(g)] = rope(ak_ref[0, :, head(g)]).astype(bf16)
        oavt[0, 0, head(g), :] = av_ref[0, :, head(g)].T.astype(bf16)
    for h in range(IDX_HEADS):
        oaiq[0, :, head(h)] = rope(aiq_ref[0, :, head(h)]).astype(bf16)
    oaik[0] = rope(aik_ref[0]).astype(bf16)
    oaiwt[0] = aiw_ref[0].T[:SUBLANES, :] * (IDX_HEADS ** -0.5 * IDX_DIM ** -0.5)

    first = pl.program_id(1) == 0
    for j, (x_ref, h_ref, o_ref) in enumerate(((cq_ref, cqh_ref, ocq), (ck_ref, ckh_ref, ock),
                                                (cv_ref, cvh_ref, ocv))):
        width = x_ref.shape[2]
        halo = jnp.where(first, 0.0, h_ref[0])
        cat = jnp.concatenate([halo, x_ref[0]], axis=0)
        w = cw_ref[:, j * width:(j + 1) * width]
        y = cat[SUBLANES - CONV_K + 1:SUBLANES - CONV_K + 1 + TT] * w[0:1]
        for i in range(1, CONV_K):
            s0 = SUBLANES - CONV_K + 1 + i
            y = y + cat[s0:s0 + TT] * w[i:i + 1]
        y = y * _sigmoid(y)
        if j == 2:
            o_ref[0] = y
        else:
            post = HEAD_DIM ** -0.5 if j == 0 else 1.0
            for h in range(GDN_HEADS):
                yh = y[:, head(h)]
                o_ref[0, :, head(h)] = yh * (lax.rsqrt(jnp.sum(yh * yh, axis=-1, keepdims=True) + NORM_EPS) * post)

    for h in range(MOBA_HEADS):
        qr = rope(dq_ref[0, :, head(h)])
        odq[0, :, head(h)] = (qr * ATTN_Q_SCALE).astype(bf16)
        qh, ql = _split2(qr)
        odqh[0, :, head(h)] = qh
        odql[0, :, head(h)] = ql
        kr = rope(dk_ref[0, :, head(h)])
        odk[0, :, head(h)] = kr.astype(bf16)
        kmh, kml = _split2(jnp.mean(kr, axis=0, keepdims=True))
        odkmh[0, 0, :, head(h)] = kmh
        odkml[0, 0, :, head(h)] = kml
        odvt[0, 0, head(h), :] = dv_ref[0, :, head(h)].T.astype(bf16)


def _prep(p, cosf, sinf, conv_w):
    B, T, _ = p.shape
    TT = min(TOK_TILE, T)
    NT = T // TT

    def seg(name, rows=TT):
        off, w = P_OFF[name]
        return pl.BlockSpec((1, rows, w), lambda b, i: (b, i, off // w))

    def halo(name):
        off, w = P_OFF[name]
        return pl.BlockSpec((1, SUBLANES, w),
                            lambda b, i: (b, jnp.maximum(i * (TT // SUBLANES) - 1, 0), off // w))

    in_names = ('a_q', 'a_k', 'a_v', 'a_iq', 'a_ik', 'a_iw', 'c_q', 'c_k', 'c_v')
    in_specs = ([seg(n) for n in in_names] + [halo(n) for n in ('c_q', 'c_k', 'c_v')]
                + [seg(n) for n in ('d_q', 'd_k', 'd_v')]
                + [pl.BlockSpec((TT, HEAD_DIM), lambda b, i: (i, 0))] * 2
                + [pl.BlockSpec(conv_w.shape, lambda b, i: (0, 0))])

    def tok(w, dt):
        return pl.BlockSpec((1, TT, w), lambda b, i: (b, i, 0)), jax.ShapeDtypeStruct((B, T, w), dt)

    def tposed(rows, dt):
        return (pl.BlockSpec((1, 1, rows, TT), lambda b, i: (b, i, 0, 0)),
                jax.ShapeDtypeStruct((B, NT, rows, TT), dt))

    def per_tile(w, dt):
        return (pl.BlockSpec((1, 1, 1, w), lambda b, i: (b, i, 0, 0)),
                jax.ShapeDtypeStruct((B, NT, 1, w), dt))

    outs = [
        tok(1024, bf16), tok(256, bf16), tposed(256, bf16), tok(512, bf16), tok(128, bf16),
        (pl.BlockSpec((1, SUBLANES, TT), lambda b, i: (b, 0, i)), jax.ShapeDtypeStruct((B, SUBLANES, T), f32)),
        tok(1024, f32), tok(1024, f32), tok(1024, f32),
        tok(1024, bf16), tok(1024, bf16), tok(1024, bf16), tok(1024, bf16), tposed(1024, bf16),
        per_tile(1024, bf16), per_tile(1024, bf16),
    ]
    res = pl.pallas_call(
        functools.partial(_prep_kernel, TT=TT),
        grid=(B, NT),
        in_specs=in_specs,
        out_specs=[o[0] for o in outs],
        out_shape=[o[1] for o in outs],
        compiler_params=_params(("parallel", "arbitrary")),
        name='mixer_prep',
    )(*([p] * 15), cosf, sinf, conv_w)
    names = ('aq', 'ak', 'avt', 'aiq', 'aik', 'aiwt', 'cq', 'ck', 'cv', 'dq', 'dqh', 'dql', 'dk', 'dvt',
             'dkmh', 'dkml')
    return dict(zip(names, res))


def _tiles_by_pairs(n, produce, consume, state):
    odd = n % 2

    def single(j, st):
        return consume(j, produce(j), st)

    def pair(m, st):
        j = odd + 2 * m
        first, second = produce(j), produce(j + 1)
        return consume(j + 1, second, consume(j, first, st))

    state = lax.fori_loop(0, odd, single, state)
    return lax.fori_loop(0, n // 2, pair, state)


def _flash_update(p, m_prev, m_new, l_prev, acc_prev, vt, ones):
    pb = p.astype(bf16)
    a = jnp.exp2(m_prev - m_new)
    both = _dot(jnp.concatenate([vt, ones], axis=0), pb)
    dh = vt.shape[0]
    l_new = a * l_prev + both[dh:dh + 1]
    acc_new = a * acc_prev + both[:dh]
    return m_new, l_new, acc_new


def _flash_step(s, mask, m_prev, l_prev, acc_prev, vt, ones):
    s = jnp.where(mask, s, NEG_BIG)
    m_new = jnp.maximum(m_prev, jnp.max(s, axis=0, keepdims=True))
    return _flash_update(jnp.exp2(s - m_new), m_prev, m_new, l_prev, acc_prev, vt, ones)


def _flash_step_cols(s, cols, m_prev, l_prev, acc_prev, vt, ones):
    m_new = jnp.where(cols, jnp.maximum(m_prev, jnp.max(s, axis=0, keepdims=True)), m_prev)
    p = jnp.exp2(s - jnp.where(cols, m_new, -NEG_BIG))
    return _flash_update(p, m_prev, m_new, l_prev, acc_prev, vt, ones)


def _dsa_kernel(iq_ref, iw_ref, q_ref, ik_ref, k_ref, vt_ref, o_ref, keys_ref, gmax_ref, acc_ref, *, TT, topk):
    i = pl.program_id(1)
    n_kt = i + 1
    t_idx = i * TT + lax.broadcasted_iota(i32, (1, TT), 1)
    row = lax.broadcasted_iota(i32, (TT, 1), 0)
    iw = iw_ref[0]
    iq = iq_ref[0]

    def tile(kt):
        return pl.ds(pl.multiple_of(kt * TT, TT), TT)

    def head_scores(kt):
        ik = ik_ref[0, tile(kt), :]
        return [_dot_nt(ik, iq[:, h * IDX_DIM:(h + 1) * IDX_DIM]) for h in range(IDX_HEADS)]

    def store_keys(kt, xs, carry):
        acc = jnp.zeros((TT, TT), f32)
        for h in range(IDX_HEADS):
            acc = acc + iw[h:h + 1, :] * jnp.maximum(xs[h], 0.0)
        bits = lax.bitcast_convert_type(acc, i32)
        key = bits ^ ((bits >> 31) & jnp.int32(0x7FFFFFFF))
        key = jnp.where(key == -1, 0, key)
        key = jnp.where(kt * TT + row <= t_idx, key, jnp.int32(INT_MIN))
        keys_ref[tile(kt), :] = key
        gmax_ref[...] = jnp.maximum(gmax_ref[...], key)
        return carry

    gmax_ref[...] = jnp.full((TT, TT), INT_MIN, i32)
    _tiles_by_pairs(n_kt, head_scores, store_keys, 0)

    def count(pred_fn):
        def body(kt, c):
            ind = pred_fn(keys_ref[tile(kt), :]).astype(i32)
            return c + jnp.sum(ind.reshape(TT // SUBLANES, SUBLANES, TT), axis=0)
        c = lax.fori_loop(0, n_kt, body, jnp.zeros((SUBLANES, TT), i32))
        return jnp.sum(c, axis=0, keepdims=True)

    n_nonneg = count(lambda key: key >= 0)
    nonneg = n_nonneg >= topk
    takes_all = t_idx + 1 <= topk
    zero_thr = nonneg & (count(lambda key: key > 0) < topk)
    gmax = gmax_ref[...]
    lo = jnp.min(gmax, axis=0, keepdims=True)
    hi = jnp.max(gmax, axis=0, keepdims=True) + 1
    cnt_lo = jnp.where(nonneg & (lo <= 0), n_nonneg, -1)
    lo, hi = jnp.where(nonneg, jnp.maximum(lo, 0), lo), jnp.where(nonneg, hi, jnp.minimum(hi, 0))

    def pending(lo, hi, cnt_lo):
        settled = takes_all | zero_thr | (cnt_lo == topk) | (hi - lo == 1)
        return jnp.max(jnp.where(settled, 0, 1))

    def bisect(state):
        it, lo, hi, cnt_lo, _ = state
        mid = lo + lax.shift_right_logical(hi - lo, 1)
        cnt = count(lambda key: key >= mid)
        ok = cnt >= topk
        lo, hi, cnt_lo = jnp.where(ok, mid, lo), jnp.where(ok, hi, mid), jnp.where(ok, cnt, cnt_lo)
        return it + 1, lo, hi, cnt_lo, pending(lo, hi, cnt_lo)

    state = (jnp.int32(0), lo, hi, cnt_lo, pending(lo, hi, cnt_lo))
    _, lo, _, _, _ = lax.while_loop(lambda st: (st[0] < 33) & (st[4] > 0), bisect, state)
    thr = jnp.where(takes_all, jnp.int32(INT_MIN), jnp.where(zero_thr, 0, lo))
    need = (topk - count(lambda key: key > thr)).astype(f32)

    tri = (lax.broadcasted_iota(i32, (TT, TT), 0) > lax.broadcasted_iota(i32, (TT, TT), 1)).astype(bf16)
    ones = jnp.ones((2 * SUBLANES, TT), bf16)
    group = A_HEADS // A_KV_HEADS
    acc_ref[...] = jnp.zeros_like(acc_ref)

    def attend_tile(kt, carry):
        tie_carry, stats = carry
        key = keys_ref[tile(kt), :]
        eq = (key == thr) & (key != INT_MIN)
        eq_f = jnp.where(eq, 1.0, 0.0)
        tie_rank = _dot(tri, eq_f.astype(bf16)) + tie_carry
        mask = (key > thr) | (eq & (tie_rank < need))
        ktile = k_ref[0, tile(kt), :]
        logits = [_dot_nt(ktile[:, (h // group) * HEAD_DIM:(h // group + 1) * HEAD_DIM],
                          q_ref[0, :, h * HEAD_DIM:(h + 1) * HEAD_DIM]) for h in range(A_HEADS)]
        new_stats = []
        for h in range(A_HEADS):
            g = h // group
            m_new, l_new, acc_new = _flash_step(logits[h], mask, *stats[h], acc_ref[h],
                                                vt_ref[0, kt, g * HEAD_DIM:(g + 1) * HEAD_DIM, :], ones)
            acc_ref[h] = acc_new
            new_stats.append((m_new, l_new))
        return tie_carry + jnp.sum(eq_f, axis=0, keepdims=True), tuple(new_stats)

    init = (jnp.zeros((1, TT), f32),
            tuple((jnp.full((1, TT), NEG_BIG, f32), jnp.zeros((1, TT), f32)) for _ in range(A_HEADS)))
    _, stats = lax.fori_loop(0, n_kt, attend_tile, init)
    for h in range(A_HEADS):
        o_ref[0, :, h * HEAD_DIM:(h + 1) * HEAD_DIM] = (acc_ref[h] / stats[h][1]).T.astype(o_ref.dtype)


def _dsa(ops):
    B, T, W = ops['aq'].shape
    TT = min(TOK_TILE, T)
    NT = T // TT
    topk = min(DSA_TOPK, T // 4)
    assert topk <= TT
    return pl.pallas_call(
        functools.partial(_dsa_kernel, TT=TT, topk=topk),
        grid=(B, NT),
        in_specs=[
            pl.BlockSpec((1, TT, IDX_HEADS * IDX_DIM), lambda b, i: (b, i, 0)),
            pl.BlockSpec((1, SUBLANES, TT), lambda b, i: (b, 0, i)),
            pl.BlockSpec((1, TT, W), lambda b, i: (b, i, 0)),
            pl.BlockSpec((1, T, IDX_DIM), lambda b, i: (b, 0, 0)),
            pl.BlockSpec((1, T, A_KV_HEADS * HEAD_DIM), lambda b, i: (b, 0, 0)),
            pl.BlockSpec((1, NT, A_KV_HEADS * HEAD_DIM, TT), lambda b, i: (b, 0, 0, 0)),
        ],
        out_specs=pl.BlockSpec((1, TT, W), lambda b, i: (b, i, 0)),
        out_shape=jax.ShapeDtypeStruct((B, T, W), bf16),
        scratch_shapes=[pltpu.VMEM((T, TT), i32), pltpu.VMEM((TT, TT), i32),
                        pltpu.VMEM((A_HEADS, HEAD_DIM, TT), f32)],
        compiler_params=_params(("parallel", "arbitrary")),
        name='dsa_attention',
    )(ops['aiq'], ops['aiwt'], ops['aq'], ops['aik'], ops['ak'], ops['avt'])


def _moba_kernel(q_ref, qh_ref, ql_ref, kmh_ref, kml_ref, k_ref, vt_ref, o_ref, sel_ref, acc_ref,
                 *, NB, BS, n_sel, HP):
    i = pl.program_id(2)
    n_idx = lax.broadcasted_iota(i32, (NB, 1), 0)
    past = n_idx < i

    def head(h):
        return slice(h * HEAD_DIM, (h + 1) * HEAD_DIM)

    for h in range(HP):
        kmh, kml = kmh_ref[0, :, head(h)], kml_ref[0, :, head(h)]
        qh, ql = qh_ref[0, :, head(h)], ql_ref[0, :, head(h)]
        gate = _dot_nt(kmh, qh) + _dot_nt(kmh, ql) + _dot_nt(kml, qh)
        g = jnp.where(past, gate, -jnp.inf)
        sel = jnp.zeros((NB, BS), f32)
        for _ in range(n_sel):
            m = jnp.max(g, axis=0, keepdims=True)
            first = jnp.min(jnp.where(g == m, n_idx, NB), axis=0, keepdims=True)
            pick = (n_idx == first) & (m > -jnp.inf)
            sel = jnp.where(pick, 1.0, sel)
            g = jnp.where(pick, -jnp.inf, g)
        sel_ref[h] = sel

    acc_ref[...] = jnp.zeros_like(acc_ref)
    ones = jnp.ones((2 * SUBLANES, BS), bf16)

    def logits_of(n):
        ktile = k_ref[0, pl.ds(pl.multiple_of(n * BS, BS), BS), :]
        return tuple(_dot_nt(ktile[:, head(h)], q_ref[0, :, head(h)]) for h in range(HP))

    def block(n, logits, stats, step_fn):
        new_stats = []
        for h in range(HP):
            m_new, l_new, acc_new = step_fn(h, logits[h], *stats[h], acc_ref[h], vt_ref[0, n, head(h), :], ones)
            acc_ref[h] = acc_new
            new_stats.append((m_new, l_new))
        return tuple(new_stats)

    def past_block(n, logits, stats):
        return block(n, logits, stats,
                     lambda h, s, *rest: _flash_step_cols(s, sel_ref[h, pl.ds(n, 1), :] > 0.5, *rest))

    init = tuple((jnp.full((1, BS), NEG_BIG, f32), jnp.zeros((1, BS), f32)) for _ in range(HP))
    stats = _tiles_by_pairs(i, logits_of, past_block, init)
    causal = lax.broadcasted_iota(i32, (BS, BS), 0) <= lax.broadcasted_iota(i32, (BS, BS), 1)
    stats = block(i, logits_of(i), stats, lambda h, s, *rest: _flash_step(s, causal, *rest))
    for h in range(HP):
        o_ref[0, :, head(h)] = (acc_ref[h] / stats[h][1]).T.astype(o_ref.dtype)


def _moba(ops, *, HP=4):
    B, T, W = ops['dq'].shape
    BS = MOBA_BLOCK
    assert T % BS == 0 and TOK_TILE == BS
    NB = T // BS
    n_sel = min(MOBA_TOPK, NB - 1)
    WP = HP * HEAD_DIM
    kmh = ops['dkmh'].reshape(B, NB, W)
    kml = ops['dkml'].reshape(B, NB, W)
    qspec = pl.BlockSpec((1, BS, WP), lambda b, hg, i: (b, i, hg))
    kmspec = pl.BlockSpec((1, NB, WP), lambda b, hg, i: (b, 0, hg))
    return pl.pallas_call(
        functools.partial(_moba_kernel, NB=NB, BS=BS, n_sel=n_sel, HP=HP),
        grid=(B, W // WP, NB),
        in_specs=[qspec, qspec, qspec, kmspec, kmspec,
                  pl.BlockSpec((1, T, WP), lambda b, hg, i: (b, 0, hg)),
                  pl.BlockSpec((1, NB, WP, BS), lambda b, hg, i: (b, 0, hg, 0))],
        out_specs=qspec,
        out_shape=jax.ShapeDtypeStruct((B, T, W), bf16),
        scratch_shapes=[pltpu.VMEM((HP, NB, BS), f32), pltpu.VMEM((HP, HEAD_DIM, BS), f32)],
        compiler_params=_params(("parallel", "parallel", "arbitrary")),
        name='moba_attention',
    )(ops['dq'], ops['dqh'], ops['dql'], kmh, kml, ops['dk'], ops['dvt'])


def _gla_kernel(q_ref, k_ref, v_ref, glr_ref, r_ref, w2h_ref, w2l_ref, gb_ref, ng_ref, o_ref, st_ref, *, C, R):
    @pl.when(pl.program_id(1) == 0)
    def _():
        st_ref[...] = jnp.zeros_like(st_ref)

    gh, gl = _split2(glr_ref[0])
    logit = _dot(gh, w2h_ref[...]) + _dot(gh, w2l_ref[...]) + _dot(gl, w2h_ref[...]) + gb_ref[...]
    log_a = -_softplus(-logit) * (1.0 / GLA_TAU)
    tril = (lax.broadcasted_iota(i32, (C, C), 0) >= lax.broadcasted_iota(i32, (C, C), 1)).astype(bf16)
    a1, a2, a3 = _split3(log_a)
    b_all = _dot(tril, a1) + _dot(tril, a2) + _dot(tril, a3)
    heads = range(GLA_HEADS)
    kss = [slice(h * GLA_DK, (h + 1) * GLA_DK) for h in heads]
    vss = [slice(h * GLA_DV, (h + 1) * GLA_DV) for h in heads]
    b = [b_all[:, ks] for ks in kss]
    q = [q_ref[0, :, ks] * GLA_DK ** -0.5 for ks in kss]
    k = [k_ref[0, :, ks] for ks in kss]
    vb = [v_ref[0, :, vs].astype(bf16) for vs in vss]
    st = [st_ref[h] for h in heads]
    inter = [_dot_nt((q[h] * jnp.exp(b[h])).astype(bf16), st[h].astype(bf16)) for h in heads]
    att = [[None] * (C // R) for _ in heads]
    for blk in range(C // R):
        r0, r1 = blk * R, (blk + 1) * R
        keep = lax.broadcasted_iota(i32, (R, r1), 1) <= lax.broadcasted_iota(i32, (R, r1), 0) + r0
        for h in heads:
            b0 = b[h][r0 - 1:r0, :] if blk else jnp.zeros((1, GLA_DK), f32)
            qe = q[h][r0:r1] * jnp.exp(b[h][r0:r1] - b0)
            ke = k[h][:r1] * jnp.exp(b0 - b[h][:r1])
            att[h][blk] = jnp.where(keep, _dot_nt(qe.astype(bf16), ke.astype(bf16)), 0.0).astype(bf16)
    intra = [jnp.concatenate([_dot(att[h][blk], vb[h][:(blk + 1) * R]) for blk in range(C // R)], axis=0)
             for h in heads]
    for h in heads:
        b_last = b[h][C - 1:C, :]
        k_dec = k[h] * jnp.exp(b_last - b[h])
        st_ref[h] = st[h] * jnp.exp(b_last) + _dot_tn(vb[h], k_dec.astype(bf16))
    for h in heads:
        o = inter[h] + intra[h]
        y = o * lax.rsqrt(jnp.mean(o * o, axis=-1, keepdims=True) + NORM_EPS) * ng_ref[...]
        rh = r_ref[0, :, vss[h]]
        o_ref[0, :, vss[h]] = (y * (rh * _sigmoid(rh))).astype(o_ref.dtype)


def _gla(p, w2, gate_b, norm_g, *, C=128, R=32):
    B, T, _ = p.shape
    C = min(C, T)
    W = GLA_HEADS * GLA_DV
    w2p = jnp.zeros((LANES, GLA_HEADS * GLA_DK), f32).at[:GLA_GATE_RANK].set(w2)
    w2h, w2l = _split2(w2p)

    def seg(name):
        off, w = P_OFF[name]
        return pl.BlockSpec((1, C, w), lambda b, c: (b, c, off // w))

    full = lambda a: pl.BlockSpec(a.shape, lambda b, c: (0,) * a.ndim)
    gb = gate_b.reshape(1, -1)
    ng = norm_g.reshape(1, -1)
    return pl.pallas_call(
        functools.partial(_gla_kernel, C=C, R=min(R, C)),
        grid=(B, T // C),
        in_specs=[seg('b_q'), seg('b_k'), seg('b_v'), seg('b_glr'), seg('b_r'),
                  full(w2h), full(w2l), full(gb), full(ng)],
        out_specs=pl.BlockSpec((1, C, W), lambda b, c: (b, c, 0)),
        out_shape=jax.ShapeDtypeStruct((B, T, W), bf16),
        scratch_shapes=[pltpu.VMEM((GLA_HEADS, GLA_DV, GLA_DK), f32)],
        compiler_params=_params(("parallel", "arbitrary")),
        name='gla',
    )(p, p, p, p, p, w2h, w2l, gb, ng)


def _unit_lower_inverses(mats, row, col):
    C = mats[0].shape[0]
    eye = jnp.where(row == col, 1.0, 0.0)
    blk = (row >> 3) == (col >> 3)
    n1 = [jnp.where(blk, a, 0.0) for a in mats]
    n2 = [_mm3(x, x) for x in n1]
    n4 = [_mm3(x, x) for x in n2]
    t = [_mm3(eye - x1, eye + x2) for x1, x2 in zip(n1, n2)]
    t = [_mm3(x, eye + x4) for x, x4 in zip(t, n4)]
    s = SUBLANES
    while s < C:
        sh = s.bit_length() - 1
        sel = ((row >> (sh + 1)) == (col >> (sh + 1))) & ((row >> sh) != (col >> sh))
        left = [_mm1(x, jnp.where(sel, a, 0.0)) for x, a in zip(t, mats)]
        t = [x - _mm1(y, x) for x, y in zip(t, left)]
        s *= 2
    return t


def _gdn_kernel(q_ref, k_ref, v_ref, ab_ref, gate_ref, alog_ref, dtb_ref, ng_ref, o_ref, s_ref, *, C):
    @pl.when(pl.program_id(1) == 0)
    def _():
        s_ref[...] = jnp.zeros_like(s_ref)

    H = GDN_HEADS
    row = lax.broadcasted_iota(i32, (C, C), 0)
    col = lax.broadcasted_iota(i32, (C, C), 1)
    ab = ab_ref[0]
    g_all = -jnp.exp(alog_ref[...]) * _softplus(ab + dtb_ref[...])
    beta_all = _sigmoid(ab)
    tril = (row >= col).astype(bf16)
    g1, g2, g3 = _split3(g_all)
    gam_all = _dot(tril, g1) + _dot(tril, g2) + _dot(tril, g3)
    gam_t = gam_all.T
    heads = range(H)
    sls = [slice(h * HEAD_DIM, (h + 1) * HEAD_DIM) for h in heads]
    q = [q_ref[0, :, sl] for sl in sls]
    k = [k_ref[0, :, sl] for sl in sls]
    kb = [x.astype(bf16) for x in k]
    gcol = [gam_all[:, h:h + 1] for h in heads]
    bcol = [beta_all[:, H + h:H + h + 1] for h in heads]
    egam = [jnp.exp(x) for x in gcol]
    dec = [jnp.where(row >= col, jnp.exp(jnp.minimum(gcol[h] - gam_t[h:h + 1, :], 0.0)), 0.0) for h in heads]
    kk = [_dot_nt(x, x) for x in kb]
    qk = [_dot_nt(q[h].astype(bf16), kb[h]) for h in heads]
    a = [jnp.where(row > col, bcol[h] * dec[h] * kk[h], 0.0) for h in heads]
    tinv = _unit_lower_inverses(a, row, col)
    sol = [_mm1(tinv[h], jnp.concatenate([bcol[h] * v_ref[0, :, sls[h]], (bcol[h] * egam[h]) * k[h]], axis=1))
           for h in heads]
    sb = [s_ref[h].astype(bf16) for h in heads]
    ws = [_dot(sol[h][:, HEAD_DIM:].astype(bf16), sb[h]) for h in heads]
    qs = [_dot((q[h] * egam[h]).astype(bf16), sb[h]) for h in heads]
    db = [(sol[h][:, :HEAD_DIM] - ws[h]).astype(bf16) for h in heads]
    o = [qs[h] + _dot((qk[h] * dec[h]).astype(bf16), db[h]) for h in heads]
    for h in heads:
        g_last = gcol[h][C - 1:C, :]
        k_dec = k[h] * jnp.exp(g_last - gcol[h])
        s_ref[h] = jnp.exp(g_last) * s_ref[h] + _dot_tn(k_dec.astype(bf16), db[h])
    for h in heads:
        y = o[h] * lax.rsqrt(jnp.mean(o[h] * o[h], axis=-1, keepdims=True) + NORM_EPS) * ng_ref[...]
        gt = gate_ref[0, :, sls[h]]
        o_ref[0, :, sls[h]] = (y * (gt * _sigmoid(gt))).astype(o_ref.dtype)


def _gdn(cq, ck, cv, p, a_log, dt_bias, norm_g, *, C=128):
    B, T, W = cq.shape
    C = min(C, T)
    H = GDN_HEADS
    alog = jnp.zeros((1, LANES), f32).at[0, :H].set(a_log)
    dtb = jnp.zeros((1, LANES), f32).at[0, :H].set(dt_bias)
    ng = norm_g.reshape(1, -1)
    tok = pl.BlockSpec((1, C, W), lambda b, c: (b, c, 0))

    def seg(name):
        off, w = P_OFF[name]
        return pl.BlockSpec((1, C, w), lambda b, c: (b, c, off // w))

    full = lambda a: pl.BlockSpec(a.shape, lambda b, c: (0,) * a.ndim)
    return pl.pallas_call(
        functools.partial(_gdn_kernel, C=C),
        grid=(B, T // C),
        in_specs=[tok, tok, tok, seg('c_ab'), seg('c_g'), full(alog), full(dtb), full(ng)],
        out_specs=tok,
        out_shape=jax.ShapeDtypeStruct((B, T, W), bf16),
        scratch_shapes=[pltpu.VMEM((H, HEAD_DIM, HEAD_DIM), f32)],
        compiler_params=_params(("parallel", "arbitrary")),
        name='gated_delta_rule',
    )(cq, ck, cv, p, p, alog, dtb, ng)


def _pack_kernel(src_ref, valid_ref, w_ref, o_ref):
    rows = lax.broadcasted_iota(i32, (LANES, 1), 0)
    keep = rows < valid_ref[pl.program_id(0)]
    for l in range(o_ref.shape[0]):
        o_ref[l] = jnp.where(keep, w_ref[:, l, :], 0.0).astype(bf16)


def _pack_w_in(w_in):
    L, D, d_in = w_in.shape
    wt = jnp.transpose(w_in, (2, 0, 1))
    src, valid = [], []
    for name, width in P_LAYOUT:
        s0, w = (REF_OFF['c_a'][0], 2 * GDN_HEADS) if name == 'c_ab' else REF_OFF[name]
        for r in range(0, width, LANES):
            assert s0 + r + LANES <= d_in
            src.append(s0 + r)
            valid.append(min(max(w - r, 0), LANES))
    grid_spec = pltpu.PrefetchScalarGridSpec(
        num_scalar_prefetch=2,
        grid=(P_WIDTH // LANES,),
        in_specs=[pl.BlockSpec((pl.Element(LANES), pl.Element(L), pl.Element(D)),
                               lambda j, src, valid: (src[j], 0, 0))],
        out_specs=pl.BlockSpec((L, LANES, D), lambda j, src, valid: (0, j, 0)),
    )
    return pl.pallas_call(
        _pack_kernel,
        grid_spec=grid_spec,
        out_shape=jax.ShapeDtypeStruct((L, P_WIDTH, D), bf16),
        compiler_params=_params(("arbitrary",)),
        name='pack_w_in',
    )(jnp.asarray(src, i32), jnp.asarray(valid, i32), wt)


def _token_mixers(hb, B, T, layer, w_in, gla_gate_w2, gla_gate_b, gla_norm_g, gdn_conv_w, gdn_a_log, gdn_dt_bias,
                  gdn_norm_g, cosf, sinf):
    p = _matmul(hb, w_in, layer, tm=1024, tn=512, tk=hb.shape[1], out_dtype=f32, b_transposed=True,
                name='in_proj')
    p = p.reshape(B, T, P_WIDTH)
    ops = _prep(p, cosf, sinf, gdn_conv_w)
    y_a = _dsa(ops)
    y_b = _gla(p, gla_gate_w2, gla_gate_b, gla_norm_g)
    y_c = _gdn(ops['cq'], ops['ck'], ops['cv'], p, gdn_a_log, gdn_dt_bias, gdn_norm_g)
    y_d = _moba(ops)
    return tuple(y.reshape(B * T, -1) for y in (y_a, y_b, y_c, y_d))


def _rope_tables(T):
    inv = ROPE_THETA ** (-jnp.arange(0, HEAD_DIM, 2, dtype=f32) / HEAD_DIM)
    ang = jnp.arange(T, dtype=f32)[:, None] * inv[None, :]
    cos, sin = jnp.cos(ang), jnp.sin(ang)
    return jnp.concatenate([cos, cos], axis=1), jnp.concatenate([-sin, sin], axis=1)


def kernel(x, ln_in_g, ln_in_b, w_in, w_out, ln1_g, ln1_b, gla_gate_w2, gla_gate_b, gla_norm_g, gdn_conv_w,
           gdn_a_log, gdn_dt_bias, gdn_norm_g, w_up, w_down, ln2_g, ln2_b):
    B, T, D = x.shape
    depth = w_in.shape[0]
    alpha = (2 * depth) ** 0.25
    cosf, sinf = _rope_tables(T)
    h, hb = _layer_norm(x.reshape(B * T, D), None, ln_in_g, ln_in_b)
    w_in_p = _pack_w_in(w_in)
    w_out_b = w_out.astype(bf16)
    for l in range(depth):
        mix = _token_mixers(hb, B, T, l, w_in_p, gla_gate_w2[l], gla_gate_b[l], gla_norm_g[l], gdn_conv_w[l],
                            gdn_a_log[l], gdn_dt_bias[l], gdn_norm_g[l], cosf, sinf)
        y, w_up_b = _matmul(mix, w_out_b, l, tm=1024, tn=512, tk=w_out.shape[1], out_dtype=f32, cast=(w_up, l),
                            name='out_proj')
        h, hb = _layer_norm(h, y, ln1_g[l], ln1_b[l], alpha=alpha)
        up, w_down_b = _matmul(hb, w_up_b, 0, tm=1024, tn=1024, tk=D, out_dtype=bf16, act='relu2',
                               cast=(w_down, l), name='mlp_up')
        ff = _matmul(up, w_down_b, 0, tm=1024, tn=1024, tk=2048, out_dtype=f32, name='mlp_down')
        h, hb = _layer_norm(h, ff, ln2_g[l], ln2_b[l], alpha=alpha, with_bf16=l + 1 < depth)
    return h.reshape(B, T, D)
```

```python
import functools
import math

import jax
import jax.numpy as jnp
from jax import lax
from jax.experimental import pallas as pl
from jax.experimental.pallas import tpu as pltpu

f32 = jnp.float32
bf16 = jnp.bfloat16
i32 = jnp.int32

HEAD_DIM = 128
A_HEADS = 8
A_KV_HEADS = 2
IDX_HEADS = 4
IDX_DIM = 128
DSA_TOPK = 256
GLA_HEADS = 4
GLA_DK = 128
GLA_DV = 256
GLA_GATE_RANK = 16
GLA_TAU = 16.0
GDN_HEADS = 8
CONV_K = 4
MOBA_HEADS = 8
MOBA_BLOCK = 256
MOBA_TOPK = 3
ROPE_THETA = 10000.0
LN_EPS = 1e-5
NORM_EPS = 1e-6

V7X_VMEM_BYTES = 64 * 2**20
VMEM_LIMIT = V7X_VMEM_BYTES * 3 // 4
LANES = 128
SUBLANES = 8

INT_MIN = -2**31
NEG_BIG = -1e30
LOG2E = math.log2(math.e)
ATTN_Q_SCALE = HEAD_DIM ** -0.5 * LOG2E

TOK_TILE = MOBA_BLOCK

P_LAYOUT = (
    ('a_q', 1024), ('b_v', 1024), ('b_r', 1024), ('c_q', 1024), ('c_k', 1024), ('c_v', 1024), ('c_g', 1024),
    ('d_q', 1024), ('d_k', 1024), ('d_v', 1024),
    ('a_iq', 512), ('b_q', 512), ('b_k', 512),
    ('a_k', 256), ('a_v', 256),
    ('a_ik', 128), ('a_iw', 128), ('b_glr', 128), ('c_ab', 128),
)
P_OFF = {}
_o = 0
for _n, _w in P_LAYOUT:
    P_OFF[_n] = (_o, _w)
    _o += _w
P_WIDTH = _o

REF_SPLITS = (
    ('a_q', 1024), ('a_k', 256), ('a_v', 256), ('a_iq', 512), ('a_ik', 128), ('a_iw', 4),
    ('b_q', 512), ('b_k', 512), ('b_v', 1024), ('b_glr', 16), ('b_r', 1024),
    ('c_q', 1024), ('c_k', 1024), ('c_v', 1024), ('c_a', 8), ('c_b', 8), ('c_g', 1024),
    ('d_q', 1024), ('d_k', 1024), ('d_v', 1024),
)
REF_OFF = {}
_o = 0
for _n, _w in REF_SPLITS:
    REF_OFF[_n] = (_o, _w)
    _o += _w


def _dot(a, b):
    return jnp.dot(a, b, preferred_element_type=f32)


def _dot_nt(a, b):
    return lax.dot_general(a, b, (((1,), (1,)), ((), ())), preferred_element_type=f32)


def _dot_tn(a, b):
    return lax.dot_general(a, b, (((0,), (0,)), ((), ())), preferred_element_type=f32)


def _split2(x):
    hi = x.astype(bf16)
    lo = (x - hi.astype(f32)).astype(bf16)
    return hi, lo


def _split3(x):
    hi = x.astype(bf16)
    r = x - hi.astype(f32)
    mid = r.astype(bf16)
    lo = (r - mid.astype(f32)).astype(bf16)
    return hi, mid, lo


def _mm1(x, y):
    return _dot(x.astype(bf16), y.astype(bf16))


def _mm3(x, y):
    xh, xl = _split2(x)
    yh, yl = _split2(y)
    return _dot(xh, yh) + _dot(xh, yl) + _dot(xl, yh)


def _sigmoid(x):
    return 1.0 / (1.0 + jnp.exp(-x))


def _softplus(x):
    return jnp.maximum(x, 0.0) + jnp.log1p(jnp.exp(-jnp.abs(x)))


def _params(sem):
    return pltpu.CompilerParams(dimension_semantics=sem, vmem_limit_bytes=VMEM_LIMIT)


def _mm_kernel(*refs, n_a, nk, act, b_transposed, side_cast):
    a_refs, b_ref = refs[:n_a], refs[n_a]
    if side_cast:
        c_ref, o_ref, co_ref = refs[n_a + 1:n_a + 4]
        co_ref[...] = c_ref[...].astype(bf16)
        scratch = refs[n_a + 4:]
    else:
        o_ref = refs[n_a + 1]
        scratch = refs[n_a + 2:]

    def product():
        if b_transposed:
            return _dot_nt(a_refs[0][...], b_ref[...])
        if n_a == 1:
            return _dot(a_refs[0][...], b_ref[...])
        kw = b_ref.shape[0] // n_a
        out = _dot(a_refs[0][...], b_ref[0:kw, :])
        for g in range(1, n_a):
            out = out + _dot(a_refs[g][...], b_ref[g * kw:(g + 1) * kw, :])
        return out

    def finish(r):
        if act == 'relu2':
            r = jnp.square(jnp.maximum(r, 0.0))
        o_ref[...] = r.astype(o_ref.dtype)

    if nk == 1:
        finish(product())
        return
    acc_ref, = scratch
    k = pl.program_id(2)

    @pl.when(k == 0)
    def _():
        acc_ref[...] = jnp.zeros_like(acc_ref)

    acc_ref[...] += product()

    @pl.when(k == nk - 1)
    def _():
        finish(acc_ref[...])


def _matmul(a, b, layer, *, tm, tn, tk, out_dtype, act=None, b_transposed=False, cast=None, name='matmul'):
    a_list = a if isinstance(a, (tuple, list)) else (a,)
    n_a = len(a_list)
    M = a_list[0].shape[0]
    _, K, N = b.shape
    if b_transposed:
        assert n_a == 1
        K, N = N, K
    tm, tn, tk = min(tm, M), min(tn, N), min(tk, K)
    assert M % tm == 0 and N % tn == 0 and K % tk == 0
    nk = K // tk
    assert n_a == 1 or nk == 1
    ka = tk // n_a
    gj = N // tn
    b_spec = (pl.BlockSpec((None, tn, tk), lambda i, j, k: (layer, j, k)) if b_transposed else
              pl.BlockSpec((None, tk, tn), lambda i, j, k: (layer, k, j)))
    in_specs = [pl.BlockSpec((tm, ka), lambda i, j, k: (i, k))] * n_a + [b_spec]
    out_specs = [pl.BlockSpec((tm, tn), lambda i, j, k: (i, j))]
    out_shape = [jax.ShapeDtypeStruct((M, N), out_dtype)]
    args = list(a_list) + [b]
    if cast is not None:
        w, wl = cast
        _, R, C = w.shape
        steps = (M // tm) * gj * nk
        assert R % steps == 0 and (R // steps) % (2 * SUBLANES) == 0
        rc = R // steps
        in_specs.append(pl.BlockSpec((None, rc, C), lambda i, j, k: (wl, (i * gj + j) * nk + k, 0)))
        out_specs.append(pl.BlockSpec((None, rc, C), lambda i, j, k: (0, (i * gj + j) * nk + k, 0)))
        out_shape.append(jax.ShapeDtypeStruct((1, R, C), bf16))
        args.append(w)
    res = pl.pallas_call(
        functools.partial(_mm_kernel, n_a=n_a, nk=nk, act=act, b_transposed=b_transposed,
                          side_cast=cast is not None),
        grid=(M // tm, gj, nk),
        in_specs=in_specs,
        out_specs=out_specs,
        out_shape=out_shape,
        scratch_shapes=[pltpu.VMEM((tm, tn), f32)] if nk > 1 else [],
        compiler_params=_params(("parallel", "parallel", "arbitrary")),
        name=name,
    )(*args)
    return res if cast is not None else res[0]


def _ln_kernel(*refs, alpha, has_y, n_out):
    x_ref = refs[0]
    z = x_ref[...] * alpha + refs[1][...] if has_y else x_ref[...]
    g_ref, b_ref = refs[1 + has_y], refs[2 + has_y]
    outs = refs[3 + has_y:]
    mu = jnp.mean(z, axis=-1, keepdims=True)
    zc = z - mu
    var = jnp.mean(zc * zc, axis=-1, keepdims=True)
    out = zc * lax.rsqrt(var + LN_EPS) * g_ref[...] + b_ref[...]
    outs[0][...] = out
    if n_out == 2:
        outs[1][...] = out.astype(bf16)


def _layer_norm(x, y, g, b, *, alpha=1.0, tm=256, with_bf16=True):
    M, D = x.shape
    tm = min(tm, M)
    row = pl.BlockSpec((tm, D), lambda i: (i, 0))
    vec = pl.BlockSpec((1, D), lambda i: (0, 0))
    has_y = y is not None
    args = (x, y) if has_y else (x,)
    n_out = 2 if with_bf16 else 1
    res = pl.pallas_call(
        functools.partial(_ln_kernel, alpha=alpha, has_y=has_y, n_out=n_out),
        grid=(M // tm,),
        in_specs=[row] * len(args) + [vec, vec],
        out_specs=[row] * n_out,
        out_shape=[jax.ShapeDtypeStruct((M, D), f32), jax.ShapeDtypeStruct((M, D), bf16)][:n_out],
        compiler_params=_params(("parallel",)),
        name='layer_norm',
    )(*args, g.reshape(1, D), b.reshape(1, D))
    return res if with_bf16 else (res[0], None)


def _prep_kernel(aq_ref, ak_ref, av_ref, aiq_ref, aik_ref, aiw_ref,
                 cq_ref, ck_ref, cv_ref, cqh_ref, ckh_ref, cvh_ref,
                 dq_ref, dk_ref, dv_ref, cos_ref, sin_ref, cw_ref,
                 oaq, oak, oavt, oaiq, oaik, oaiwt, ocq, ock, ocv,
                 odq, odqh, odql, odk, odvt, odkmh, odkml, *, TT):
    cosf, sinf = cos_ref[...], sin_ref[...]

    def rope(x):
        return x * cosf + pltpu.roll(x, HEAD_DIM // 2, 1) * sinf

    def head(h):
        return slice(h * HEAD_DIM, (h + 1) * HEAD_DIM)

    for h in range(A_HEADS):
        oaq[0, :, head(h)] = (rope(aq_ref[0, :, head(h)]) * ATTN_Q_SCALE).astype(bf16)
    for g in range(A_KV_HEADS):
        oak[0, :, head(g)] = rope(ak_ref[0, :, head(g)]).astype(bf16)
        oavt[0, 0, head(g), :] = av_ref[0, :, head(g)].T.astype(bf16)
    for h in range(IDX_HEADS):
        oaiq[0, :, head(h)] = rope(aiq_ref[0, :, head(h)]).astype(bf16)
    oaik[0] = rope(aik_ref[0]).astype(bf16)
    oaiwt[0] = aiw_ref[0].T[:SUBLANES, :] * (IDX_HEADS ** -0.5 * IDX_DIM ** -0.5)

    first = pl.program_id(1) == 0
    for j, (x_ref, h_ref, o_ref) in enumerate(((cq_ref, cqh_ref, ocq), (ck_ref, ckh_ref, ock),
                                                (cv_ref, cvh_ref, ocv))):
        width = x_ref.shape[2]
        halo = jnp.where(first, 0.0, h_ref[0])
        cat = jnp.concatenate([halo, x_ref[0]], axis=0)
        w = cw_ref[:, j * width:(j + 1) * width]
        y = cat[SUBLANES - CONV_K + 1:SUBLANES - CONV_K + 1 + TT] * w[0:1]
        for i in range(1, CONV_K):
            s0 = SUBLANES - CONV_K + 1 + i
            y = y + cat[s0:s0 + TT] * w[i:i + 1]
        y = y * _sigmoid(y)
        if j == 2:
            o_ref[0] = y
        else:
            post = HEAD_DIM ** -0.5 if j == 0 else 1.0
            for h in range(GDN_HEADS):
                yh = y[:, head(h)]
                o_ref[0, :, head(h)] = yh * (lax.rsqrt(jnp.sum(yh * yh, axis=-1, keepdims=True) + NORM_EPS) * post)

    for h in range(MOBA_HEADS):
        qr = rope(dq_ref[0, :, head(h)])
        odq[0, :, head(h)] = (qr * ATTN_Q_SCALE).astype(bf16)
        qh, ql = _split2(qr)
        odqh[0, :, head(h)] = qh
        odql[0, :, head(h)] = ql
        kr = rope(dk_ref[0, :, head(h)])
        odk[0, :, head(h)] = kr.astype(bf16)
        kmh, kml = _split2(jnp.mean(kr, axis=0, keepdims=True))
        odkmh[0, 0, :, head(h)] = kmh
        odkml[0, 0, :, head(h)] = kml
        odvt[0, 0, head(h), :] = dv_ref[0, :, head(h)].T.astype(bf16)


def _prep(p, cosf, sinf, conv_w):
    B, T, _ = p.shape
    TT = min(TOK_TILE, T)
    NT = T // TT

    def seg(name, rows=TT):
        off, w = P_OFF[name]
        return pl.BlockSpec((1, rows, w), lambda b, i: (b, i, off // w))

    def halo(name):
        off, w = P_OFF[name]
        return pl.BlockSpec((1, SUBLANES, w),
                            lambda b, i: (b, jnp.maximum(i * (TT // SUBLANES) - 1, 0), off // w))

    in_names = ('a_q', 'a_k', 'a_v', 'a_iq', 'a_ik', 'a_iw', 'c_q', 'c_k', 'c_v')
    in_specs = ([seg(n) for n in in_names] + [halo(n) for n in ('c_q', 'c_k', 'c_v')]
                + [seg(n) for n in ('d_q', 'd_k', 'd_v')]
                + [pl.BlockSpec((TT, HEAD_DIM), lambda b, i: (i, 0))] * 2
                + [pl.BlockSpec(conv_w.shape, lambda b, i: (0, 0))])

    def tok(w, dt):
        return pl.BlockSpec((1, TT, w), lambda b, i: (b, i, 0)), jax.ShapeDtypeStruct((B, T, w), dt)

    def tposed(rows, dt):
        return (pl.BlockSpec((1, 1, rows, TT), lambda b, i: (b, i, 0, 0)),
                jax.ShapeDtypeStruct((B, NT, rows, TT), dt))

    def per_tile(w, dt):
        return (pl.BlockSpec((1, 1, 1, w), lambda b, i: (b, i, 0, 0)),
                jax.ShapeDtypeStruct((B, NT, 1, w), dt))

    outs = [
        tok(1024, bf16), tok(256, bf16), tposed(256, bf16), tok(512, bf16), tok(128, bf16),
        (pl.BlockSpec((1, SUBLANES, TT), lambda b, i: (b, 0, i)), jax.ShapeDtypeStruct((B, SUBLANES, T), f32)),
        tok(1024, f32), tok(1024, f32), tok(1024, f32),
        tok(1024, bf16), tok(1024, bf16), tok(1024, bf16), tok(1024, bf16), tposed(1024, bf16),
        per_tile(1024, bf16), per_tile(1024, bf16),
    ]
    res = pl.pallas_call(
        functools.partial(_prep_kernel, TT=TT),
        grid=(B, NT),
        in_specs=in_specs,
        out_specs=[o[0] for o in outs],
        out_shape=[o[1] for o in outs],
        compiler_params=_params(("parallel", "arbitrary")),
        name='mixer_prep',
    )(*([p] * 15), cosf, sinf, conv_w)
    names = ('aq', 'ak', 'avt', 'aiq', 'aik', 'aiwt', 'cq', 'ck', 'cv', 'dq', 'dqh', 'dql', 'dk', 'dvt',
             'dkmh', 'dkml')
    return dict(zip(names, res))


def _tiles_by_pairs(n, produce, consume, state):
    odd = n % 2

    def single(j, st):
        return consume(j, produce(j), st)

    def pair(m, st):
        j = odd + 2 * m
        first, second = produce(j), produce(j + 1)
        return consume(j + 1, second, consume(j, first, st))

    state = lax.fori_loop(0, odd, single, state)
    return lax.fori_loop(0, n // 2, pair, state)


def _flash_update(p, m_prev, m_new, l_prev, acc_prev, vt, ones):
    pb = p.astype(bf16)
    a = jnp.exp2(m_prev - m_new)
    both = _dot(jnp.concatenate([vt, ones], axis=0), pb)
    dh = vt.shape[0]
    l_new = a * l_prev + both[dh:dh + 1]
    acc_new = a * acc_prev + both[:dh]
    return m_new, l_new, acc_new


def _flash_step(s, mask, m_prev, l_prev, acc_prev, vt, ones):
    ms, ps = [], []
    for sl in (slice(0, LANES), slice(LANES, s.shape[1])):
        sh = jnp.where(mask[:, sl], s[:, sl], NEG_BIG)
        mh = jnp.maximum(m_prev[:, sl], jnp.max(sh, axis=0, keepdims=True))
        ms.append(mh)
        ps.append(jnp.exp2(sh - mh).astype(bf16))
    m_new = jnp.concatenate(ms, axis=1)
    return _flash_update(jnp.concatenate(ps, axis=1), m_prev, m_new, l_prev, acc_prev, vt, ones)


def _flash_step_cols(s, cols, m_prev, l_prev, acc_prev, vt, ones):
    m_new = jnp.where(cols, jnp.maximum(m_prev, jnp.max(s, axis=0, keepdims=True)), m_prev)
    p = jnp.exp2(s - jnp.where(cols, m_new, -NEG_BIG))
    return _flash_update(p, m_prev, m_new, l_prev, acc_prev, vt, ones)


def _dsa_kernel(iq_ref, iw_ref, q_ref, ik_ref, k_ref, vt_ref, o_ref, keys_ref, gmax_ref, acc_ref, *, TT, topk):
    i = pl.program_id(1)
    n_kt = i + 1
    t_idx = i * TT + lax.broadcasted_iota(i32, (1, TT), 1)
    row = lax.broadcasted_iota(i32, (TT, 1), 0)
    iw = iw_ref[0]
    iq = iq_ref[0]

    def tile(kt):
        return pl.ds(pl.multiple_of(kt * TT, TT), TT)

    def head_scores(kt):
        ik = ik_ref[0, tile(kt), :]
        return [_dot_nt(ik, iq[:, h * IDX_DIM:(h + 1) * IDX_DIM]) for h in range(IDX_HEADS)]

    def store_keys(kt, xs, carry):
        acc = jnp.zeros((TT, TT), f32)
        for h in range(IDX_HEADS):
            acc = acc + iw[h:h + 1, :] * jnp.maximum(xs[h], 0.0)
        bits = lax.bitcast_convert_type(acc, i32)
        key = bits ^ ((bits >> 31) & jnp.int32(0x7FFFFFFF))
        key = jnp.where(key == -1, 0, key)
        key = jnp.where(kt * TT + row <= t_idx, key, jnp.int32(INT_MIN))
        keys_ref[tile(kt), :] = key
        gmax_ref[...] = jnp.maximum(gmax_ref[...], key)
        return carry

    gmax_ref[...] = jnp.full((TT, TT), INT_MIN, i32)
    _tiles_by_pairs(n_kt, head_scores, store_keys, 0)

    def count(pred_fn):
        def body(kt, c):
            ind = pred_fn(keys_ref[tile(kt), :]).astype(i32)
            return c + jnp.sum(ind.reshape(TT // SUBLANES, SUBLANES, TT), axis=0)
        c = lax.fori_loop(0, n_kt, body, jnp.zeros((SUBLANES, TT), i32))
        return jnp.sum(c, axis=0, keepdims=True)

    n_nonneg = count(lambda key: key >= 0)
    nonneg = n_nonneg >= topk
    takes_all = t_idx + 1 <= topk
    zero_thr = nonneg & (count(lambda key: key > 0) < topk)
    gmax = gmax_ref[...]
    lo = jnp.min(gmax, axis=0, keepdims=True)
    hi = jnp.max(gmax, axis=0, keepdims=True) + 1
    cnt_lo = jnp.where(nonneg & (lo <= 0), n_nonneg, -1)
    lo, hi = jnp.where(nonneg, jnp.maximum(lo, 0), lo), jnp.where(nonneg, hi, jnp.minimum(hi, 0))

    def pending(lo, hi, cnt_lo):
        settled = takes_all | zero_thr | (cnt_lo == topk) | (hi - lo == 1)
        return jnp.max(jnp.where(settled, 0, 1))

    def bisect(state):
        it, lo, hi, cnt_lo, _ = state
        mid = lo + lax.shift_right_logical(hi - lo, 1)
        cnt = count(lambda key: key >= mid)
        ok = cnt >= topk
        lo, hi, cnt_lo = jnp.where(ok, mid, lo), jnp.where(ok, hi, mid), jnp.where(ok, cnt, cnt_lo)
        return it + 1, lo, hi, cnt_lo, pending(lo, hi, cnt_lo)

    state = (jnp.int32(0), lo, hi, cnt_lo, pending(lo, hi, cnt_lo))
    _, lo, _, _, _ = lax.while_loop(lambda st: (st[0] < 33) & (st[4] > 0), bisect, state)
    thr = jnp.where(takes_all, jnp.int32(INT_MIN), jnp.where(zero_thr, 0, lo))
    need = (topk - count(lambda key: key > thr)).astype(f32)

    tri = (lax.broadcasted_iota(i32, (TT, TT), 0) > lax.broadcasted_iota(i32, (TT, TT), 1)).astype(bf16)
    ones = jnp.ones((2 * SUBLANES, TT), bf16)
    group = A_HEADS // A_KV_HEADS
    acc_ref[...] = jnp.zeros_like(acc_ref)

    def attend_tile(kt, carry):
        tie_carry, stats = carry
        key = keys_ref[tile(kt), :]
        eq = (key == thr) & (key != INT_MIN)
        eq_f = jnp.where(eq, 1.0, 0.0)
        tie_rank = _dot(tri, eq_f.astype(bf16)) + tie_carry
        mask = (key > thr) | (eq & (tie_rank < need))
        ktile = k_ref[0, tile(kt), :]
        logits = [_dot_nt(ktile[:, (h // group) * HEAD_DIM:(h // group + 1) * HEAD_DIM],
                          q_ref[0, :, h * HEAD_DIM:(h + 1) * HEAD_DIM]) for h in range(A_HEADS)]
        new_stats = []
        for h in range(A_HEADS):
            g = h // group
            m_new, l_new, acc_new = _flash_step(logits[h], mask, *stats[h], acc_ref[h],
                                                vt_ref[0, kt, g * HEAD_DIM:(g + 1) * HEAD_DIM, :], ones)
            acc_ref[h] = acc_new
            new_stats.append((m_new, l_new))
        return tie_carry + jnp.sum(eq_f, axis=0, keepdims=True), tuple(new_stats)

    init = (jnp.zeros((1, TT), f32),
            tuple((jnp.full((1, TT), NEG_BIG, f32), jnp.zeros((1, TT), f32)) for _ in range(A_HEADS)))
    _, stats = lax.fori_loop(0, n_kt, attend_tile, init)
    for h in range(A_HEADS):
        o_ref[0, :, h * HEAD_DIM:(h + 1) * HEAD_DIM] = (acc_ref[h] / stats[h][1]).T.astype(o_ref.dtype)


def _dsa(ops):
    B, T, W = ops['aq'].shape
    TT = min(TOK_TILE, T)
    NT = T // TT
    topk = min(DSA_TOPK, T // 4)
    assert topk <= TT
    return pl.pallas_call(
        functools.partial(_dsa_kernel, TT=TT, topk=topk),
        grid=(B, NT),
        in_specs=[
            pl.BlockSpec((1, TT, IDX_HEADS * IDX_DIM), lambda b, i: (b, i, 0)),
            pl.BlockSpec((1, SUBLANES, TT), lambda b, i: (b, 0, i)),
            pl.BlockSpec((1, TT, W), lambda b, i: (b, i, 0)),
            pl.BlockSpec((1, T, IDX_DIM), lambda b, i: (b, 0, 0)),
            pl.BlockSpec((1, T, A_KV_HEADS * HEAD_DIM), lambda b, i: (b, 0, 0)),
            pl.BlockSpec((1, NT, A_KV_HEADS * HEAD_DIM, TT), lambda b, i: (b, 0, 0, 0)),
        ],
        out_specs=pl.BlockSpec((1, TT, W), lambda b, i: (b, i, 0)),
        out_shape=jax.ShapeDtypeStruct((B, T, W), bf16),
        scratch_shapes=[pltpu.VMEM((T, TT), i32), pltpu.VMEM((TT, TT), i32),
                        pltpu.VMEM((A_HEADS, HEAD_DIM, TT), f32)],
        compiler_params=_params(("parallel", "arbitrary")),
        name='dsa_attention',
    )(ops['aiq'], ops['aiwt'], ops['aq'], ops['aik'], ops['ak'], ops['avt'])


def _moba_kernel(q_ref, qh_ref, ql_ref, kmh_ref, kml_ref, k_ref, vt_ref, o_ref, sel_ref, acc_ref,
                 *, NB, BS, n_sel, HP):
    i = pl.program_id(2)
    n_idx = lax.broadcasted_iota(i32, (NB, 1), 0)
    past = n_idx < i

    def head(h):
        return slice(h * HEAD_DIM, (h + 1) * HEAD_DIM)

    for h in range(HP):
        kmh, kml = kmh_ref[0, :, head(h)], kml_ref[0, :, head(h)]
        qh, ql = qh_ref[0, :, head(h)], ql_ref[0, :, head(h)]
        gate = _dot_nt(kmh, qh) + _dot_nt(kmh, ql) + _dot_nt(kml, qh)
        g = jnp.where(past, gate, -jnp.inf)
        sel = jnp.zeros((NB, BS), f32)
        for _ in range(n_sel):
            m = jnp.max(g, axis=0, keepdims=True)
            first = jnp.min(jnp.where(g == m, n_idx, NB), axis=0, keepdims=True)
            pick = (n_idx == first) & (m > -jnp.inf)
            sel = jnp.where(pick, 1.0, sel)
            g = jnp.where(pick, -jnp.inf, g)
        sel_ref[h] = sel

    acc_ref[...] = jnp.zeros_like(acc_ref)
    ones = jnp.ones((2 * SUBLANES, BS), bf16)

    def logits_of(n):
        ktile = k_ref[0, pl.ds(pl.multiple_of(n * BS, BS), BS), :]
        return tuple(_dot_nt(ktile[:, head(h)], q_ref[0, :, head(h)]) for h in range(HP))

    def block(n, logits, stats, step_fn):
        new_stats = []
        for h in range(HP):
            m_new, l_new, acc_new = step_fn(h, logits[h], *stats[h], acc_ref[h], vt_ref[0, n, head(h), :], ones)
            acc_ref[h] = acc_new
            new_stats.append((m_new, l_new))
        return tuple(new_stats)

    def past_block(n, logits, stats):
        return block(n, logits, stats,
                     lambda h, s, *rest: _flash_step_cols(s, sel_ref[h, pl.ds(n, 1), :] > 0.5, *rest))

    init = tuple((jnp.full((1, BS), NEG_BIG, f32), jnp.zeros((1, BS), f32)) for _ in range(HP))
    stats = _tiles_by_pairs(i, logits_of, past_block, init)
    causal = lax.broadcasted_iota(i32, (BS, BS), 0) <= lax.broadcasted_iota(i32, (BS, BS), 1)
    stats = block(i, logits_of(i), stats, lambda h, s, *rest: _flash_step(s, causal, *rest))
    for h in range(HP):
        o_ref[0, :, head(h)] = (acc_ref[h] / stats[h][1]).T.astype(o_ref.dtype)


def _moba(ops, *, HP=4):
    B, T, W = ops['dq'].shape
    BS = MOBA_BLOCK
    assert T % BS == 0 and TOK_TILE == BS
    NB = T // BS
    n_sel = min(MOBA_TOPK, NB - 1)
    WP = HP * HEAD_DIM
    kmh = ops['dkmh'].reshape(B, NB, W)
    kml = ops['dkml'].reshape(B, NB, W)
    qspec = pl.BlockSpec((1, BS, WP), lambda b, hg, i: (b, i, hg))
    kmspec = pl.BlockSpec((1, NB, WP), lambda b, hg, i: (b, 0, hg))
    return pl.pallas_call(
        functools.partial(_moba_kernel, NB=NB, BS=BS, n_sel=n_sel, HP=HP),
        grid=(B, W // WP, NB),
        in_specs=[qspec, qspec, qspec, kmspec, kmspec,
                  pl.BlockSpec((1, T, WP), lambda b, hg, i: (b, 0, hg)),
                  pl.BlockSpec((1, NB, WP, BS), lambda b, hg, i: (b, 0, hg, 0))],
        out_specs=qspec,
        out_shape=jax.ShapeDtypeStruct((B, T, W), bf16),
        scratch_shapes=[pltpu.VMEM((HP, NB, BS), f32), pltpu.VMEM((HP, HEAD_DIM, BS), f32)],
        compiler_params=_params(("parallel", "parallel", "arbitrary")),
        name='moba_attention',
    )(ops['dq'], ops['dqh'], ops['dql'], kmh, kml, ops['dk'], ops['dvt'])


def _gla_kernel(q_ref, k_ref, v_ref, glr_ref, r_ref, w2h_ref, w2l_ref, gb_ref, ng_ref, o_ref, st_ref, *, C, R):
    @pl.when(pl.program_id(1) == 0)
    def _():
        st_ref[...] = jnp.zeros_like(st_ref)

    gh, gl = _split2(glr_ref[0])
    logit = _dot(gh, w2h_ref[...]) + _dot(gh, w2l_ref[...]) + _dot(gl, w2h_ref[...]) + gb_ref[...]
    log_a = -_softplus(-logit) * (1.0 / GLA_TAU)
    tril = (lax.broadcasted_iota(i32, (C, C), 0) >= lax.broadcasted_iota(i32, (C, C), 1)).astype(bf16)
    a1, a2, a3 = _split3(log_a)
    b_all = _dot(tril, a1) + _dot(tril, a2) + _dot(tril, a3)
    heads = range(GLA_HEADS)
    kss = [slice(h * GLA_DK, (h + 1) * GLA_DK) for h in heads]
    vss = [slice(h * GLA_DV, (h + 1) * GLA_DV) for h in heads]
    b = [b_all[:, ks] for ks in kss]
    q = [q_ref[0, :, ks] * GLA_DK ** -0.5 for ks in kss]
    k = [k_ref[0, :, ks] for ks in kss]
    vb = [v_ref[0, :, vs].astype(bf16) for vs in vss]
    st = [st_ref[h] for h in heads]
    inter = [_dot_nt((q[h] * jnp.exp(b[h])).astype(bf16), st[h].astype(bf16)) for h in heads]
    att = [[None] * (C // R) for _ in heads]
    for blk in range(C // R):
        r0, r1 = blk * R, (blk + 1) * R
        keep = lax.broadcasted_iota(i32, (R, r1), 1) <= lax.broadcasted_iota(i32, (R, r1), 0) + r0
        for h in heads:
            b0 = b[h][r0 - 1:r0, :] if blk else jnp.zeros((1, GLA_DK), f32)
            qe = q[h][r0:r1] * jnp.exp(b[h][r0:r1] - b0)
            ke = k[h][:r1] * jnp.exp(b0 - b[h][:r1])
            att[h][blk] = jnp.where(keep, _dot_nt(qe.astype(bf16), ke.astype(bf16)), 0.0).astype(bf16)
    intra = [jnp.concatenate([_dot(att[h][blk], vb[h][:(blk + 1) * R]) for blk in range(C // R)], axis=0)
             for h in heads]
    for h in heads:
        b_last = b[h][C - 1:C, :]
        k_dec = k[h] * jnp.exp(b_last - b[h])
        st_ref[h] = st[h] * jnp.exp(b_last) + _dot_tn(vb[h], k_dec.astype(bf16))
    for h in heads:
        o = inter[h] + intra[h]
        y = o * lax.rsqrt(jnp.mean(o * o, axis=-1, keepdims=True) + NORM_EPS) * ng_ref[...]
        rh = r_ref[0, :, vss[h]]
        o_ref[0, :, vss[h]] = (y * (rh * _sigmoid(rh))).astype(o_ref.dtype)


def _gla(p, w2, gate_b, norm_g, *, C=128, R=32):
    B, T, _ = p.shape
    C = min(C, T)
    W = GLA_HEADS * GLA_DV
    w2p = jnp.zeros((LANES, GLA_HEADS * GLA_DK), f32).at[:GLA_GATE_RANK].set(w2)
    w2h, w2l = _split2(w2p)

    def seg(name):
        off, w = P_OFF[name]
        return pl.BlockSpec((1, C, w), lambda b, c: (b, c, off // w))

    full = lambda a: pl.BlockSpec(a.shape, lambda b, c: (0,) * a.ndim)
    gb = gate_b.reshape(1, -1)
    ng = norm_g.reshape(1, -1)
    return pl.pallas_call(
        functools.partial(_gla_kernel, C=C, R=min(R, C)),
        grid=(B, T // C),
        in_specs=[seg('b_q'), seg('b_k'), seg('b_v'), seg('b_glr'), seg('b_r'),
                  full(w2h), full(w2l), full(gb), full(ng)],
        out_specs=pl.BlockSpec((1, C, W), lambda b, c: (b, c, 0)),
        out_shape=jax.ShapeDtypeStruct((B, T, W), bf16),
        scratch_shapes=[pltpu.VMEM((GLA_HEADS, GLA_DV, GLA_DK), f32)],
        compiler_params=_params(("parallel", "arbitrary")),
        name='gla',
    )(p, p, p, p, p, w2h, w2l, gb, ng)


def _unit_lower_inverses(mats, row, col):
    C = mats[0].shape[0]
    eye = jnp.where(row == col, 1.0, 0.0)
    blk = (row >> 3) == (col >> 3)
    n1 = [jnp.where(blk, a, 0.0) for a in mats]
    n2 = [_mm3(x, x) for x in n1]
    n4 = [_mm3(x, x) for x in n2]
    t = [_mm3(eye - x1, eye + x2) for x1, x2 in zip(n1, n2)]
    t = [_mm3(x, eye + x4) for x, x4 in zip(t, n4)]
    s = SUBLANES
    while s < C:
        sh = s.bit_length() - 1
        sel = ((row >> (sh + 1)) == (col >> (sh + 1))) & ((row >> sh) != (col >> sh))
        left = [_mm1(x, jnp.where(sel, a, 0.0)) for x, a in zip(t, mats)]
        t = [x - _mm1(y, x) for x, y in zip(t, left)]
        s *= 2
    return t


def _gdn_kernel(q_ref, k_ref, v_ref, ab_ref, gate_ref, alog_ref, dtb_ref, ng_ref, o_ref, s_ref, *, C):
    @pl.when(pl.program_id(1) == 0)
    def _():
        s_ref[...] = jnp.zeros_like(s_ref)

    H = GDN_HEADS
    row = lax.broadcasted_iota(i32, (C, C), 0)
    col = lax.broadcasted_iota(i32, (C, C), 1)
    ab = ab_ref[0]
    g_all = -jnp.exp(alog_ref[...]) * _softplus(ab + dtb_ref[...])
    beta_all = _sigmoid(ab)
    tril = (row >= col).astype(bf16)
    g1, g2, g3 = _split3(g_all)
    gam_all = _dot(tril, g1) + _dot(tril, g2) + _dot(tril, g3)
    gam_t = gam_all.T
    heads = range(H)
    sls = [slice(h * HEAD_DIM, (h + 1) * HEAD_DIM) for h in heads]
    q = [q_ref[0, :, sl] for sl in sls]
    k = [k_ref[0, :, sl] for sl in sls]
    kb = [x.astype(bf16) for x in k]
    gcol = [gam_all[:, h:h + 1] for h in heads]
    bcol = [beta_all[:, H + h:H + h + 1] for h in heads]
    egam = [jnp.exp(x) for x in gcol]
    dec = [jnp.where(row >= col, jnp.exp(jnp.minimum(gcol[h] - gam_t[h:h + 1, :], 0.0)), 0.0) for h in heads]
    kk = [_dot_nt(x, x) for x in kb]
    qk = [_dot_nt(q[h].astype(bf16), kb[h]) for h in heads]
    a = [jnp.where(row > col, bcol[h] * dec[h] * kk[h], 0.0) for h in heads]
    tinv = _unit_lower_inverses(a, row, col)
    sol = [_mm1(tinv[h], jnp.concatenate([bcol[h] * v_ref[0, :, sls[h]], (bcol[h] * egam[h]) * k[h]], axis=1))
           for h in heads]
    sb = [s_ref[h].astype(bf16) for h in heads]
    ws = [_dot(sol[h][:, HEAD_DIM:].astype(bf16), sb[h]) for h in heads]
    qs = [_dot((q[h] * egam[h]).astype(bf16), sb[h]) for h in heads]
    db = [(sol[h][:, :HEAD_DIM] - ws[h]).astype(bf16) for h in heads]
    o = [qs[h] + _dot((qk[h] * dec[h]).astype(bf16), db[h]) for h in heads]
    for h in heads:
        g_last = gcol[h][C - 1:C, :]
        k_dec = k[h] * jnp.exp(g_last - gcol[h])
        s_ref[h] = jnp.exp(g_last) * s_ref[h] + _dot_tn(k_dec.astype(bf16), db[h])
    for h in heads:
        y = o[h] * lax.rsqrt(jnp.mean(o[h] * o[h], axis=-1, keepdims=True) + NORM_EPS) * ng_ref[...]
        gt = gate_ref[0, :, sls[h]]
        o_ref[0, :, sls[h]] = (y * (gt * _sigmoid(gt))).astype(o_ref.dtype)


def _gdn(cq, ck, cv, p, a_log, dt_bias, norm_g, *, C=128):
    B, T, W = cq.shape
    C = min(C, T)
    H = GDN_HEADS
    alog = jnp.zeros((1, LANES), f32).at[0, :H].set(a_log)
    dtb = jnp.zeros((1, LANES), f32).at[0, :H].set(dt_bias)
    ng = norm_g.reshape(1, -1)
    tok = pl.BlockSpec((1, C, W), lambda b, c: (b, c, 0))

    def seg(name):
        off, w = P_OFF[name]
        return pl.BlockSpec((1, C, w), lambda b, c: (b, c, off // w))

    full = lambda a: pl.BlockSpec(a.shape, lambda b, c: (0,) * a.ndim)
    return pl.pallas_call(
        functools.partial(_gdn_kernel, C=C),
        grid=(B, T // C),
        in_specs=[tok, tok, tok, seg('c_ab'), seg('c_g'), full(alog), full(dtb), full(ng)],
        out_specs=tok,
        out_shape=jax.ShapeDtypeStruct((B, T, W), bf16),
        scratch_shapes=[pltpu.VMEM((H, HEAD_DIM, HEAD_DIM), f32)],
        compiler_params=_params(("parallel", "arbitrary")),
        name='gated_delta_rule',
    )(cq, ck, cv, p, p, alog, dtb, ng)


def _pack_kernel(src_ref, valid_ref, w_ref, o_ref):
    rows = lax.broadcasted_iota(i32, (LANES, 1), 0)
    keep = rows < valid_ref[pl.program_id(0)]
    for l in range(o_ref.shape[0]):
        o_ref[l] = jnp.where(keep, w_ref[:, l, :], 0.0).astype(bf16)


def _pack_w_in(w_in):
    L, D, d_in = w_in.shape
    wt = jnp.transpose(w_in, (2, 0, 1))
    src, valid = [], []
    for name, width in P_LAYOUT:
        s0, w = (REF_OFF['c_a'][0], 2 * GDN_HEADS) if name == 'c_ab' else REF_OFF[name]
        for r in range(0, width, LANES):
            assert s0 + r + LANES <= d_in
            src.append(s0 + r)
            valid.append(min(max(w - r, 0), LANES))
    grid_spec = pltpu.PrefetchScalarGridSpec(
        num_scalar_prefetch=2,
        grid=(P_WIDTH // LANES,),
        in_specs=[pl.BlockSpec((pl.Element(LANES), pl.Element(L), pl.Element(D)),
                               lambda j, src, valid: (src[j], 0, 0))],
        out_specs=pl.BlockSpec((L, LANES, D), lambda j, src, valid: (0, j, 0)),
    )
    return pl.pallas_call(
        _pack_kernel,
        grid_spec=grid_spec,
        out_shape=jax.ShapeDtypeStruct((L, P_WIDTH, D), bf16),
        compiler_params=_params(("arbitrary",)),
        name='pack_w_in',
    )(jnp.asarray(src, i32), jnp.asarray(valid, i32), wt)


def _token_mixers(hb, B, T, layer, w_in, gla_gate_w2, gla_gate_b, gla_norm_g, gdn_conv_w, gdn_a_log, gdn_dt_bias,
                  gdn_norm_g, cosf, sinf):
    p = _matmul(hb, w_in, layer, tm=1024, tn=512, tk=hb.shape[1], out_dtype=f32, b_transposed=True,
                name='in_proj')
    p = p.reshape(B, T, P_WIDTH)
    ops = _prep(p, cosf, sinf, gdn_conv_w)
    y_a = _dsa(ops)
    y_b = _gla(p, gla_gate_w2, gla_gate_b, gla_norm_g)
    y_c = _gdn(ops['cq'], ops['ck'], ops['cv'], p, gdn_a_log, gdn_dt_bias, gdn_norm_g)
    y_d = _moba(ops)
    return tuple(y.reshape(B * T, -1) for y in (y_a, y_b, y_c, y_d))


def _rope_tables(T):
    inv = ROPE_THETA ** (-jnp.arange(0, HEAD_DIM, 2, dtype=f32) / HEAD_DIM)
    ang = jnp.arange(T, dtype=f32)[:, None] * inv[None, :]
    cos, sin = jnp.cos(ang), jnp.sin(ang)
    return jnp.concatenate([cos, cos], axis=1), jnp.concatenate([-sin, sin], axis=1)


def kernel(x, ln_in_g, ln_in_b, w_in, w_out, ln1_g, ln1_b, gla_gate_w2, gla_gate_b, gla_norm_g, gdn_conv_w,
           gdn_a_log, gdn_dt_bias, gdn_norm_g, w_up, w_down, ln2_g, ln2_b):
    B, T, D = x.shape
    depth = w_in.shape[0]
    alpha = (2 * depth) ** 0.25
    cosf, sinf = _rope_tables(T)
    h, hb = _layer_norm(x.reshape(B * T, D), None, ln_in_g, ln_in_b)
    w_in_p = _pack_w_in(w_in)
    w_out_b = w_out.astype(bf16)
    for l in range(depth):
        mix = _token_mixers(hb, B, T, l, w_in_p, gla_gate_w2[l], gla_gate_b[l], gla_norm_g[l], gdn_conv_w[l],
                            gdn_a_log[l], gdn_dt_bias[l], gdn_norm_g[l], cosf, sinf)
        y, w_up_b = _matmul(mix, w_out_b, l, tm=1024, tn=512, tk=w_out.shape[1], out_dtype=f32, cast=(w_up, l),
                            name='out_proj')
        h, hb = _layer_norm(h, y, ln1_g[l], ln1_b[l], alpha=alpha)
        up, w_down_b = _matmul(hb, w_up_b, 0, tm=1024, tn=1024, tk=D, out_dtype=bf16, act='relu2',
                               cast=(w_down, l), name='mlp_up')
        ff = _matmul(up, w_down_b, 0, tm=1024, tn=1024, tk=2048, out_dtype=f32, name='mlp_down')
        h, hb = _layer_norm(h, ff, ln2_g[l], ln2_b[l], alpha=alpha, with_bf16=l + 1 < depth)
    return h.reshape(B, T, D)
```
